```python
import math
import numpy as np
import jax
import jax.numpy as jnp
from jax import lax


D_MODEL = 2048
BATCH = 8
SEQ = 4096
DEPTH = 4

CHUNK = 64
N_EVEN = (DEPTH + 1) // 2
N_ODD = DEPTH // 2
ROPE_THETA = 500000.0
NORM_EPS = 1e-5

A_HEADS = 8
A_HEAD_DIM = 128
A_ROPE_DIM = A_HEAD_DIM // 4
A_NOPE_DIM = A_HEAD_DIM - A_ROPE_DIM
A_V_DIM = 128
A_WIDTH = A_HEADS * A_V_DIM
A_Q_RANK = 512
A_KV_RANK = 256
IDX_HEADS = 16
IDX_DIM = 64
IDX_ROPE_DIM = IDX_DIM // 4
TOPK_MAX = 256
Q_BLOCK = 128

B_HEAD = 64
B_WIDTH = D_MODEL // 2
B_HEADS = B_WIDTH // B_HEAD
B_DECAY_LORA = 64
B_A_LORA = 64
B_GATE_LORA = 160
B_LN_EPS = 64e-5

C_WIDTH = D_MODEL
C_GROUP = 16
C_GROUPS = C_WIDTH // C_GROUP
C_STATE = 64

MEM_LEN = 256
X_HEADS = 4
X_HEAD_DIM = 128

FFN_DIM = 4 * D_MODEL

A_COLS = A_Q_RANK + A_KV_RANK + A_ROPE_DIM + IDX_DIM + IDX_HEADS
B_COLS = 3 * B_WIDTH + B_DECAY_LORA + B_A_LORA + B_GATE_LORA
EVEN_COLS = A_COLS + B_COLS

kernel_name = 'hybrid_dsa_rwkv7_s5_stream_encoder'


def rms_norm(x, g):
    xf = x.astype(jnp.float32)
    y = xf * lax.rsqrt(jnp.mean(xf * xf, axis=-1, keepdims=True) + NORM_EPS)
    return (y * g.astype(jnp.float32)).astype(x.dtype)


def split_cols(t, sizes):
    offs = np.cumsum(sizes)[:-1].tolist()
    return jnp.split(t, offs, axis=-1)


def rope_tables(pos, rot_dim):
    inv_freq = ROPE_THETA ** (-jnp.arange(0, rot_dim, 2, dtype=jnp.float32) / rot_dim)
    ang = pos[..., None] * inv_freq
    return jnp.cos(ang), jnp.sin(ang)


def apply_partial_rope(x, cos, sin, rot_dim):
    xf = x.astype(jnp.float32)
    half = rot_dim // 2
    x1, x2, rest = xf[..., :half], xf[..., half:rot_dim], xf[..., rot_dim:]
    out = jnp.concatenate([x1 * cos - x2 * sin, x2 * cos + x1 * sin, rest], axis=-1)
    return out.astype(x.dtype)


def token_shift(t):
    return jnp.pad(t[:, :-1], ((0, 0), (1, 0), (0, 0)))


def dsa_mixer(h, cos_a, sin_a, cos_i, sin_i, cq_norm, ckv_norm, kidx_norm, w_uq, w_uk, w_uv, w_qidx):
    f32 = jnp.float32
    bsz, seq, _ = h.shape
    c_q, c_kv, k_rope, k_idx, w_idx = split_cols(h, [A_Q_RANK, A_KV_RANK, A_ROPE_DIM, IDX_DIM, IDX_HEADS])
    c_q = rms_norm(c_q, cq_norm)
    c_kv = rms_norm(c_kv, ckv_norm)
    k_rope = apply_partial_rope(k_rope, cos_a, sin_a, A_ROPE_DIM)
    q = jnp.einsum('bsr,rhd->bshd', c_q, w_uq)
    q_rope = apply_partial_rope(q[..., :A_ROPE_DIM], cos_a[:, :, None], sin_a[:, :, None], A_ROPE_DIM)
    q_lat = jnp.einsum('bshd,rhd->bshr', q[..., A_ROPE_DIM:], w_uk)
    q_idx = apply_partial_rope(jnp.einsum('bsr,rhd->bshd', c_q, w_qidx), cos_i[:, :, None], sin_i[:, :, None], IDX_ROPE_DIM)
    k_idx = apply_partial_rope(rms_norm(k_idx, kidx_norm), cos_i, sin_i, IDX_ROPE_DIM)
    w_idx = w_idx.astype(f32) * (IDX_HEADS ** -0.5) * (IDX_DIM ** -0.5)

    topk = min(TOPK_MAX, seq // 4)
    n_blk = seq // Q_BLOCK
    scale = A_HEAD_DIM ** -0.5
    key_pos = jnp.arange(seq, dtype=jnp.int32)
    gather = jax.vmap(lambda table, idx: table[idx])

    def to_blocks(t):
        return t.reshape((bsz, n_blk, Q_BLOCK) + t.shape[2:]).swapaxes(0, 1)

    def attend_block(args):
        ql, qr, qi, wi, t0 = args
        q_pos = t0 + jnp.arange(Q_BLOCK, dtype=jnp.int32)
        limit = (q_pos // CHUNK + 1) * CHUNK
        allowed = key_pos[None, :] < limit[:, None]
        logits = jnp.einsum('bqhd,bsd->bqhs', qi.astype(f32), k_idx.astype(f32))
        score = jnp.einsum('bqh,bqhs->bqs', wi, jax.nn.relu(logits))
        score = jnp.where(allowed[None], score, -jnp.inf)
        _, sel = lax.top_k(score, topk)
        valid = sel < limit[None, :, None]
        ckv_sel = gather(c_kv, sel)
        kr_sel = gather(k_rope, sel)
        s = (jnp.einsum('bqhr,bqkr->bqhk', ql, ckv_sel).astype(f32)
             + jnp.einsum('bqhd,bqkd->bqhk', qr, kr_sel).astype(f32)) * scale
        s = jnp.where(valid[:, :, None, :], s, -jnp.inf)
        p = jax.nn.softmax(s, axis=-1).astype(ckv_sel.dtype)
        return jnp.einsum('bqhk,bqkr->bqhr', p, ckv_sel)

    blk_start = jnp.arange(n_blk, dtype=jnp.int32) * Q_BLOCK
    o_lat = lax.map(attend_block, (to_blocks(q_lat), to_blocks(q_rope), to_blocks(q_idx), to_blocks(w_idx), blk_start))
    o_lat = o_lat.swapaxes(0, 1).reshape(bsz, seq, A_HEADS, A_KV_RANK)
    return jnp.einsum('bshr,rhd->bshd', o_lat, w_uv).reshape(bsz, seq, A_WIDTH)


def rwkv7_mixer(h, mu, w0, w2, a0, a2, g2, k_k, k_a, r_k, ln_w, ln_b):
    f32 = jnp.float32
    bsz, seq, _ = h.shape
    hm = h + (token_shift(h) - h) * mu
    r, k, v, wl, al, gl = split_cols(hm, [B_WIDTH, B_WIDTH, B_WIDTH, B_DECAY_LORA, B_A_LORA, B_GATE_LORA])
    log_w = -jax.nn.softplus(-(w0 + jnp.tanh(wl) @ w2).astype(f32)) - 0.5
    decay = jnp.exp(-jnp.exp(log_w))
    a = jax.nn.sigmoid((a0 + al @ a2).astype(f32))
    g = (jax.nn.sigmoid(gl) @ g2).astype(f32)

    def heads(t):
        return t.reshape(bsz, seq, B_HEADS, B_HEAD)

    r, k, v = heads(r.astype(f32)), heads(k.astype(f32)), heads(v.astype(f32))
    decay, a = heads(decay), heads(a)
    kk = k * k_k.astype(f32).reshape(B_HEADS, B_HEAD)
    kk = kk * lax.rsqrt(jnp.maximum(jnp.sum(kk * kk, axis=-1, keepdims=True), 1e-24))
    k = k * (1.0 + (a - 1.0) * k_a.astype(f32).reshape(B_HEADS, B_HEAD))

    def step(state, inp):
        r_t, w_t, k_t, v_t, kk_t, a_t = inp
        sa = jnp.einsum('bhvk,bhk->bhv', state, -kk_t)
        state = (state * w_t[:, :, None, :] + sa[..., None] * (kk_t * a_t)[:, :, None, :]
                 + v_t[..., None] * k_t[:, :, None, :])
        return state, jnp.einsum('bhvk,bhk->bhv', state, r_t)

    tm = lambda t: jnp.moveaxis(t, 1, 0)
    state0 = jnp.zeros((bsz, B_HEADS, B_HEAD, B_HEAD), f32)
    _, out = lax.scan(step, state0, (tm(r), tm(decay), tm(k), tm(v), tm(kk), tm(a)))
    out = jnp.moveaxis(out, 0, 1)
    mean = jnp.mean(out, axis=-1, keepdims=True)
    var = jnp.mean(jnp.square(out - mean), axis=-1, keepdims=True)
    out = ((out - mean) * lax.rsqrt(var + B_LN_EPS)).reshape(bsz, seq, B_WIDTH)
    out = out * ln_w.astype(f32) + ln_b.astype(f32)
    bonus = jnp.sum(r * k * r_k.astype(f32), axis=-1, keepdims=True) * v
    out = (out + bonus.reshape(bsz, seq, B_WIDTH)) * g
    return out.astype(h.dtype)


def s5_mixer(u, lam_re, lam_im, log_dt, b_re, b_im, c_re, c_im, d_skip, w_glu, b_glu):
    f32 = jnp.float32
    bsz, seq, _ = u.shape
    lam = lax.complex(jnp.minimum(lam_re.astype(f32), -1e-4), lam_im.astype(f32))
    dt = jnp.exp(log_dt.astype(f32))[:, None]
    lam_bar = jnp.exp(lam * dt)
    b_bar = ((lam_bar - 1.0) / lam)[..., None] * lax.complex(b_re.astype(f32), b_im.astype(f32))
    c_mat = lax.complex(c_re.astype(f32), c_im.astype(f32))
    n_chunk = seq // CHUNK
    uf = u.astype(f32)
    u_chunks = jnp.moveaxis(uf.reshape(bsz, n_chunk, CHUNK, C_GROUPS, C_GROUP), 1, 0)
    a_elem = jnp.broadcast_to(lam_bar, (bsz, CHUNK, C_GROUPS, C_STATE))

    def combine(e1, e2):
        a1, x1 = e1
        a2, x2 = e2
        return a1 * a2, a2 * x1 + x2

    def chunk_step(state, u_c):
        bu = jnp.einsum('bcgi,gpi->bcgp', u_c.astype(jnp.complex64), b_bar)
        bu = bu.at[:, 0].add(lam_bar * state)
        _, states = lax.associative_scan(combine, (a_elem, bu), axis=1)
        y = jnp.einsum('bcgp,gip->bcgi', states, c_mat).real
        return states[:, -1], y

    state0 = jnp.zeros((bsz, C_GROUPS, C_STATE), jnp.complex64)
    _, y = lax.scan(chunk_step, state0, u_chunks)
    y = jnp.moveaxis(y, 0, 1).reshape(bsz, seq, C_WIDTH) + d_skip.astype(f32) * uf
    z = jax.nn.gelu(y)
    out = z * jax.nn.sigmoid(z @ w_glu.astype(f32) + b_glu.astype(f32))
    return out.astype(u.dtype)


def memory_cross_attention(xn, memn, wq, wkv, wo):
    f32 = jnp.float32
    bsz, seq, _ = xn.shape
    q = (xn @ wq).reshape(bsz, seq, X_HEADS, X_HEAD_DIM)
    kv = (memn @ wkv).reshape(bsz, memn.shape[1], 2, X_HEADS, X_HEAD_DIM)
    s = jnp.einsum('bshd,bmhd->bshm', q, kv[:, :, 0]).astype(f32) * (X_HEAD_DIM ** -0.5)
    p = jax.nn.softmax(s, axis=-1).astype(xn.dtype)
    o = jnp.einsum('bshm,bmhd->bshd', p, kv[:, :, 1]).reshape(bsz, seq, X_HEADS * X_HEAD_DIM)
    return o @ wo


def squared_relu_mlp(xn, w_up, w_down):
    return jnp.square(jax.nn.relu(xn @ w_up)) @ w_down


def setup_inputs(seed: int = 0) -> dict:
    key = jax.random.key(seed)
    ks = iter(jax.random.split(key, 64))
    f32 = jnp.float32

    def nrm(shape, scale):
        return scale * jax.random.normal(next(ks), shape, f32)

    def gain(shape):
        return 1.0 + 0.02 * jax.random.normal(next(ks), shape, f32)

    ramp = (jnp.arange(B_WIDTH, dtype=f32) / (B_WIDTH - 1)) ** 0.85
    inp = {}
    inp['x'] = nrm((BATCH, SEQ, D_MODEL), 1.0)
    inp['mem'] = nrm((BATCH, MEM_LEN, D_MODEL), 1.0)
    inp['start_frame'] = jax.random.randint(next(ks), (BATCH,), 0, 256, dtype=jnp.int32) * CHUNK
    inp['norm_mix'] = gain((DEPTH, D_MODEL))
    inp['norm_xattn'] = gain((DEPTH, D_MODEL))
    inp['norm_mem'] = gain((DEPTH, D_MODEL))
    inp['norm_ffn'] = gain((DEPTH, D_MODEL))
    inp['final_norm'] = gain((D_MODEL,))
    inp['xattn_wq'] = nrm((DEPTH, D_MODEL, X_HEADS * X_HEAD_DIM), D_MODEL ** -0.5)
    inp['xattn_wkv'] = nrm((DEPTH, D_MODEL, 2 * X_HEADS * X_HEAD_DIM), D_MODEL ** -0.5)
    inp['xattn_wo'] = nrm((DEPTH, X_HEADS * X_HEAD_DIM, D_MODEL), (X_HEADS * X_HEAD_DIM) ** -0.5)
    inp['ffn_up'] = nrm((DEPTH, D_MODEL, FFN_DIM), D_MODEL ** -0.5)
    inp['ffn_down'] = nrm((DEPTH, FFN_DIM, D_MODEL), FFN_DIM ** -0.5)
    inp['even_w_in'] = nrm((N_EVEN, D_MODEL, EVEN_COLS), D_MODEL ** -0.5)
    inp['even_w_out'] = nrm((N_EVEN, A_WIDTH + B_WIDTH, D_MODEL), (A_WIDTH + B_WIDTH) ** -0.5)
    inp['dsa_cq_norm'] = gain((N_EVEN, A_Q_RANK))
    inp['dsa_ckv_norm'] = gain((N_EVEN, A_KV_RANK))
    inp['dsa_kidx_norm'] = gain((N_EVEN, IDX_DIM))
    inp['dsa_w_uq'] = nrm((N_EVEN, A_Q_RANK, A_HEADS, A_HEAD_DIM), A_Q_RANK ** -0.5)
    inp['dsa_w_uk'] = nrm((N_EVEN, A_KV_RANK, A_HEADS, A_NOPE_DIM), A_KV_RANK ** -0.5)
    inp['dsa_w_uv'] = nrm((N_EVEN, A_KV_RANK, A_HEADS, A_V_DIM), A_KV_RANK ** -0.5)
    inp['dsa_w_qidx'] = nrm((N_EVEN, A_Q_RANK, IDX_HEADS, IDX_DIM), A_Q_RANK ** -0.5)
    inp['rwkv_mu'] = jax.random.uniform(next(ks), (N_EVEN, B_COLS), f32)
    inp['rwkv_w0'] = -6.0 + 5.0 * ramp + nrm((N_EVEN, B_WIDTH), 0.1)
    inp['rwkv_w2'] = nrm((N_EVEN, B_DECAY_LORA, B_WIDTH), 0.5 * B_DECAY_LORA ** -0.5)
    inp['rwkv_a0'] = nrm((N_EVEN, B_WIDTH), 0.1)
    inp['rwkv_a2'] = nrm((N_EVEN, B_A_LORA, B_WIDTH), B_A_LORA ** -0.5)
    inp['rwkv_g2'] = nrm((N_EVEN, B_GATE_LORA, B_WIDTH), B_GATE_LORA ** -0.5)
    inp['rwkv_k_k'] = 0.85 + nrm((N_EVEN, B_WIDTH), 0.02)
    inp['rwkv_k_a'] = gain((N_EVEN, B_WIDTH))
    inp['rwkv_r_k'] = -0.04 + nrm((N_EVEN, B_HEADS, B_HEAD), 0.1)
    inp['rwkv_ln_w'] = gain((N_EVEN, B_WIDTH))
    inp['rwkv_ln_b'] = nrm((N_EVEN, B_WIDTH), 0.02)
    inp['odd_w_in'] = nrm((N_ODD, D_MODEL, C_WIDTH), D_MODEL ** -0.5)
    inp['odd_w_out'] = nrm((N_ODD, C_WIDTH, D_MODEL), C_WIDTH ** -0.5)
    inp['s5_lam_re'] = -0.5 + nrm((N_ODD, C_GROUPS, C_STATE), 0.01)
    inp['s5_lam_im'] = math.pi * jnp.arange(C_STATE, dtype=f32) + nrm((N_ODD, C_GROUPS, C_STATE), 0.01)
    inp['s5_log_dt'] = jax.random.uniform(next(ks), (N_ODD, C_GROUPS), f32, math.log(1e-3), math.log(1e-1))
    inp['s5_b_re'] = nrm((N_ODD, C_GROUPS, C_STATE, C_GROUP), (2 * C_GROUP) ** -0.5)
    inp['s5_b_im'] = nrm((N_ODD, C_GROUPS, C_STATE, C_GROUP), (2 * C_GROUP) ** -0.5)
    inp['s5_c_re'] = nrm((N_ODD, C_GROUPS, C_GROUP, C_STATE), C_STATE ** -0.5)
    inp['s5_c_im'] = nrm((N_ODD, C_GROUPS, C_GROUP, C_STATE), C_STATE ** -0.5)
    inp['s5_d'] = nrm((N_ODD, C_WIDTH), 1.0)
    inp['s5_w_glu'] = nrm((N_ODD, C_WIDTH, C_WIDTH), C_WIDTH ** -0.5)
    inp['s5_b_glu'] = nrm((N_ODD, C_WIDTH), 0.02)
    return inp


def reference(x, mem, start_frame, norm_mix, norm_xattn, norm_mem, norm_ffn, final_norm,
              xattn_wq, xattn_wkv, xattn_wo, ffn_up, ffn_down, even_w_in, even_w_out,
              dsa_cq_norm, dsa_ckv_norm, dsa_kidx_norm, dsa_w_uq, dsa_w_uk, dsa_w_uv, dsa_w_qidx,
              rwkv_mu, rwkv_w0, rwkv_w2, rwkv_a0, rwkv_a2, rwkv_g2, rwkv_k_k, rwkv_k_a, rwkv_r_k,
              rwkv_ln_w, rwkv_ln_b, odd_w_in, odd_w_out, s5_lam_re, s5_lam_im, s5_log_dt,
              s5_b_re, s5_b_im, s5_c_re, s5_c_im, s5_d, s5_w_glu, s5_b_glu):
    bsz, seq, _ = x.shape
    pos = (start_frame[:, None] + jnp.arange(seq, dtype=jnp.int32)[None, :]).astype(jnp.float32)
    cos_a, sin_a = rope_tables(pos, A_ROPE_DIM)
    cos_i, sin_i = rope_tables(pos, IDX_ROPE_DIM)
    for layer in range(DEPTH):
        i = layer // 2
        xn = rms_norm(x, norm_mix[layer])
        if layer % 2 == 0:
            h = xn @ even_w_in[i]
            y_a = dsa_mixer(h[..., :A_COLS], cos_a, sin_a, cos_i, sin_i, dsa_cq_norm[i], dsa_ckv_norm[i],
                            dsa_kidx_norm[i], dsa_w_uq[i], dsa_w_uk[i], dsa_w_uv[i], dsa_w_qidx[i])
            y_b = rwkv7_mixer(h[..., A_COLS:], rwkv_mu[i], rwkv_w0[i], rwkv_w2[i], rwkv_a0[i], rwkv_a2[i],
                              rwkv_g2[i], rwkv_k_k[i], rwkv_k_a[i], rwkv_r_k[i], rwkv_ln_w[i], rwkv_ln_b[i])
            mix = jnp.concatenate([y_a, y_b], axis=-1) @ even_w_out[i]
        else:
            y_c = s5_mixer(xn @ odd_w_in[i], s5_lam_re[i], s5_lam_im[i], s5_log_dt[i], s5_b_re[i], s5_b_im[i],
                           s5_c_re[i], s5_c_im[i], s5_d[i], s5_w_glu[i], s5_b_glu[i])
            mix = y_c @ odd_w_out[i]
        x = x + mix
        x = x + memory_cross_attention(rms_norm(x, norm_xattn[layer]), rms_norm(mem, norm_mem[layer]),
                                       xattn_wq[layer], xattn_wkv[layer], xattn_wo[layer])
        x = x + squared_relu_mlp(rms_norm(x, norm_ffn[layer]), ffn_up[layer], ffn_down[layer])
    return rms_norm(x, final_norm)
```

```python
import functools
import math

import numpy as np
import jax
import jax.numpy as jnp
from jax import lax
from jax.experimental import pallas as pl
from jax.experimental.pallas import tpu as pltpu

F32 = jnp.float32
BF16 = jnp.bfloat16
I32 = jnp.int32

D_MODEL = 2048
CHUNK = 64
ROPE_THETA = 500000.0
NORM_EPS = 1e-5

A_HEADS = 8
A_HEAD_DIM = 128
A_ROPE_DIM = 32
A_V_DIM = 128
A_WIDTH = A_HEADS * A_V_DIM
A_Q_RANK = 512
A_KV_RANK = 256
IDX_HEADS = 16
IDX_DIM = 64
IDX_ROPE_DIM = 16
TOPK_MAX = 256

B_HEAD = 64
B_WIDTH = 1024
B_HEADS = 16
B_DECAY_LORA = 64
B_A_LORA = 64
B_GATE_LORA = 160
B_LN_EPS = 64e-5

C_WIDTH = 2048
C_GROUP = 16
C_GROUPS = 128
C_STATE = 64

X_HEADS = 4
X_HEAD_DIM = 128
FFN_DIM = 4 * D_MODEL

LANES = 128
VMEM_LIMIT = 56 * 1024 * 1024

A_PAD_COLS = A_Q_RANK + A_KV_RANK + 3 * LANES
B_LORA_COLS = 2 * LANES + 2 * LANES
B_PAD_COLS = 3 * B_WIDTH + B_LORA_COLS
KCAT = A_KV_RANK + LANES

INT_MIN = -2147483648
NEG_BIG = -1e30


def _cparams(sem):
    return pltpu.CompilerParams(dimension_semantics=sem, vmem_limit_bytes=VMEM_LIMIT)


def _rms(x, g):
    return x * lax.rsqrt(jnp.mean(x * x, axis=-1, keepdims=True) + NORM_EPS) * g


def _dot(a, b):
    return jnp.dot(a, b, preferred_element_type=F32)


def _dot_nt(a, b):
    return lax.dot_general(a, b, (((1,), (1,)), ((), ())), preferred_element_type=F32)


def _dot_tn(a, b):
    return lax.dot_general(a, b, (((0,), (0,)), ((), ())), preferred_element_type=F32)


def _split2(x):
    hi = x.astype(BF16)
    lo = (x - hi.astype(F32)).astype(BF16)
    return hi, lo


def _dotp(a, b, passes, kind="nn"):
    f = {"nn": _dot, "nt": _dot_nt, "tn": _dot_tn}[kind]
    if passes == 1:
        return f(a.astype(BF16), b.astype(BF16))
    ah, al = _split2(a)
    bh, bl = _split2(b)
    return f(ah, bh) + (f(ah, bl) + f(al, bh))


def _mm_body(*refs, n_row, n_const, n_epi, prologue, epilogue):
    rows = refs[:n_row]
    consts = refs[n_row:n_row + n_const]
    w_ref = refs[n_row + n_const]
    epis = refs[n_row + n_const + 1:n_row + n_const + 1 + n_epi]
    o_ref, xs_ref = refs[-2], refs[-1]

    @pl.when(pl.program_id(1) == 0)
    def _():
        xs_ref[...] = prologue(*[r[...] for r in rows], *[c[...] for c in consts]).astype(BF16)

    acc = _dot(xs_ref[...], w_ref[...])
    o_ref[...] = epilogue(acc, *[e[...] for e in epis]).astype(o_ref.dtype)


def _matmul(rows, consts, w, epis, *, prologue, epilogue, out_dtype, tm, tn, name):
    m = rows[0].shape[0]
    k, n = w.shape
    tm = min(tm, m)
    tn = min(tn, n)
    assert m % tm == 0 and n % tn == 0, (m, tm, n, tn)
    in_specs = [pl.BlockSpec((tm, r.shape[1]), lambda i, j: (i, 0)) for r in rows]
    in_specs += [pl.BlockSpec(c.shape, lambda i, j: (0, 0)) for c in consts]
    in_specs += [pl.BlockSpec((k, tn), lambda i, j: (0, j))]
    for _, kind in epis:
        if kind == "tile":
            in_specs.append(pl.BlockSpec((tm, tn), lambda i, j: (i, j)))
        else:
            in_specs.append(pl.BlockSpec((1, tn), lambda i, j: (0, j)))
    body = functools.partial(_mm_body, n_row=len(rows), n_const=len(consts), n_epi=len(epis),
                             prologue=prologue, epilogue=epilogue)
    return pl.pallas_call(
        body,
        grid=(m // tm, n // tn),
        in_specs=in_specs,
        out_specs=pl.BlockSpec((tm, tn), lambda i, j: (i, j)),
        out_shape=jax.ShapeDtypeStruct((m, n), out_dtype),
        scratch_shapes=[pltpu.VMEM((tm, k), BF16)],
        compiler_params=_cparams(("parallel", "arbitrary")),
        name=name,
    )(*rows, *consts, w, *[e for e, _ in epis])


def _epi_id(acc):
    return acc


def _epi_add(acc, res):
    return acc + res


def _pro_id(x):
    return x


def _ffn_body(x_ref, g_ref, up_ref, dn_ref, o_ref, xs_ref):
    @pl.when(pl.program_id(1) == 0)
    def _():
        x = x_ref[...]
        xs_ref[...] = _rms(x, g_ref[...]).astype(BF16)
        o_ref[...] = x

    h = _dot(xs_ref[...], up_ref[...])
    h = jnp.square(jnp.maximum(h, 0.0)).astype(BF16)
    o_ref[...] += _dot(h, dn_ref[...])


def _ffn(x, g, w_up, w_dn, *, tm=512, tf=512):
    m, d = x.shape
    f = w_up.shape[1]
    tm = min(tm, m)
    return pl.pallas_call(
        _ffn_body,
        grid=(m // tm, f // tf),
        in_specs=[pl.BlockSpec((tm, d), lambda i, j: (i, 0)),
                  pl.BlockSpec((1, d), lambda i, j: (0, 0)),
                  pl.BlockSpec((d, tf), lambda i, j: (0, j)),
                  pl.BlockSpec((tf, d), lambda i, j: (j, 0))],
        out_specs=pl.BlockSpec((tm, d), lambda i, j: (i, 0)),
        out_shape=jax.ShapeDtypeStruct((m, d), F32),
        scratch_shapes=[pltpu.VMEM((tm, d), BF16)],
        compiler_params=_cparams(("parallel", "arbitrary")),
        name="ffn",
    )(x, g, w_up, w_dn)


def _xattn_body(x_ref, g_ref, wq_ref, kv_ref, wo_ref, o_ref):
    x = x_ref[0]
    xn = _rms(x, g_ref[...]).astype(BF16)
    q = _dot(xn, wq_ref[...])
    width = X_HEADS * X_HEAD_DIM
    outs = []
    for h in range(X_HEADS):
        sl = slice(h * X_HEAD_DIM, (h + 1) * X_HEAD_DIM)
        qh = q[:, sl].astype(BF16)
        kh = kv_ref[0, :, sl]
        vh = kv_ref[0, :, width + h * X_HEAD_DIM:width + (h + 1) * X_HEAD_DIM]
        s = _dot_nt(qh, kh) * (X_HEAD_DIM ** -0.5)
        s = s - jnp.max(s, axis=-1, keepdims=True)
        p = jnp.exp(s)
        p = p / jnp.sum(p, axis=-1, keepdims=True)
        outs.append(_dot(p.astype(BF16), vh))
    o = jnp.concatenate(outs, axis=-1).astype(BF16)
    o_ref[0] = x + _dot(o, wo_ref[...])


def _xattn(x3, g, wq, kv, wo, *, tm=512):
    b, s, d = x3.shape
    tm = min(tm, s)
    mlen = kv.shape[1]
    width = X_HEADS * X_HEAD_DIM
    return pl.pallas_call(
        _xattn_body,
        grid=(b, s // tm),
        in_specs=[pl.BlockSpec((1, tm, d), lambda bi, i: (bi, i, 0)),
                  pl.BlockSpec((1, d), lambda bi, i: (0, 0)),
                  pl.BlockSpec((d, width), lambda bi, i: (0, 0)),
                  pl.BlockSpec((1, mlen, 2 * width), lambda bi, i: (bi, 0, 0)),
                  pl.BlockSpec((width, d), lambda bi, i: (0, 0))],
        out_specs=pl.BlockSpec((1, tm, d), lambda bi, i: (bi, i, 0)),
        out_shape=jax.ShapeDtypeStruct((b, s, d), F32),
        compiler_params=_cparams(("parallel", "parallel")),
        name="xattn",
    )(x3, g, wq, kv, wo)


def _rope_tables():
    inv_a = ROPE_THETA ** (-jnp.arange(0, A_ROPE_DIM, 2, dtype=F32) / A_ROPE_DIM)
    inv_i = ROPE_THETA ** (-jnp.arange(0, IDX_ROPE_DIM, 2, dtype=F32) / IDX_ROPE_DIM)
    ha, hi = A_ROPE_DIM // 2, IDX_ROPE_DIM // 2

    def row(*pieces):
        r = jnp.concatenate([jnp.asarray(p, F32) for p in pieces])
        return jnp.pad(r, (0, LANES - r.shape[0]))

    ones, zeros = (lambda n: jnp.ones((n,), F32)), (lambda n: jnp.zeros((n,), F32))
    return jnp.stack([row(inv_a, inv_a), row(ones(2 * ha)), row(-ones(ha)), row(zeros(ha), ones(ha)),
                      row(inv_i, inv_i), row(-ones(hi)), row(zeros(hi), ones(hi)), zeros(LANES)])


def _dsa_proj_body(sf_ref, h_ref, tab_ref, nq_ref, nkv_ref, nki_ref, wuq_ref, wuk_ref, wqi_ref,
                   qcat_ref, kcat_ref, qidx_ref, kidx_ref, widx_ref):
    bi, i = pl.program_id(0), pl.program_id(1)
    tm = h_ref.shape[1]
    ha, hi = A_ROPE_DIM // 2, IDX_ROPE_DIM // 2
    pos = (sf_ref[bi] + i * tm + lax.broadcasted_iota(I32, (tm, LANES), 0)).astype(F32)
    tab = tab_ref[...]
    ang_a = pos * tab[0:1]
    cos_a = jnp.cos(ang_a) * tab[1:2]
    sin_a = jnp.sin(ang_a)
    sin_a1, sin_a2 = sin_a * tab[2:3], sin_a * tab[3:4]
    ang_i = pos * tab[4:5]
    cos_i = jnp.cos(ang_i)
    sin_i = jnp.sin(ang_i)
    sin_i1, sin_i2 = sin_i * tab[5:6], sin_i * tab[6:7]

    def rope_a(xb):
        return xb * cos_a + pltpu.roll(xb, LANES - ha, 1) * sin_a1 + pltpu.roll(xb, ha, 1) * sin_a2

    def rope_i(xb):
        return xb * cos_i + pltpu.roll(xb, LANES - hi, 1) * sin_i1 + pltpu.roll(xb, hi, 1) * sin_i2

    c_q = h_ref[0, :, 0:A_Q_RANK]
    c_kv = h_ref[0, :, A_Q_RANK:A_Q_RANK + A_KV_RANK]
    off = A_Q_RANK + A_KV_RANK
    k_rope = h_ref[0, :, off:off + LANES]
    k_idx = h_ref[0, :, off + LANES:off + 2 * LANES]
    w_idx = h_ref[0, :, off + 2 * LANES:off + 3 * LANES]

    cqn = _rms(c_q, nq_ref[...]).astype(BF16)
    kcat_ref[0, :, 0:A_KV_RANK] = _rms(c_kv, nkv_ref[...]).astype(BF16)
    kcat_ref[0, :, A_KV_RANK:KCAT] = rope_a(k_rope).astype(BF16)

    ki = k_idx * lax.rsqrt(jnp.sum(k_idx * k_idx, axis=-1, keepdims=True) * (1.0 / IDX_DIM) + NORM_EPS)
    kidx_ref[0] = rope_i(ki * nki_ref[...]).astype(BF16)
    widx_ref[0] = w_idx * ((IDX_HEADS ** -0.5) * (IDX_DIM ** -0.5))

    q = _dot(cqn, wuq_ref[...])
    for h in range(A_HEADS):
        qh = q[:, h * A_HEAD_DIM:(h + 1) * A_HEAD_DIM]
        qcat_ref[0, h, :, 0:A_KV_RANK] = _dot(qh.astype(BF16), wuk_ref[h]).astype(BF16)
        qcat_ref[0, h, :, A_KV_RANK:KCAT] = rope_a(qh).astype(BF16)
    qi = _dot(cqn, wqi_ref[...])
    for h in range(IDX_HEADS):
        qidx_ref[0, h] = rope_i(qi[:, h * LANES:(h + 1) * LANES]).astype(BF16)


def _dsa_proj(h_a, start_frame, tab, nq, nkv, nki, wuq, wuk_t, wqi, *, tm=256):
    b, s, _ = h_a.shape
    tm = min(tm, s)
    c3 = lambda bi, i, sf: (bi, i, 0)
    c4 = lambda bi, i, sf: (bi, 0, i, 0)
    z2 = lambda bi, i, sf: (0, 0)
    z3 = lambda bi, i, sf: (0, 0, 0)
    grid_spec = pltpu.PrefetchScalarGridSpec(
        num_scalar_prefetch=1,
        grid=(b, s // tm),
        in_specs=[pl.BlockSpec((1, tm, A_PAD_COLS), c3),
                  pl.BlockSpec(tab.shape, z2),
                  pl.BlockSpec(nq.shape, z2),
                  pl.BlockSpec(nkv.shape, z2),
                  pl.BlockSpec(nki.shape, z2),
                  pl.BlockSpec(wuq.shape, z2),
                  pl.BlockSpec(wuk_t.shape, z3),
                  pl.BlockSpec(wqi.shape, z2)],
        out_specs=[pl.BlockSpec((1, A_HEADS, tm, KCAT), c4),
                   pl.BlockSpec((1, tm, KCAT), c3),
                   pl.BlockSpec((1, IDX_HEADS, tm, LANES), c4),
                   pl.BlockSpec((1, tm, LANES), c3),
                   pl.BlockSpec((1, tm, LANES), c3)],
    )
    return pl.pallas_call(
        _dsa_proj_body,
        grid_spec=grid_spec,
        out_shape=[jax.ShapeDtypeStruct((b, A_HEADS, s, KCAT), BF16),
                   jax.ShapeDtypeStruct((b, s, KCAT), BF16),
                   jax.ShapeDtypeStruct((b, IDX_HEADS, s, LANES), BF16),
                   jax.ShapeDtypeStruct((b, s, LANES), BF16),
                   jax.ShapeDtypeStruct((b, s, LANES), F32)],
        compiler_params=_cparams(("parallel", "parallel")),
        name="dsa_proj",
    )(start_frame, h_a, tab, nq, nkv, nki, wuq, wuk_t, wqi)


def _dsa_attend_body(qidx_ref, widx_ref, kidx_ref, qcat_ref, kcat_ref, wuv_ref, o_ref,
                     key_ref, m_ref, l_ref, acc_ref, *, tq, kt, topk):
    i = pl.program_id(1)
    t0 = i * tq
    nk = (t0 + tq + kt - 1) // kt
    row = lax.broadcasted_iota(I32, (tq, kt), 0)
    col = lax.broadcasted_iota(I32, (tq, kt), 1)
    limit = ((t0 + row) // CHUNK + 1) * CHUNK

    qi = qidx_ref[0].reshape(IDX_HEADS * tq, LANES)
    wq = widx_ref[0]

    def score_tile(j, carry):
        off = pl.multiple_of(j * kt, kt)
        kid = kidx_ref[0, pl.ds(off, kt), :]
        lg = _dot_nt(qi, kid)
        sc = jnp.zeros((tq, kt), F32)
        for h in range(IDX_HEADS):
            sc = sc + wq[:, h:h + 1] * jnp.maximum(lg[h * tq:(h + 1) * tq], 0.0)
        bits = pltpu.bitcast(sc, I32)
        key = jnp.where(bits >= 0, bits, bits ^ 0x7FFFFFFF)
        key_ref[j] = jnp.where(off + col < limit, key, INT_MIN)
        return carry

    lax.fori_loop(0, nk, score_tile, 0)

    def count_ge(cand):
        def body(j, acc):
            hit = jnp.where(key_ref[j] >= cand, 1.0, 0.0)
            part = hit[:, 0:LANES]
            for c in range(1, kt // LANES):
                part = part + hit[:, c * LANES:(c + 1) * LANES]
            return acc + part
        acc = lax.fori_loop(0, nk, body, jnp.zeros((tq, LANES), F32))
        return jnp.sum(acc, axis=1, keepdims=True)

    kf = float(topk)
    thr = jnp.where(count_ge(jnp.zeros((tq, 1), I32)) >= kf, 0, INT_MIN).astype(I32)

    def bit_step(it, thr):
        cand = thr | jnp.left_shift(jnp.int32(1), 30 - it)
        return jnp.where(count_ge(cand) >= kf, cand, thr)

    thr = lax.fori_loop(0, 31, bit_step, thr)
    thr = jnp.maximum(thr, INT_MIN + 1)

    qc = qcat_ref[0].reshape(A_HEADS * tq, KCAT)
    m_ref[...] = jnp.full(m_ref.shape, NEG_BIG, F32)
    l_ref[...] = jnp.zeros(l_ref.shape, F32)
    acc_ref[...] = jnp.zeros(acc_ref.shape, F32)
    scale = A_HEAD_DIM ** -0.5

    def attend_tile(j, carry):
        off = pl.multiple_of(j * kt, kt)
        kc = kcat_ref[0, pl.ds(off, kt), :]
        s = _dot_nt(qc, kc) * scale
        sel = key_ref[j] >= thr
        s3 = jnp.where(sel[None], s.reshape(A_HEADS, tq, kt), NEG_BIG)
        m_old = m_ref[...]
        m_new = jnp.maximum(m_old, jnp.max(s3, axis=-1, keepdims=True))
        p = jnp.where(sel[None], jnp.exp(s3 - m_new), 0.0)
        alpha = jnp.exp(m_old - m_new)
        l_ref[...] = alpha * l_ref[...] + jnp.sum(p, axis=-1, keepdims=True)
        pv = _dot(p.reshape(A_HEADS * tq, kt).astype(BF16), kc[:, 0:A_KV_RANK])
        acc_ref[...] = alpha * acc_ref[...] + pv.reshape(A_HEADS, tq, A_KV_RANK)
        m_ref[...] = m_new
        return carry

    lax.fori_loop(0, nk, attend_tile, 0)

    o_lat = acc_ref[...] / l_ref[...]
    for h in range(A_HEADS):
        o_ref[0, :, h * A_V_DIM:(h + 1) * A_V_DIM] = _dot(o_lat[h].astype(BF16), wuv_ref[h]).astype(o_ref.dtype)


def _dsa_attend(qidx, widx, kidx, qcat, kcat, wuv, *, tq=128, kt=512):
    b, _, s, _ = qidx.shape
    kt = min(kt, s)
    topk = min(TOPK_MAX, s // 4)
    assert s % kt == 0 and s % tq == 0 and tq % CHUNK == 0
    body = functools.partial(_dsa_attend_body, tq=tq, kt=kt, topk=topk)
    return pl.pallas_call(
        body,
        grid=(b, s // tq),
        in_specs=[pl.BlockSpec((1, IDX_HEADS, tq, LANES), lambda bi, i: (bi, 0, i, 0)),
                  pl.BlockSpec((1, tq, LANES), lambda bi, i: (bi, i, 0)),
                  pl.BlockSpec((1, s, LANES), lambda bi, i: (bi, 0, 0)),
                  pl.BlockSpec((1, A_HEADS, tq, KCAT), lambda bi, i: (bi, 0, i, 0)),
                  pl.BlockSpec((1, s, KCAT), lambda bi, i: (bi, 0, 0)),
                  pl.BlockSpec(wuv.shape, lambda bi, i: (0, 0, 0))],
        out_specs=pl.BlockSpec((1, tq, A_WIDTH), lambda bi, i: (bi, i, 0)),
        out_shape=jax.ShapeDtypeStruct((b, s, A_WIDTH), BF16),
        scratch_shapes=[pltpu.VMEM((s // kt, tq, kt), I32),
                        pltpu.VMEM((A_HEADS, tq, 1), F32),
                        pltpu.VMEM((A_HEADS, tq, 1), F32),
                        pltpu.VMEM((A_HEADS, tq, A_KV_RANK), F32)],
        compiler_params=_cparams(("parallel", "arbitrary")),
        name="dsa_attend",
    )(qidx, widx, kidx, qcat, kcat, wuv)


def _rwkv_prep_body(h_ref, prev_ref, mu_ref, w0_ref, a0_ref, w2_ref, a2_ref, g2_ref,
                    rkv_ref, ld_ref, a_ref, g_ref):
    i = pl.program_id(1)
    h = h_ref[0]
    tm = h.shape[0]
    prev = jnp.where(i > 0, prev_ref[0, 7:8, :], 0.0)
    row = lax.broadcasted_iota(I32, h.shape, 0)
    shifted = jnp.where(row == 0, prev, pltpu.roll(h, 1, 0))
    hm = h + (shifted - h) * mu_ref[...]
    w3 = 3 * B_WIDTH
    rkv_ref[0] = hm[:, 0:w3]
    wl = hm[:, w3:w3 + LANES]
    al = hm[:, w3 + LANES:w3 + 2 * LANES]
    gl = hm[:, w3 + 2 * LANES:w3 + 4 * LANES]
    wz = w0_ref[...] + _dot(jnp.tanh(wl).astype(BF16), w2_ref[...])
    log_w = -jax.nn.softplus(-wz) - 0.5
    ld_ref[0] = -jnp.exp(log_w)
    a_ref[0] = jax.nn.sigmoid(a0_ref[...] + _dot(al.astype(BF16), a2_ref[...]))
    g_ref[0] = _dot(jax.nn.sigmoid(gl).astype(BF16), g2_ref[...])


def _rwkv_prep(h_b, mu, w0, a0, w2, a2, g2, *, tm=256):
    b, s, cols = h_b.shape
    tm = min(tm, s)
    c3 = lambda bi, i: (bi, i, 0)
    z2 = lambda bi, i: (0, 0)
    prev_map = lambda bi, i: (bi, jnp.maximum(i * (tm // 8) - 1, 0), 0)
    out = lambda w: jax.ShapeDtypeStruct((b, s, w), F32)
    return pl.pallas_call(
        _rwkv_prep_body,
        grid=(b, s // tm),
        in_specs=[pl.BlockSpec((1, tm, cols), c3),
                  pl.BlockSpec((1, 8, cols), prev_map),
                  pl.BlockSpec(mu.shape, z2), pl.BlockSpec(w0.shape, z2), pl.BlockSpec(a0.shape, z2),
                  pl.BlockSpec(w2.shape, z2), pl.BlockSpec(a2.shape, z2), pl.BlockSpec(g2.shape, z2)],
        out_specs=[pl.BlockSpec((1, tm, 3 * B_WIDTH), c3), pl.BlockSpec((1, tm, B_WIDTH), c3),
                   pl.BlockSpec((1, tm, B_WIDTH), c3), pl.BlockSpec((1, tm, B_WIDTH), c3)],
        out_shape=[out(3 * B_WIDTH), out(B_WIDTH), out(B_WIDTH), out(B_WIDTH)],
        compiler_params=_cparams(("parallel", "parallel")),
        name="rwkv_prep",
    )(h_b, h_b, mu, w0, a0, w2, a2, g2)


RWKV_CHUNK = 64
RWKV_SUB = 16
RWKV_PASSES = 1


def _unit_lower_inverse(l_mat, eye, same_blk, passes):
    mm = lambda x, y: _dotp(x, y, passes)
    ld = jnp.where(same_blk, l_mat, 0.0)
    lo = l_mat - ld
    x = eye + ld
    p = mm(ld, ld)
    x = x + mm(x, p)
    p = mm(p, p)
    x = x + mm(x, p)
    p = mm(p, p)
    d = x + mm(x, p)
    n = mm(d, lo)
    y = eye + n
    y = y + mm(y, mm(n, n))
    return mm(y, d)


def _rwkv_scan_body(r_ref, k_ref, v_ref, ld_ref, a_ref, g_ref, par_ref, o_ref, st_ref, *, passes):
    c = RWKV_CHUNK
    tblk = r_ref.shape[1]

    @pl.when(pl.program_id(2) == 0)
    def _():
        st_ref[...] = jnp.zeros(st_ref.shape, F32)

    ri = lax.broadcasted_iota(I32, (c, c), 0)
    ci = lax.broadcasted_iota(I32, (c, c), 1)
    eye = jnp.where(ri == ci, 1.0, 0.0).astype(F32)
    lower = ri >= ci
    strict = ri > ci
    same_blk = (ri // RWKV_SUB) == (ci // RWKV_SUB)
    tri = jnp.where(lower, 1.0, 0.0).astype(BF16)
    mm = lambda x, y, kind="nn": _dotp(x, y, passes, kind)

    for hd in range(LANES // B_HEAD):
        sl = slice(hd * B_HEAD, (hd + 1) * B_HEAD)
        k_k, k_a, r_k = par_ref[0:1, sl], par_ref[1:2, sl], par_ref[2:3, sl]
        ln_w, ln_b = par_ref[3:4, sl], par_ref[4:5, sl]
        st = st_ref[hd]
        for ch in range(tblk // c):
            rs = slice(ch * c, (ch + 1) * c)
            r, k, v = r_ref[0, rs, sl], k_ref[0, rs, sl], v_ref[0, rs, sl]
            ld, a, g = ld_ref[0, rs, sl], a_ref[0, rs, sl], g_ref[0, rs, sl]
            kk = k * k_k
            kk = kk * lax.rsqrt(jnp.maximum(jnp.sum(kk * kk, axis=-1, keepdims=True), 1e-24))
            km = k * (1.0 + (a - 1.0) * k_a)
            bv = kk * a
            l1 = ld.astype(BF16)
            rem = ld - l1.astype(F32)
            l2 = rem.astype(BF16)
            l3 = (rem - l2.astype(F32)).astype(BF16)
            cs = _dot(tri, l1) + (_dot(tri, l2) + _dot(tri, l3))
            cs_end = cs[c - 1:c, :]
            e_pos = jnp.exp(cs)
            e_neg = jnp.exp(-cs)
            e_end = jnp.exp(cs_end - cs)
            at = -kk * jnp.exp(cs - ld)
            rt = r * e_pos
            bt = bv * e_neg
            kt = km * e_neg
            lab = jnp.where(strict, mm(at, bt, "nt"), 0.0)
            lak = jnp.where(strict, mm(at, kt, "nt"), 0.0)
            lrb = jnp.where(lower, mm(rt, bt, "nt"), 0.0)
            lrk = jnp.where(lower, mm(rt, kt, "nt"), 0.0)
            tinv = _unit_lower_inverse(lab, eye, same_blk, passes)
            u = mm(tinv, mm(at, st, "nt") + mm(lak, v))
            out = mm(rt, st, "nt") + mm(lrk, v) + mm(lrb, u)
            st = jnp.exp(cs_end) * st + mm(u, bv * e_end, "tn") + mm(v, km * e_end, "tn")
            mean = jnp.mean(out, axis=-1, keepdims=True)
            var = jnp.mean(jnp.square(out - mean), axis=-1, keepdims=True)
            y = (out - mean) * lax.rsqrt(var + B_LN_EPS) * ln_w + ln_b
            y = y + jnp.sum(r * km * r_k, axis=-1, keepdims=True) * v
            o_ref[0, rs, sl] = (y * g).astype(o_ref.dtype)
        st_ref[hd] = st


def _rwkv_scan(rkv, ld, a, g, par, *, tblk=256):
    b, s, _ = ld.shape
    tblk = min(tblk, s)
    npair = B_WIDTH // LANES
    blk = (1, tblk, LANES)
    body = functools.partial(_rwkv_scan_body, passes=RWKV_PASSES)
    return pl.pallas_call(
        body,
        grid=(b, npair, s // tblk),
        in_specs=[pl.BlockSpec(blk, lambda bi, hp, t: (bi, t, hp)),
                  pl.BlockSpec(blk, lambda bi, hp, t: (bi, t, npair + hp)),
                  pl.BlockSpec(blk, lambda bi, hp, t: (bi, t, 2 * npair + hp)),
                  pl.BlockSpec(blk, lambda bi, hp, t: (bi, t, hp)),
                  pl.BlockSpec(blk, lambda bi, hp, t: (bi, t, hp)),
                  pl.BlockSpec(blk, lambda bi, hp, t: (bi, t, hp)),
                  pl.BlockSpec((8, LANES), lambda bi, hp, t: (0, hp))],
        out_specs=pl.BlockSpec(blk, lambda bi, hp, t: (bi, t, hp)),
        out_shape=jax.ShapeDtypeStruct((b, s, B_WIDTH), BF16),
        scratch_shapes=[pltpu.VMEM((LANES // B_HEAD, B_HEAD, B_HEAD), F32)],
        compiler_params=_cparams(("parallel", "parallel", "arbitrary")),
        name="rwkv_scan",
    )(rkv, rkv, rkv, ld, a, g, par)


S5_ROWS = CHUNK * C_GROUP
S5_PASSES = 3


def _s5_build_body(lam_c_ref, lam_r_ref, bt_ref, ctr_ref, cti_ref, m_ref, w_ref, v_ref, lc_ref):
    def zoh(lre, lim, dt):
        lre = jnp.minimum(lre, -1e-4)
        rho, th = lre * dt, lim * dt
        mag = jnp.exp(rho)
        lbr, lbi = mag * jnp.cos(th), mag * jnp.sin(th)
        den = 1.0 / (lre * lre + lim * lim)
        cr = ((lbr - 1.0) * lre + lbi * lim) * den
        cim = (lbi * lre - (lbr - 1.0) * lim) * den
        return rho, th, lbr, lbi, cr, cim

    lc = lam_c_ref[0]
    rho_c, th_c, lbr_c, lbi_c, _, _ = zoh(lc[:, 0:1], lc[:, 1:2], jnp.exp(lc[:, 2:3]))
    lr = lam_r_ref[0]
    rho_r, th_r, _, _, cr_r, ci_r = zoh(lr[0:1], lr[1:2], jnp.exp(lr[2:3]))

    tau = (lax.broadcasted_iota(I32, (C_STATE, S5_ROWS), 1) // C_GROUP).astype(F32)
    mag = jnp.exp(rho_c * tau)
    e_re, e_im = mag * jnp.cos(th_c * tau), mag * jnp.sin(th_c * tau)
    ctr, cti = ctr_ref[0], cti_ref[0]
    g_re = e_re * ctr - e_im * cti
    g_im = e_re * cti + e_im * ctr
    btr, bti = bt_ref[0, 0], bt_ref[0, 1]
    bbr = btr * cr_r - bti * ci_r
    bbi = btr * ci_r + bti * cr_r
    kt = _dotp(bbr, g_re, S5_PASSES) - _dotp(bbi, g_im, S5_PASSES)
    lane = lax.broadcasted_iota(I32, (C_GROUP, S5_ROWS), 1)
    for s in range(CHUNK):
        blk = kt if s == 0 else jnp.where(lane >= s * C_GROUP, pltpu.roll(kt, s * C_GROUP, 1), 0.0)
        m_ref[0, s * C_GROUP:(s + 1) * C_GROUP, :] = blk.astype(m_ref.dtype)

    back = (CHUNK - 1 - lax.broadcasted_iota(I32, (CHUNK, C_STATE), 0)).astype(F32)
    mag2 = jnp.exp(rho_r * back)
    f_re, f_im = mag2 * jnp.cos(th_r * back), mag2 * jnp.sin(th_r * back)
    for s in range(CHUNK):
        fr, fi = f_re[s:s + 1], f_im[s:s + 1]
        w_ref[0, s * C_GROUP:(s + 1) * C_GROUP, :] = jnp.concatenate(
            [bbr * fr - bbi * fi, bbr * fi + bbi * fr], axis=1).astype(w_ref.dtype)

    v_ref[0, 0:C_STATE, :] = (g_re * lbr_c - g_im * lbi_c).astype(v_ref.dtype)
    v_ref[0, C_STATE:2 * C_STATE, :] = (-(g_re * lbi_c + g_im * lbr_c)).astype(v_ref.dtype)

    magc = jnp.exp(rho_r * float(CHUNK))
    pr, pi = magc * jnp.cos(th_r * float(CHUNK)), magc * jnp.sin(th_r * float(CHUNK))
    lc_ref[0] = jnp.concatenate([jnp.concatenate([pr, pr], axis=1),
                                 jnp.concatenate([-pi, pi], axis=1),
                                 jnp.zeros((6, 2 * C_STATE), F32)], axis=0)


def _s5_build(lam_c, lam_r, bt, ctr, cti):
    g = lam_c.shape[0]
    i3 = lambda gi: (gi, 0, 0)
    return pl.pallas_call(
        _s5_build_body,
        grid=(g,),
        in_specs=[pl.BlockSpec((1, C_STATE, 8), i3),
                  pl.BlockSpec((1, 8, C_STATE), i3),
                  pl.BlockSpec((1, 2, C_GROUP, C_STATE), lambda gi: (gi, 0, 0, 0)),
                  pl.BlockSpec((1, C_STATE, S5_ROWS), i3),
                  pl.BlockSpec((1, C_STATE, S5_ROWS), i3)],
        out_specs=[pl.BlockSpec((1, S5_ROWS, S5_ROWS), i3),
                   pl.BlockSpec((1, S5_ROWS, 2 * C_STATE), i3),
                   pl.BlockSpec((1, 2 * C_STATE, S5_ROWS), i3),
                   pl.BlockSpec((1, 8, 2 * C_STATE), i3)],
        out_shape=[jax.ShapeDtypeStruct((g, S5_ROWS, S5_ROWS), BF16),
                   jax.ShapeDtypeStruct((g, S5_ROWS, 2 * C_STATE), BF16),
                   jax.ShapeDtypeStruct((g, 2 * C_STATE, S5_ROWS), BF16),
                   jax.ShapeDtypeStruct((g, 8, 2 * C_STATE), F32)],
        compiler_params=_cparams(("parallel",)),
        name="s5_build",
    )(lam_c, lam_r, bt, ctr, cti)


def _s5_apply_body(u_ref, m_ref, w_ref, v_ref, lc_ref, y_ref, z_ref, xs_ref, *, bsz):
    u = u_ref[0]
    z_ref[...] = _dot(u, w_ref[0])
    n_chunk = u.shape[0] // bsz
    lr, li = lc_ref[0, 0:1, :], lc_ref[0, 1:2, :]

    def step(ci, x):
        r0 = pl.multiple_of(ci * bsz, bsz)
        xs_ref[pl.ds(r0, bsz), :] = x
        return x * lr + pltpu.roll(x, C_STATE, 1) * li + z_ref[pl.ds(r0, bsz), :]

    lax.fori_loop(0, n_chunk, step, jnp.zeros((bsz, 2 * C_STATE), F32))
    y_ref[0] = _dot(u, m_ref[0]) + _dot(xs_ref[...].astype(BF16), v_ref[0])


def _s5_apply(u_g, m, w, v, lc, *, bsz):
    g, rows, _ = u_g.shape
    i3 = lambda gi: (gi, 0, 0)
    body = functools.partial(_s5_apply_body, bsz=bsz)
    return pl.pallas_call(
        body,
        grid=(g,),
        in_specs=[pl.BlockSpec((1, rows, S5_ROWS), i3),
                  pl.BlockSpec((1, S5_ROWS, S5_ROWS), i3),
                  pl.BlockSpec((1, S5_ROWS, 2 * C_STATE), i3),
                  pl.BlockSpec((1, 2 * C_STATE, S5_ROWS), i3),
                  pl.BlockSpec((1, 8, 2 * C_STATE), i3)],
        out_specs=pl.BlockSpec((1, rows, S5_ROWS), i3),
        out_shape=jax.ShapeDtypeStruct((g, rows, S5_ROWS), F32),
        scratch_shapes=[pltpu.VMEM((rows, 2 * C_STATE), F32), pltpu.VMEM((rows, 2 * C_STATE), F32)],
        compiler_params=_cparams(("parallel",)),
        name="s5_apply",
    )(u_g, m, w, v, lc)


def _pad_cols(w, width):
    return jnp.pad(w, ((0, 0), (0, width - w.shape[1])))


def _pad_rows(w, height):
    return jnp.pad(w, ((0, height - w.shape[0]), (0, 0)))


def _even_in_weights(w_in):
    o = 0
    pieces = {}
    for name, width in (("cq", A_Q_RANK), ("ckv", A_KV_RANK), ("krope", A_ROPE_DIM), ("kidx", IDX_DIM),
                        ("widx", IDX_HEADS), ("rkv", 3 * B_WIDTH), ("wl", B_DECAY_LORA),
                        ("al", B_A_LORA), ("gl", B_GATE_LORA)):
        pieces[name] = w_in[:, o:o + width]
        o += width
    w_a = jnp.concatenate([pieces["cq"], pieces["ckv"], _pad_cols(pieces["krope"], LANES),
                           _pad_cols(pieces["kidx"], LANES), _pad_cols(pieces["widx"], LANES)], axis=1)
    w_b = jnp.concatenate([pieces["rkv"], _pad_cols(pieces["wl"], LANES), _pad_cols(pieces["al"], LANES),
                           _pad_cols(pieces["gl"], 2 * LANES)], axis=1)
    return w_a.astype(BF16), w_b.astype(BF16)


def _pad_mu(mu):
    o = 3 * B_WIDTH
    wl = mu[o:o + B_DECAY_LORA]
    al = mu[o + B_DECAY_LORA:o + B_DECAY_LORA + B_A_LORA]
    gl = mu[o + B_DECAY_LORA + B_A_LORA:]
    z = lambda n: jnp.zeros((n,), F32)
    return jnp.concatenate([mu[:o], wl, z(LANES - B_DECAY_LORA), al, z(LANES - B_A_LORA),
                            gl, z(2 * LANES - B_GATE_LORA)])[None, :]


def _dsa_mixer(h_a3, start_frame, cq_norm, ckv_norm, kidx_norm, w_uq, w_uk, w_uv, w_qidx):
    wuq = w_uq.reshape(A_Q_RANK, A_HEADS * A_HEAD_DIM).astype(BF16)
    wuk_t = jnp.pad(jnp.transpose(w_uk, (1, 2, 0)), ((0, 0), (A_ROPE_DIM, 0), (0, 0))).astype(BF16)
    wqi = jnp.pad(w_qidx, ((0, 0), (0, 0), (0, LANES - IDX_DIM))).reshape(A_Q_RANK, IDX_HEADS * LANES).astype(BF16)
    wuv = jnp.transpose(w_uv, (1, 0, 2)).astype(BF16)
    qcat, kcat, qidx, kidx, widx = _dsa_proj(
        h_a3, start_frame, _rope_tables(), cq_norm[None, :], ckv_norm[None, :],
        _pad_cols(kidx_norm[None, :], LANES), wuq, wuk_t, wqi)
    return _dsa_attend(qidx, widx, kidx, qcat, kcat, wuv)


def _rwkv_mixer(h_b3, mu, w0, w2, a0, a2, g2, k_k, k_a, r_k, ln_w, ln_b):
    rkv, ld, a, g = _rwkv_prep(h_b3, _pad_mu(mu), w0[None, :], a0[None, :],
                               _pad_rows(w2, LANES).astype(BF16), _pad_rows(a2, LANES).astype(BF16),
                               _pad_rows(g2, 2 * LANES).astype(BF16))
    par = jnp.concatenate([k_k[None, :], k_a[None, :], r_k.reshape(1, B_WIDTH), ln_w[None, :], ln_b[None, :],
                           jnp.zeros((3, B_WIDTH), F32)], axis=0)
    return _rwkv_scan(rkv, ld, a, g, par)


def _even_mixer(x2, bsz, seq, start_frame, g_mix, w_in, w_out, cq_norm, ckv_norm, kidx_norm, w_uq, w_uk, w_uv,
                w_qidx, mu, w0, w2, a0, a2, g2, k_k, k_a, r_k, ln_w, ln_b):
    w_a, w_b = _even_in_weights(w_in)
    rms_pro = lambda x, g: _rms(x, g)
    h_a = _matmul([x2], [g_mix], w_a, [], prologue=rms_pro, epilogue=_epi_id, out_dtype=F32,
                  tm=512, tn=A_PAD_COLS, name="even_in_a")
    h_b = _matmul([x2], [g_mix], w_b, [], prologue=rms_pro, epilogue=_epi_id, out_dtype=F32,
                  tm=512, tn=B_PAD_COLS // 4, name="even_in_b")
    y_a = _dsa_mixer(h_a.reshape(bsz, seq, A_PAD_COLS), start_frame, cq_norm, ckv_norm, kidx_norm,
                     w_uq, w_uk, w_uv, w_qidx)
    y_b = _rwkv_mixer(h_b.reshape(bsz, seq, B_PAD_COLS), mu, w0, w2, a0, a2, g2, k_k, k_a, r_k, ln_w, ln_b)

    m = bsz * seq
    cat_pro = lambda ya, yb: jnp.concatenate([ya, yb], axis=-1)
    return _matmul([y_a.reshape(m, A_WIDTH), y_b.reshape(m, B_WIDTH)], [], w_out.astype(BF16), [(x2, "tile")],
                   prologue=cat_pro, epilogue=_epi_add, out_dtype=F32, tm=512, tn=512, name="even_out")


def _gelu_tanh(y):
    return 0.5 * y * (1.0 + jnp.tanh(math.sqrt(2.0 / math.pi) * (y + 0.044715 * (y * y * y))))


def _odd_mixer(x2, bsz, seq, g_mix, w_in, w_out, lam_re, lam_im, log_dt, b_re, b_im, c_re, c_im, d_skip,
               w_glu, b_glu):
    u = _matmul([x2], [g_mix], w_in.astype(BF16), [], prologue=lambda x, g: _rms(x, g), epilogue=_epi_id,
                out_dtype=F32, tm=512, tn=512, name="odd_in")
    gated = _s5_mixer(u, bsz, seq, lam_re, lam_im, log_dt, b_re, b_im, c_re, c_im, d_skip, w_glu, b_glu)
    return _matmul([gated], [], w_out.astype(BF16), [(x2, "tile")], prologue=_pro_id, epilogue=_epi_add,
                   out_dtype=F32, tm=512, tn=512, name="odd_out")


def _s5_mixer(u, bsz, seq, lam_re, lam_im, log_dt, b_re, b_im, c_re, c_im, d_skip, w_glu, b_glu):
    m = bsz * seq
    n_chunk = seq // CHUNK
    assert bsz % 8 == 0, "chunk rows are addressed as whole sublane tiles"

    ldt = jnp.broadcast_to(log_dt[:, None], lam_re.shape)
    lam_r = jnp.pad(jnp.stack([lam_re, lam_im, ldt], axis=1), ((0, 0), (0, 5), (0, 0)))
    lam_c = jnp.transpose(lam_r, (0, 2, 1))
    bt = jnp.stack([jnp.transpose(b_re, (0, 2, 1)), jnp.transpose(b_im, (0, 2, 1))], axis=1)
    ctr = jnp.tile(jnp.transpose(c_re, (0, 2, 1)), (1, 1, CHUNK))
    cti = jnp.tile(jnp.transpose(c_im, (0, 2, 1)), (1, 1, CHUNK))
    m_mat, w_mat, v_mat, lc = _s5_build(lam_c, lam_r, bt, ctr, cti)

    u_g = jnp.transpose(u.astype(BF16).reshape(bsz, n_chunk, CHUNK, C_GROUPS, C_GROUP), (3, 1, 0, 2, 4))
    y_g = _s5_apply(u_g.reshape(C_GROUPS, n_chunk * bsz, S5_ROWS), m_mat, w_mat, v_mat, lc, bsz=bsz)
    y = jnp.transpose(y_g.reshape(C_GROUPS, n_chunk, bsz, CHUNK, C_GROUP), (2, 1, 3, 0, 4)).reshape(m, C_WIDTH)

    z_of = lambda yy, uu, dd: _gelu_tanh(yy + dd * uu)
    return _matmul([y, u], [d_skip[None, :]], w_glu.astype(BF16),
                   [(y, "tile"), (u, "tile"), (d_skip[None, :], "row"), (b_glu[None, :], "row")],
                   prologue=z_of,
                   epilogue=lambda acc, yy, uu, dd, bb: z_of(yy, uu, dd) * jax.nn.sigmoid(acc + bb),
                   out_dtype=BF16, tm=512, tn=512, name="s5_glu")


def _final_norm_body(x_ref, g_ref, o_ref):
    o_ref[...] = _rms(x_ref[...], g_ref[...])


def _final_norm(x2, g, *, tm=512):
    m, d = x2.shape
    tm = min(tm, m)
    return pl.pallas_call(
        _final_norm_body,
        grid=(m // tm,),
        in_specs=[pl.BlockSpec((tm, d), lambda i: (i, 0)), pl.BlockSpec((1, d), lambda i: (0, 0))],
        out_specs=pl.BlockSpec((tm, d), lambda i: (i, 0)),
        out_shape=jax.ShapeDtypeStruct((m, d), F32),
        compiler_params=_cparams(("parallel",)),
        name="final_norm",
    )(x2, g)


def kernel(x, mem, start_frame, norm_mix, norm_xattn, norm_mem, norm_ffn, final_norm, xattn_wq, xattn_wkv, xattn_wo, ffn_up, ffn_down, even_w_in, even_w_out, dsa_cq_norm, dsa_ckv_norm, dsa_kidx_norm, dsa_w_uq, dsa_w_uk, dsa_w_uv, dsa_w_qidx, rwkv_mu, rwkv_w0, rwkv_w2, rwkv_a0, rwkv_a2, rwkv_g2, rwkv_k_k, rwkv_k_a, rwkv_r_k, rwkv_ln_w, rwkv_ln_b, odd_w_in, odd_w_out, s5_lam_re, s5_lam_im, s5_log_dt, s5_b_re, s5_b_im, s5_c_re, s5_c_im, s5_d, s5_w_glu, s5_b_glu):
    bsz, seq, d = x.shape
    depth = norm_mix.shape[0]
    m = bsz * seq
    x2 = x.reshape(m, d)
    mem2 = mem.reshape(bsz * mem.shape[1], d)
    for layer in range(depth):
        i = layer // 2
        g_mix = norm_mix[layer][None, :]
        if layer % 2 == 0:
            x2 = _even_mixer(x2, bsz, seq, start_frame, g_mix, even_w_in[i], even_w_out[i], dsa_cq_norm[i],
                             dsa_ckv_norm[i], dsa_kidx_norm[i], dsa_w_uq[i], dsa_w_uk[i], dsa_w_uv[i],
                             dsa_w_qidx[i], rwkv_mu[i], rwkv_w0[i], rwkv_w2[i], rwkv_a0[i], rwkv_a2[i],
                             rwkv_g2[i], rwkv_k_k[i], rwkv_k_a[i], rwkv_r_k[i], rwkv_ln_w[i], rwkv_ln_b[i])
        else:
            x2 = _odd_mixer(x2, bsz, seq, g_mix, odd_w_in[i], odd_w_out[i], s5_lam_re[i], s5_lam_im[i],
                            s5_log_dt[i], s5_b_re[i], s5_b_im[i], s5_c_re[i], s5_c_im[i], s5_d[i],
                            s5_w_glu[i], s5_b_glu[i])
        kv = _matmul([mem2], [norm_mem[layer][None, :]], xattn_wkv[layer].astype(BF16), [],
                     prologue=lambda a, g: _rms(a, g), epilogue=_epi_id, out_dtype=BF16, tm=512, tn=512,
                     name="xattn_kv")
        x2 = _xattn(x2.reshape(bsz, seq, d), norm_xattn[layer][None, :], xattn_wq[layer].astype(BF16),
                    kv.reshape(bsz, mem.shape[1], -1), xattn_wo[layer].astype(BF16)).reshape(m, d)
        x2 = _ffn(x2, norm_ffn[layer][None, :], ffn_up[layer].astype(BF16), ffn_down[layer].astype(BF16))
    return _final_norm(x2, final_norm[None, :]).reshape(bsz, seq, d)
```

```python
import functools
import math

import numpy as np
import jax
import jax.numpy as jnp
from jax import lax
from jax.experimental import pallas as pl
from jax.experimental.pallas import tpu as pltpu

F32 = jnp.float32
BF16 = jnp.bfloat16
I32 = jnp.int32

D_MODEL = 2048
CHUNK = 64
ROPE_THETA = 500000.0
NORM_EPS = 1e-5

A_HEADS = 8
A_HEAD_DIM = 128
A_ROPE_DIM = 32
A_V_DIM = 128
A_WIDTH = A_HEADS * A_V_DIM
A_Q_RANK = 512
A_KV_RANK = 256
IDX_HEADS = 16
IDX_DIM = 64
IDX_ROPE_DIM = 16
TOPK_MAX = 256

B_HEAD = 64
B_WIDTH = 1024
B_HEADS = 16
B_DECAY_LORA = 64
B_A_LORA = 64
B_GATE_LORA = 160
B_LN_EPS = 64e-5

C_WIDTH = 2048
C_GROUP = 16
C_GROUPS = 128
C_STATE = 64

X_HEADS = 4
X_HEAD_DIM = 128
FFN_DIM = 4 * D_MODEL

LANES = 128
VMEM_LIMIT = 56 * 1024 * 1024

A_PAD_COLS = A_Q_RANK + A_KV_RANK + 3 * LANES
B_LORA_COLS = 2 * LANES + 2 * LANES
B_PAD_COLS = 3 * B_WIDTH + B_LORA_COLS
KCAT = A_KV_RANK + LANES

INT_MIN = -2147483648
NEG_BIG = -1e30
M_FLOOR = -1e20
COUNT_ROWS = 32


def _cparams(sem):
    return pltpu.CompilerParams(dimension_semantics=sem, vmem_limit_bytes=VMEM_LIMIT)


def _rms(x, g):
    return x * lax.rsqrt(jnp.mean(x * x, axis=-1, keepdims=True) + NORM_EPS) * g


def _dot(a, b):
    return jnp.dot(a, b, preferred_element_type=F32)


def _dot_nt(a, b):
    return lax.dot_general(a, b, (((1,), (1,)), ((), ())), preferred_element_type=F32)


def _dot_tn(a, b):
    return lax.dot_general(a, b, (((0,), (0,)), ((), ())), preferred_element_type=F32)


def _split2(x):
    hi = x.astype(BF16)
    lo = (x - hi.astype(F32)).astype(BF16)
    return hi, lo


def _dotp(a, b, passes, kind="nn"):
    f = {"nn": _dot, "nt": _dot_nt, "tn": _dot_tn}[kind]
    if passes == 1:
        return f(a.astype(BF16), b.astype(BF16))
    ah, al = _split2(a)
    bh, bl = _split2(b)
    return f(ah, bh) + (f(ah, bl) + f(al, bh))


def _mm_body(*refs, n_row, n_const, n_epi, prologue, epilogue):
    rows = refs[:n_row]
    consts = refs[n_row:n_row + n_const]
    w_ref = refs[n_row + n_const]
    epis = refs[n_row + n_const + 1:n_row + n_const + 1 + n_epi]
    o_ref, xs_ref = refs[-2], refs[-1]

    @pl.when(pl.program_id(1) == 0)
    def _():
        xs_ref[...] = prologue(*[r[...] for r in rows], *[c[...] for c in consts]).astype(BF16)

    acc = _dot(xs_ref[...], w_ref[...])
    o_ref[...] = epilogue(acc, *[e[...] for e in epis]).astype(o_ref.dtype)


def _matmul(rows, consts, w, epis, *, prologue, epilogue, out_dtype, tm, tn, name):
    m = rows[0].shape[0]
    k, n = w.shape
    tm = min(tm, m)
    tn = min(tn, n)
    assert m % tm == 0 and n % tn == 0, (m, tm, n, tn)
    in_specs = [pl.BlockSpec((tm, r.shape[1]), lambda i, j: (i, 0)) for r in rows]
    in_specs += [pl.BlockSpec(c.shape, lambda i, j: (0, 0)) for c in consts]
    in_specs += [pl.BlockSpec((k, tn), lambda i, j: (0, j))]
    for _, kind in epis:
        if kind == "tile":
            in_specs.append(pl.BlockSpec((tm, tn), lambda i, j: (i, j)))
        else:
            in_specs.append(pl.BlockSpec((1, tn), lambda i, j: (0, j)))
    body = functools.partial(_mm_body, n_row=len(rows), n_const=len(consts), n_epi=len(epis),
                             prologue=prologue, epilogue=epilogue)
    return pl.pallas_call(
        body,
        grid=(m // tm, n // tn),
        in_specs=in_specs,
        out_specs=pl.BlockSpec((tm, tn), lambda i, j: (i, j)),
        out_shape=jax.ShapeDtypeStruct((m, n), out_dtype),
        scratch_shapes=[pltpu.VMEM((tm, k), BF16)],
        compiler_params=_cparams(("parallel", "arbitrary")),
        name=name,
    )(*rows, *consts, w, *[e for e, _ in epis])


def _epi_id(acc):
    return acc


def _epi_add(acc, res):
    return acc + res


def _pro_id(x):
    return x


def _ffn_body(x_ref, g_ref, up_ref, dn_ref, o_ref, xs_ref):
    @pl.when(pl.program_id(1) == 0)
    def _():
        x = x_ref[...]
        xs_ref[...] = _rms(x, g_ref[...]).astype(BF16)
        o_ref[...] = x

    h = _dot(xs_ref[...], up_ref[...])
    h = jnp.square(jnp.maximum(h, 0.0)).astype(BF16)
    o_ref[...] += _dot(h, dn_ref[...])


def _ffn(x, g, w_up, w_dn, *, tm=512, tf=512):
    m, d = x.shape
    f = w_up.shape[1]
    tm = min(tm, m)
    return pl.pallas_call(
        _ffn_body,
        grid=(m // tm, f // tf),
        in_specs=[pl.BlockSpec((tm, d), lambda i, j: (i, 0)),
                  pl.BlockSpec((1, d), lambda i, j: (0, 0)),
                  pl.BlockSpec((d, tf), lambda i, j: (0, j)),
                  pl.BlockSpec((tf, d), lambda i, j: (j, 0))],
        out_specs=pl.BlockSpec((tm, d), lambda i, j: (i, 0)),
        out_shape=jax.ShapeDtypeStruct((m, d), F32),
        scratch_shapes=[pltpu.VMEM((tm, d), BF16)],
        compiler_params=_cparams(("parallel", "arbitrary")),
        name="ffn",
    )(x, g, w_up, w_dn)


def _xattn_body(x_ref, g_ref, wq_ref, kv_ref, wo_ref, o_ref):
    x = x_ref[0]
    xn = _rms(x, g_ref[...]).astype(BF16)
    q = _dot(xn, wq_ref[...])
    width = X_HEADS * X_HEAD_DIM
    outs = []
    for h in range(X_HEADS):
        sl = slice(h * X_HEAD_DIM, (h + 1) * X_HEAD_DIM)
        qh = q[:, sl].astype(BF16)
        kh = kv_ref[0, :, sl]
        vh = kv_ref[0, :, width + h * X_HEAD_DIM:width + (h + 1) * X_HEAD_DIM]
        s = _dot_nt(qh, kh) * (X_HEAD_DIM ** -0.5)
        s = s - jnp.max(s, axis=-1, keepdims=True)
        p = jnp.exp(s)
        p = p / jnp.sum(p, axis=-1, keepdims=True)
        outs.append(_dot(p.astype(BF16), vh))
    o = jnp.concatenate(outs, axis=-1).astype(BF16)
    o_ref[0] = x + _dot(o, wo_ref[...])


def _xattn(x3, g, wq, kv, wo, *, tm=512):
    b, s, d = x3.shape
    tm = min(tm, s)
    mlen = kv.shape[1]
    width = X_HEADS * X_HEAD_DIM
    return pl.pallas_call(
        _xattn_body,
        grid=(b, s // tm),
        in_specs=[pl.BlockSpec((1, tm, d), lambda bi, i: (bi, i, 0)),
                  pl.BlockSpec((1, d), lambda bi, i: (0, 0)),
                  pl.BlockSpec((d, width), lambda bi, i: (0, 0)),
                  pl.BlockSpec((1, mlen, 2 * width), lambda bi, i: (bi, 0, 0)),
                  pl.BlockSpec((width, d), lambda bi, i: (0, 0))],
        out_specs=pl.BlockSpec((1, tm, d), lambda bi, i: (bi, i, 0)),
        out_shape=jax.ShapeDtypeStruct((b, s, d), F32),
        compiler_params=_cparams(("parallel", "parallel")),
        name="xattn",
    )(x3, g, wq, kv, wo)


def _rope_tables():
    inv_a = ROPE_THETA ** (-jnp.arange(0, A_ROPE_DIM, 2, dtype=F32) / A_ROPE_DIM)
    inv_i = ROPE_THETA ** (-jnp.arange(0, IDX_ROPE_DIM, 2, dtype=F32) / IDX_ROPE_DIM)
    ha, hi = A_ROPE_DIM // 2, IDX_ROPE_DIM // 2

    def row(*pieces):
        r = jnp.concatenate([jnp.asarray(p, F32) for p in pieces])
        return jnp.pad(r, (0, LANES - r.shape[0]))

    ones, zeros = (lambda n: jnp.ones((n,), F32)), (lambda n: jnp.zeros((n,), F32))
    return jnp.stack([row(inv_a, inv_a), row(ones(2 * ha)), row(-ones(ha)), row(zeros(ha), ones(ha)),
                      row(inv_i, inv_i), row(-ones(hi)), row(zeros(hi), ones(hi)), zeros(LANES)])


def _dsa_proj_body(sf_ref, h_ref, tab_ref, nq_ref, nkv_ref, nki_ref, wuq_ref, wuk_ref, wqi_ref,
                   qcat_ref, kcat_ref, qidx_ref, kidx_ref, widx_ref):
    bi, i = pl.program_id(0), pl.program_id(1)
    tm = h_ref.shape[1]
    ha, hi = A_ROPE_DIM // 2, IDX_ROPE_DIM // 2
    pos = (sf_ref[bi] + i * tm + lax.broadcasted_iota(I32, (tm, LANES), 0)).astype(F32)
    tab = tab_ref[...]
    ang_a = pos * tab[0:1]
    cos_a = jnp.cos(ang_a) * tab[1:2]
    sin_a = jnp.sin(ang_a)
    sin_a1, sin_a2 = sin_a * tab[2:3], sin_a * tab[3:4]
    ang_i = pos * tab[4:5]
    cos_i = jnp.cos(ang_i)
    sin_i = jnp.sin(ang_i)
    sin_i1, sin_i2 = sin_i * tab[5:6], sin_i * tab[6:7]

    def rope_a(xb):
        return xb * cos_a + pltpu.roll(xb, LANES - ha, 1) * sin_a1 + pltpu.roll(xb, ha, 1) * sin_a2

    def rope_i(xb):
        return xb * cos_i + pltpu.roll(xb, LANES - hi, 1) * sin_i1 + pltpu.roll(xb, hi, 1) * sin_i2

    c_q = h_ref[0, :, 0:A_Q_RANK]
    c_kv = h_ref[0, :, A_Q_RANK:A_Q_RANK + A_KV_RANK]
    off = A_Q_RANK + A_KV_RANK
    k_rope = h_ref[0, :, off:off + LANES]
    k_idx = h_ref[0, :, off + LANES:off + 2 * LANES]
    w_idx = h_ref[0, :, off + 2 * LANES:off + 3 * LANES]

    cqn = _rms(c_q, nq_ref[...]).astype(BF16)
    kcat_ref[0, :, 0:A_KV_RANK] = _rms(c_kv, nkv_ref[...]).astype(BF16)
    kcat_ref[0, :, A_KV_RANK:KCAT] = rope_a(k_rope).astype(BF16)

    ki = k_idx * lax.rsqrt(jnp.sum(k_idx * k_idx, axis=-1, keepdims=True) * (1.0 / IDX_DIM) + NORM_EPS)
    kidx_ref[0] = rope_i(ki * nki_ref[...]).astype(BF16)
    widx_ref[0] = w_idx * ((IDX_HEADS ** -0.5) * (IDX_DIM ** -0.5))

    q = _dot(cqn, wuq_ref[...])
    for h in range(A_HEADS):
        qh = q[:, h * A_HEAD_DIM:(h + 1) * A_HEAD_DIM]
        qcat_ref[0, h, :, 0:A_KV_RANK] = _dot(qh.astype(BF16), wuk_ref[h]).astype(BF16)
        qcat_ref[0, h, :, A_KV_RANK:KCAT] = rope_a(qh).astype(BF16)
    qi = _dot(cqn, wqi_ref[...])
    for h in range(IDX_HEADS):
        qidx_ref[0, h] = rope_i(qi[:, h * LANES:(h + 1) * LANES]).astype(BF16)


def _dsa_proj(h_a, start_frame, tab, nq, nkv, nki, wuq, wuk_t, wqi, *, tm=256):
    b, s, _ = h_a.shape
    tm = min(tm, s)
    c3 = lambda bi, i, sf: (bi, i, 0)
    c4 = lambda bi, i, sf: (bi, 0, i, 0)
    z2 = lambda bi, i, sf: (0, 0)
    z3 = lambda bi, i, sf: (0, 0, 0)
    grid_spec = pltpu.PrefetchScalarGridSpec(
        num_scalar_prefetch=1,
        grid=(b, s // tm),
        in_specs=[pl.BlockSpec((1, tm, A_PAD_COLS), c3),
                  pl.BlockSpec(tab.shape, z2),
                  pl.BlockSpec(nq.shape, z2),
                  pl.BlockSpec(nkv.shape, z2),
                  pl.BlockSpec(nki.shape, z2),
                  pl.BlockSpec(wuq.shape, z2),
                  pl.BlockSpec(wuk_t.shape, z3),
                  pl.BlockSpec(wqi.shape, z2)],
        out_specs=[pl.BlockSpec((1, A_HEADS, tm, KCAT), c4),
                   pl.BlockSpec((1, tm, KCAT), c3),
                   pl.BlockSpec((1, IDX_HEADS, tm, LANES), c4),
                   pl.BlockSpec((1, tm, LANES), c3),
                   pl.BlockSpec((1, tm, LANES), c3)],
    )
    return pl.pallas_call(
        _dsa_proj_body,
        grid_spec=grid_spec,
        out_shape=[jax.ShapeDtypeStruct((b, A_HEADS, s, KCAT), BF16),
                   jax.ShapeDtypeStruct((b, s, KCAT), BF16),
                   jax.ShapeDtypeStruct((b, IDX_HEADS, s, LANES), BF16),
                   jax.ShapeDtypeStruct((b, s, LANES), BF16),
                   jax.ShapeDtypeStruct((b, s, LANES), F32)],
        compiler_params=_cparams(("parallel", "parallel")),
        name="dsa_proj",
    )(start_frame, h_a, tab, nq, nkv, nki, wuq, wuk_t, wqi)


def _dsa_attend_body(qidx_ref, widx_ref, kidx_ref, qcat_ref, kcat_ref, wuv_ref, o_ref,
                     key_ref, m_ref, l_ref, a_ref, acc_ref, t_ref, p_ref, *, tq, kt, topk):
    i = pl.program_id(1)
    t0 = i * tq
    nk = (t0 + tq + kt - 1) // kt
    krow = lax.broadcasted_iota(I32, (kt, tq), 0)
    qcol = lax.broadcasted_iota(I32, (kt, tq), 1)
    limit = ((t0 + qcol) // CHUNK + 1) * CHUNK

    wt = widx_ref[0]

    def score_tile(j, carry):
        off = pl.multiple_of(j * kt, kt)
        kid = kidx_ref[0, pl.ds(off, kt), :]
        sc = jnp.zeros((kt, tq), F32)
        for h in range(IDX_HEADS):
            sc = sc + wt[h:h + 1, :] * jnp.maximum(_dot_nt(kid, qidx_ref[0, h]), 0.0)
        bits = pltpu.bitcast(sc, I32)
        key = jnp.where(bits >= 0, bits, bits ^ 0x7FFFFFFF)
        key_ref[j] = jnp.where(off + krow < limit, key, INT_MIN)
        return carry

    lax.fori_loop(0, nk, score_tile, 0)

    def count_ge(cand):
        def body(j, acc):
            hit = jnp.where(key_ref[j] >= cand, 1.0, 0.0)
            return acc + jnp.sum(hit.reshape(kt // COUNT_ROWS, COUNT_ROWS, tq), axis=0)
        acc = lax.fori_loop(0, nk, body, jnp.zeros((COUNT_ROWS, tq), F32))
        return jnp.sum(acc, axis=0, keepdims=True)

    kf = float(topk)
    thr = jnp.where(count_ge(jnp.zeros((1, tq), I32)) >= kf, 0, INT_MIN).astype(I32)

    def bit_step(it, thr):
        cand = thr | jnp.left_shift(jnp.int32(1), 30 - it)
        return jnp.where(count_ge(cand) >= kf, cand, thr)

    thr = lax.fori_loop(0, 31, bit_step, thr)
    thr = jnp.maximum(thr, INT_MIN + 1)

    m_ref[...] = jnp.full(m_ref.shape, M_FLOOR, F32)
    l_ref[...] = jnp.zeros(l_ref.shape, F32)
    acc_ref[...] = jnp.zeros(acc_ref.shape, F32)
    c_exp = (A_HEAD_DIM ** -0.5) * math.log2(math.e)
    eye_q = jnp.where(lax.broadcasted_iota(I32, (tq, tq), 0) == lax.broadcasted_iota(I32, (tq, tq), 1),
                      1.0, 0.0).astype(BF16)

    def attend_tile(j, carry):
        off = pl.multiple_of(j * kt, kt)
        kc = kcat_ref[0, pl.ds(off, kt), :]
        ckv = kc[:, 0:A_KV_RANK]
        bias_t = jnp.where(key_ref[j] >= thr, 0.0, NEG_BIG).astype(BF16)
        bias = _dot_nt(eye_q, bias_t)
        for h in range(A_HEADS):
            t_ref[h] = _dot_nt(qcat_ref[0, h], kc) * c_exp + bias
        for h in range(A_HEADS):
            t = t_ref[h]
            m_old = m_ref[h]
            m_new = jnp.maximum(m_old, jnp.max(t, axis=-1, keepdims=True))
            p = jnp.exp2(t - m_new)
            alpha = jnp.exp2(m_old - m_new)
            l_ref[h] = alpha * l_ref[h] + jnp.sum(p, axis=-1, keepdims=True)
            p_ref[h] = p.astype(BF16)
            a_ref[h] = alpha
            m_ref[h] = m_new
        for h in range(A_HEADS):
            acc_ref[h] = a_ref[h] * acc_ref[h] + _dot(p_ref[h], ckv)
        return carry

    lax.fori_loop(0, nk, attend_tile, 0)

    o_lat = acc_ref[...] / l_ref[...]
    for h in range(A_HEADS):
        o_ref[0, :, h * A_V_DIM:(h + 1) * A_V_DIM] = _dot(o_lat[h].astype(BF16), wuv_ref[h]).astype(o_ref.dtype)


def _dsa_attend(qidx, widx, kidx, qcat, kcat, wuv, *, tq=256, kt=512):
    b, _, s, _ = qidx.shape
    kt = min(kt, s)
    tq = min(tq, s)
    topk = min(TOPK_MAX, s // 4)
    assert s % kt == 0 and s % tq == 0 and tq % CHUNK == 0
    body = functools.partial(_dsa_attend_body, tq=tq, kt=kt, topk=topk)
    widx_t = jnp.transpose(widx[:, :, 0:IDX_HEADS], (0, 2, 1))
    return pl.pallas_call(
        body,
        grid=(b, s // tq),
        in_specs=[pl.BlockSpec((1, IDX_HEADS, tq, LANES), lambda bi, i: (bi, 0, i, 0)),
                  pl.BlockSpec((1, IDX_HEADS, tq), lambda bi, i: (bi, 0, i)),
                  pl.BlockSpec((1, s, LANES), lambda bi, i: (bi, 0, 0)),
                  pl.BlockSpec((1, A_HEADS, tq, KCAT), lambda bi, i: (bi, 0, i, 0)),
                  pl.BlockSpec((1, s, KCAT), lambda bi, i: (bi, 0, 0)),
                  pl.BlockSpec(wuv.shape, lambda bi, i: (0, 0, 0))],
        out_specs=pl.BlockSpec((1, tq, A_WIDTH), lambda bi, i: (bi, i, 0)),
        out_shape=jax.ShapeDtypeStruct((b, s, A_WIDTH), BF16),
        scratch_shapes=[pltpu.VMEM((s // kt, kt, tq), I32),
                        pltpu.VMEM((A_HEADS, tq, 1), F32),
                        pltpu.VMEM((A_HEADS, tq, 1), F32),
                        pltpu.VMEM((A_HEADS, tq, 1), F32),
                        pltpu.VMEM((A_HEADS, tq, A_KV_RANK), F32),
                        pltpu.VMEM((A_HEADS, tq, kt), F32),
                        pltpu.VMEM((A_HEADS, tq, kt), BF16)],
        compiler_params=_cparams(("parallel", "arbitrary")),
        name="dsa_attend",
    )(qidx, widx_t, kidx, qcat, kcat, wuv)


def _rwkv_prep_body(h_ref, prev_ref, mu_ref, w0_ref, a0_ref, w2_ref, a2_ref, g2_ref,
                    rkv_ref, ld_ref, a_ref, g_ref):
    i = pl.program_id(1)
    h = h_ref[0]
    tm = h.shape[0]
    prev = jnp.where(i > 0, prev_ref[0, 7:8, :], 0.0)
    row = lax.broadcasted_iota(I32, h.shape, 0)
    shifted = jnp.where(row == 0, prev, pltpu.roll(h, 1, 0))
    hm = h + (shifted - h) * mu_ref[...]
    w3 = 3 * B_WIDTH
    rkv_ref[0] = hm[:, 0:w3]
    wl = hm[:, w3:w3 + LANES]
    al = hm[:, w3 + LANES:w3 + 2 * LANES]
    gl = hm[:, w3 + 2 * LANES:w3 + 4 * LANES]
    wz = w0_ref[...] + _dot(jnp.tanh(wl).astype(BF16), w2_ref[...])
    log_w = -jax.nn.softplus(-wz) - 0.5
    ld_ref[0] = -jnp.exp(log_w)
    a_ref[0] = jax.nn.sigmoid(a0_ref[...] + _dot(al.astype(BF16), a2_ref[...]))
    g_ref[0] = _dot(jax.nn.sigmoid(gl).astype(BF16), g2_ref[...])


def _rwkv_prep(h_b, mu, w0, a0, w2, a2, g2, *, tm=256):
    b, s, cols = h_b.shape
    tm = min(tm, s)
    c3 = lambda bi, i: (bi, i, 0)
    z2 = lambda bi, i: (0, 0)
    prev_map = lambda bi, i: (bi, jnp.maximum(i * (tm // 8) - 1, 0), 0)
    out = lambda w: jax.ShapeDtypeStruct((b, s, w), F32)
    return pl.pallas_call(
        _rwkv_prep_body,
        grid=(b, s // tm),
        in_specs=[pl.BlockSpec((1, tm, cols), c3),
                  pl.BlockSpec((1, 8, cols), prev_map),
                  pl.BlockSpec(mu.shape, z2), pl.BlockSpec(w0.shape, z2), pl.BlockSpec(a0.shape, z2),
                  pl.BlockSpec(w2.shape, z2), pl.BlockSpec(a2.shape, z2), pl.BlockSpec(g2.shape, z2)],
        out_specs=[pl.BlockSpec((1, tm, 3 * B_WIDTH), c3), pl.BlockSpec((1, tm, B_WIDTH), c3),
                   pl.BlockSpec((1, tm, B_WIDTH), c3), pl.BlockSpec((1, tm, B_WIDTH), c3)],
        out_shape=[out(3 * B_WIDTH), out(B_WIDTH), out(B_WIDTH), out(B_WIDTH)],
        compiler_params=_cparams(("parallel", "parallel")),
        name="rwkv_prep",
    )(h_b, h_b, mu, w0, a0, w2, a2, g2)


RWKV_CHUNK = 64
RWKV_SUB = 16


def _bmm(a, b, kind="nn"):
    spec = {"nn": "bij,bjk->bik", "nt": "bik,bjk->bij"}[kind]
    return jnp.einsum(spec, a.astype(BF16), b.astype(BF16), preferred_element_type=F32)


def _unit_lower_inverse(l_mat, eye, same_blk):
    mm = _bmm
    ld = jnp.where(same_blk, l_mat, 0.0)
    lo = l_mat - ld
    x = eye + ld
    p = mm(ld, ld)
    x = x + mm(x, p)
    p = mm(p, p)
    x = x + mm(x, p)
    p = mm(p, p)
    d = x + mm(x, p)
    n = mm(d, lo)
    y = eye + n
    y = y + mm(y, mm(n, n))
    return mm(y, d)


def _rwkv_scan_body(r_ref, k_ref, v_ref, ld_ref, a_ref, g_ref, par_ref, o_ref, st_ref):
    c = RWKV_CHUNK
    tblk = r_ref.shape[1]
    nc = tblk // c
    nh = LANES // B_HEAD

    @pl.when(pl.program_id(2) == 0)
    def _():
        st_ref[...] = jnp.zeros(st_ref.shape, F32)

    def heads(x):
        return jnp.concatenate([x[:, hd * B_HEAD:(hd + 1) * B_HEAD].reshape(nc, c, B_HEAD) for hd in range(nh)],
                               axis=0)

    def head_row(i):
        p = par_ref[i:i + 1, :]
        return jnp.concatenate([jnp.broadcast_to(p[:, hd * B_HEAD:(hd + 1) * B_HEAD][None], (nc, 1, B_HEAD))
                                for hd in range(nh)], axis=0)

    ri = lax.broadcasted_iota(I32, (c, c), 0)
    ci = lax.broadcasted_iota(I32, (c, c), 1)
    eye = jnp.where(ri == ci, 1.0, 0.0).astype(F32)[None]
    lower = (ri >= ci)[None]
    strict = (ri > ci)[None]
    same_blk = ((ri // RWKV_SUB) == (ci // RWKV_SUB))[None]

    rb = lax.broadcasted_iota(I32, (tblk, tblk), 0)
    cb = lax.broadcasted_iota(I32, (tblk, tblk), 1)
    tri = jnp.where((rb >= cb) & (rb // c == cb // c), 1.0, 0.0).astype(BF16)
    ld2 = ld_ref[0]
    l1 = ld2.astype(BF16)
    rem = ld2 - l1.astype(F32)
    l2 = rem.astype(BF16)
    l3 = (rem - l2.astype(F32)).astype(BF16)
    cs = heads(_dot(tri, l1) + (_dot(tri, l2) + _dot(tri, l3)))

    r, k, v = heads(r_ref[0]), heads(k_ref[0]), heads(v_ref[0])
    ld, a, g = heads(ld2), heads(a_ref[0]), heads(g_ref[0])
    kk = k * head_row(0)
    kk = kk * lax.rsqrt(jnp.maximum(jnp.sum(kk * kk, axis=-1, keepdims=True), 1e-24))
    km = k * (1.0 + (a - 1.0) * head_row(1))
    bv = kk * a
    cs_end = cs[:, c - 1:c, :]
    e_neg = jnp.exp(-cs)
    e_end = jnp.exp(cs_end - cs)
    at = -kk * jnp.exp(cs - ld)
    rt = r * jnp.exp(cs)
    bt, kt = bv * e_neg, km * e_neg
    bd, kd = bv * e_end, km * e_end
    p_end = jnp.exp(cs_end)

    lab = jnp.where(strict, _bmm(at, bt, "nt"), 0.0)
    lak = jnp.where(strict, _bmm(at, kt, "nt"), 0.0)
    lrb = jnp.where(lower, _bmm(rt, bt, "nt"), 0.0)
    lrk = jnp.where(lower, _bmm(rt, kt, "nt"), 0.0)
    tinv = _unit_lower_inverse(lab, eye, same_blk)
    tat = _bmm(tinv, at)
    c1 = _bmm(tinv, _bmm(lak, v))
    r2 = rt + _bmm(lrb, tat)
    c2 = _bmm(lrk, v) + _bmm(lrb, c1)
    gmat = _bmm(jnp.swapaxes(tat, 1, 2), bd)
    c3 = _bmm(jnp.swapaxes(c1, 1, 2), bd) + _bmm(jnp.swapaxes(v, 1, 2), kd)

    outs = [None] * (nh * nc)
    for hd in range(nh):
        st = st_ref[hd]
        for ch in range(nc):
            i = hd * nc + ch
            sb = st.astype(BF16)
            outs[i] = _dot_nt(r2[i].astype(BF16), sb) + c2[i]
            st = st * p_end[i] + (_dot(sb, gmat[i].astype(BF16)) + c3[i])
        st_ref[hd] = st
    out = jnp.stack(outs, axis=0)

    mean = jnp.mean(out, axis=-1, keepdims=True)
    var = jnp.mean(jnp.square(out - mean), axis=-1, keepdims=True)
    y = (out - mean) * lax.rsqrt(var + B_LN_EPS) * head_row(3) + head_row(4)
    y = (y + jnp.sum(r * km * head_row(2), axis=-1, keepdims=True) * v) * g
    for hd in range(nh):
        o_ref[0, :, hd * B_HEAD:(hd + 1) * B_HEAD] = y[hd * nc:(hd + 1) * nc].reshape(tblk, B_HEAD).astype(o_ref.dtype)


def _rwkv_scan(rkv, ld, a, g, par, *, tblk=512):
    b, s, _ = ld.shape
    tblk = min(tblk, s)
    npair = B_WIDTH // LANES
    blk = (1, tblk, LANES)
    body = _rwkv_scan_body
    return pl.pallas_call(
        body,
        grid=(b, npair, s // tblk),
        in_specs=[pl.BlockSpec(blk, lambda bi, hp, t: (bi, t, hp)),
                  pl.BlockSpec(blk, lambda bi, hp, t: (bi, t, npair + hp)),
                  pl.BlockSpec(blk, lambda bi, hp, t: (bi, t, 2 * npair + hp)),
                  pl.BlockSpec(blk, lambda bi, hp, t: (bi, t, hp)),
                  pl.BlockSpec(blk, lambda bi, hp, t: (bi, t, hp)),
                  pl.BlockSpec(blk, lambda bi, hp, t: (bi, t, hp)),
                  pl.BlockSpec((8, LANES), lambda bi, hp, t: (0, hp))],
        out_specs=pl.BlockSpec(blk, lambda bi, hp, t: (bi, t, hp)),
        out_shape=jax.ShapeDtypeStruct((b, s, B_WIDTH), BF16),
        scratch_shapes=[pltpu.VMEM((LANES // B_HEAD, B_HEAD, B_HEAD), F32)],
        compiler_params=_cparams(("parallel", "parallel", "arbitrary")),
        name="rwkv_scan",
    )(rkv, rkv, rkv, ld, a, g, par)


S5_ROWS = CHUNK * C_GROUP
S5_PASSES = 3


def _s5_build_body(lam_c_ref, lam_r_ref, bt_ref, ctr_ref, cti_ref, m_ref, w_ref, v_ref, lc_ref):
    def zoh(lre, lim, dt):
        lre = jnp.minimum(lre, -1e-4)
        rho, th = lre * dt, lim * dt
        mag = jnp.exp(rho)
        lbr, lbi = mag * jnp.cos(th), mag * jnp.sin(th)
        den = 1.0 / (lre * lre + lim * lim)
        cr = ((lbr - 1.0) * lre + lbi * lim) * den
        cim = (lbi * lre - (lbr - 1.0) * lim) * den
        return rho, th, lbr, lbi, cr, cim

    lc = lam_c_ref[0]
    rho_c, th_c, lbr_c, lbi_c, _, _ = zoh(lc[:, 0:1], lc[:, 1:2], jnp.exp(lc[:, 2:3]))
    lr = lam_r_ref[0]
    rho_r, th_r, _, _, cr_r, ci_r = zoh(lr[0:1], lr[1:2], jnp.exp(lr[2:3]))

    tau = (lax.broadcasted_iota(I32, (C_STATE, S5_ROWS), 1) // C_GROUP).astype(F32)
    mag = jnp.exp(rho_c * tau)
    e_re, e_im = mag * jnp.cos(th_c * tau), mag * jnp.sin(th_c * tau)
    ctr, cti = ctr_ref[0], cti_ref[0]
    g_re = e_re * ctr - e_im * cti
    g_im = e_re * cti + e_im * ctr
    btr, bti = bt_ref[0, 0], bt_ref[0, 1]
    bbr = btr * cr_r - bti * ci_r
    bbi = btr * ci_r + bti * cr_r
    kt = _dotp(bbr, g_re, S5_PASSES) - _dotp(bbi, g_im, S5_PASSES)
    lane = lax.broadcasted_iota(I32, (C_GROUP, S5_ROWS), 1)
    for s in range(CHUNK):
        blk = kt if s == 0 else jnp.where(lane >= s * C_GROUP, pltpu.roll(kt, s * C_GROUP, 1), 0.0)
        m_ref[0, s * C_GROUP:(s + 1) * C_GROUP, :] = blk.astype(m_ref.dtype)

    back = (CHUNK - 1 - lax.broadcasted_iota(I32, (CHUNK, C_STATE), 0)).astype(F32)
    mag2 = jnp.exp(rho_r * back)
    f_re, f_im = mag2 * jnp.cos(th_r * back), mag2 * jnp.sin(th_r * back)
    for s in range(CHUNK):
        fr, fi = f_re[s:s + 1], f_im[s:s + 1]
        w_ref[0, s * C_GROUP:(s + 1) * C_GROUP, :] = jnp.concatenate(
            [bbr * fr - bbi * fi, bbr * fi + bbi * fr], axis=1).astype(w_ref.dtype)

    v_ref[0, 0:C_STATE, :] = (g_re * lbr_c - g_im * lbi_c).astype(v_ref.dtype)
    v_ref[0, C_STATE:2 * C_STATE, :] = (-(g_re * lbi_c + g_im * lbr_c)).astype(v_ref.dtype)

    magc = jnp.exp(rho_r * float(CHUNK))
    pr, pi = magc * jnp.cos(th_r * float(CHUNK)), magc * jnp.sin(th_r * float(CHUNK))
    lc_ref[0] = jnp.concatenate([jnp.concatenate([pr, pr], axis=1),
                                 jnp.concatenate([-pi, pi], axis=1),
                                 jnp.zeros((6, 2 * C_STATE), F32)], axis=0)


def _s5_build(lam_c, lam_r, bt, ctr, cti):
    g = lam_c.shape[0]
    i3 = lambda gi: (gi, 0, 0)
    return pl.pallas_call(
        _s5_build_body,
        grid=(g,),
        in_specs=[pl.BlockSpec((1, C_STATE, 8), i3),
                  pl.BlockSpec((1, 8, C_STATE), i3),
                  pl.BlockSpec((1, 2, C_GROUP, C_STATE), lambda gi: (gi, 0, 0, 0)),
                  pl.BlockSpec((1, C_STATE, S5_ROWS), i3),
                  pl.BlockSpec((1, C_STATE, S5_ROWS), i3)],
        out_specs=[pl.BlockSpec((1, S5_ROWS, S5_ROWS), i3),
                   pl.BlockSpec((1, S5_ROWS, 2 * C_STATE), i3),
                   pl.BlockSpec((1, 2 * C_STATE, S5_ROWS), i3),
                   pl.BlockSpec((1, 8, 2 * C_STATE), i3)],
        out_shape=[jax.ShapeDtypeStruct((g, S5_ROWS, S5_ROWS), BF16),
                   jax.ShapeDtypeStruct((g, S5_ROWS, 2 * C_STATE), BF16),
                   jax.ShapeDtypeStruct((g, 2 * C_STATE, S5_ROWS), BF16),
                   jax.ShapeDtypeStruct((g, 8, 2 * C_STATE), F32)],
        compiler_params=_cparams(("parallel",)),
        name="s5_build",
    )(lam_c, lam_r, bt, ctr, cti)


def _s5_apply_body(u_ref, m_ref, w_ref, v_ref, lc_ref, y_ref, z_ref, xs_ref, *, bsz):
    u = u_ref[0]
    z_ref[...] = _dot(u, w_ref[0])
    n_chunk = u.shape[0] // bsz
    lr, li = lc_ref[0, 0:1, :], lc_ref[0, 1:2, :]

    def step(ci, x):
        r0 = pl.multiple_of(ci * bsz, bsz)
        xs_ref[pl.ds(r0, bsz), :] = x
        return x * lr + pltpu.roll(x, C_STATE, 1) * li + z_ref[pl.ds(r0, bsz), :]

    lax.fori_loop(0, n_chunk, step, jnp.zeros((bsz, 2 * C_STATE), F32))
    y_ref[0] = _dot(u, m_ref[0]) + _dot(xs_ref[...].astype(BF16), v_ref[0])


def _s5_apply(u_g, m, w, v, lc, *, bsz):
    g, rows, _ = u_g.shape
    i3 = lambda gi: (gi, 0, 0)
    body = functools.partial(_s5_apply_body, bsz=bsz)
    return pl.pallas_call(
        body,
        grid=(g,),
        in_specs=[pl.BlockSpec((1, rows, S5_ROWS), i3),
                  pl.BlockSpec((1, S5_ROWS, S5_ROWS), i3),
                  pl.BlockSpec((1, S5_ROWS, 2 * C_STATE), i3),
                  pl.BlockSpec((1, 2 * C_STATE, S5_ROWS), i3),
                  pl.BlockSpec((1, 8, 2 * C_STATE), i3)],
        out_specs=pl.BlockSpec((1, rows, S5_ROWS), i3),
        out_shape=jax.ShapeDtypeStruct((g, rows, S5_ROWS), F32),
        scratch_shapes=[pltpu.VMEM((rows, 2 * C_STATE), F32), pltpu.VMEM((rows, 2 * C_STATE), F32)],
        compiler_params=_cparams(("parallel",)),
        name="s5_apply",
    )(u_g, m, w, v, lc)


def _pad_cols(w, width):
    return jnp.pad(w, ((0, 0), (0, width - w.shape[1])))


def _pad_rows(w, height):
    return jnp.pad(w, ((0, height - w.shape[0]), (0, 0)))


def _even_in_weights(w_in):
    o = 0
    pieces = {}
    for name, width in (("cq", A_Q_RANK), ("ckv", A_KV_RANK), ("krope", A_ROPE_DIM), ("kidx", IDX_DIM),
                        ("widx", IDX_HEADS), ("rkv", 3 * B_WIDTH), ("wl", B_DECAY_LORA),
                        ("al", B_A_LORA), ("gl", B_GATE_LORA)):
        pieces[name] = w_in[:, o:o + width]
        o += width
    w_a = jnp.concatenate([pieces["cq"], pieces["ckv"], _pad_cols(pieces["krope"], LANES),
                           _pad_cols(pieces["kidx"], LANES), _pad_cols(pieces["widx"], LANES)], axis=1)
    w_b = jnp.concatenate([pieces["rkv"], _pad_cols(pieces["wl"], LANES), _pad_cols(pieces["al"], LANES),
                           _pad_cols(pieces["gl"], 2 * LANES)], axis=1)
    return w_a.astype(BF16), w_b.astype(BF16)


def _pad_mu(mu):
    o = 3 * B_WIDTH
    wl = mu[o:o + B_DECAY_LORA]
    al = mu[o + B_DECAY_LORA:o + B_DECAY_LORA + B_A_LORA]
    gl = mu[o + B_DECAY_LORA + B_A_LORA:]
    z = lambda n: jnp.zeros((n,), F32)
    return jnp.concatenate([mu[:o], wl, z(LANES - B_DECAY_LORA), al, z(LANES - B_A_LORA),
                            gl, z(2 * LANES - B_GATE_LORA)])[None, :]


def _dsa_mixer(h_a3, start_frame, cq_norm, ckv_norm, kidx_norm, w_uq, w_uk, w_uv, w_qidx):
    wuq = w_uq.reshape(A_Q_RANK, A_HEADS * A_HEAD_DIM).astype(BF16)
    wuk_t = jnp.pad(jnp.transpose(w_uk, (1, 2, 0)), ((0, 0), (A_ROPE_DIM, 0), (0, 0))).astype(BF16)
    wqi = jnp.pad(w_qidx, ((0, 0), (0, 0), (0, LANES - IDX_DIM))).reshape(A_Q_RANK, IDX_HEADS * LANES).astype(BF16)
    wuv = jnp.transpose(w_uv, (1, 0, 2)).astype(BF16)
    qcat, kcat, qidx, kidx, widx = _dsa_proj(
        h_a3, start_frame, _rope_tables(), cq_norm[None, :], ckv_norm[None, :],
        _pad_cols(kidx_norm[None, :], LANES), wuq, wuk_t, wqi)
    return _dsa_attend(qidx, widx, kidx, qcat, kcat, wuv)


def _rwkv_mixer(h_b3, mu, w0, w2, a0, a2, g2, k_k, k_a, r_k, ln_w, ln_b):
    rkv, ld, a, g = _rwkv_prep(h_b3, _pad_mu(mu), w0[None, :], a0[None, :],
                               _pad_rows(w2, LANES).astype(BF16), _pad_rows(a2, LANES).astype(BF16),
                               _pad_rows(g2, 2 * LANES).astype(BF16))
    par = jnp.concatenate([k_k[None, :], k_a[None, :], r_k.reshape(1, B_WIDTH), ln_w[None, :], ln_b[None, :],
                           jnp.zeros((3, B_WIDTH), F32)], axis=0)
    return _rwkv_scan(rkv, ld, a, g, par)


def _even_mixer(x2, bsz, seq, start_frame, g_mix, w_in, w_out, cq_norm, ckv_norm, kidx_norm, w_uq, w_uk, w_uv,
                w_qidx, mu, w0, w2, a0, a2, g2, k_k, k_a, r_k, ln_w, ln_b):
    w_a, w_b = _even_in_weights(w_in)
    rms_pro = lambda x, g: _rms(x, g)
    h_a = _matmul([x2], [g_mix], w_a, [], prologue=rms_pro, epilogue=_epi_id, out_dtype=F32,
                  tm=512, tn=A_PAD_COLS, name="even_in_a")
    h_b = _matmul([x2], [g_mix], w_b, [], prologue=rms_pro, epilogue=_epi_id, out_dtype=F32,
                  tm=512, tn=B_PAD_COLS // 4, name="even_in_b")
    y_a = _dsa_mixer(h_a.reshape(bsz, seq, A_PAD_COLS), start_frame, cq_norm, ckv_norm, kidx_norm,
                     w_uq, w_uk, w_uv, w_qidx)
    y_b = _rwkv_mixer(h_b.reshape(bsz, seq, B_PAD_COLS), mu, w0, w2, a0, a2, g2, k_k, k_a, r_k, ln_w, ln_b)

    m = bsz * seq
    cat_pro = lambda ya, yb: jnp.concatenate([ya, yb], axis=-1)
    return _matmul([y_a.reshape(m, A_WIDTH), y_b.reshape(m, B_WIDTH)], [], w_out.astype(BF16), [(x2, "tile")],
                   prologue=cat_pro, epilogue=_epi_add, out_dtype=F32, tm=512, tn=512, name="even_out")


def _gelu_tanh(y):
    return 0.5 * y * (1.0 + jnp.tanh(math.sqrt(2.0 / math.pi) * (y + 0.044715 * (y * y * y))))


def _odd_mixer(x2, bsz, seq, g_mix, w_in, w_out, lam_re, lam_im, log_dt, b_re, b_im, c_re, c_im, d_skip,
               w_glu, b_glu):
    u = _matmul([x2], [g_mix], w_in.astype(BF16), [], prologue=lambda x, g: _rms(x, g), epilogue=_epi_id,
                out_dtype=F32, tm=512, tn=512, name="odd_in")
    gated = _s5_mixer(u, bsz, seq, lam_re, lam_im, log_dt, b_re, b_im, c_re, c_im, d_skip, w_glu, b_glu)
    return _matmul([gated], [], w_out.astype(BF16), [(x2, "tile")], prologue=_pro_id, epilogue=_epi_add,
                   out_dtype=F32, tm=512, tn=512, name="odd_out")


def _s5_mixer(u, bsz, seq, lam_re, lam_im, log_dt, b_re, b_im, c_re, c_im, d_skip, w_glu, b_glu):
    m = bsz * seq
    n_chunk = seq // CHUNK
    assert bsz % 8 == 0, "chunk rows are addressed as whole sublane tiles"

    ldt = jnp.broadcast_to(log_dt[:, None], lam_re.shape)
    lam_r = jnp.pad(jnp.stack([lam_re, lam_im, ldt], axis=1), ((0, 0), (0, 5), (0, 0)))
    lam_c = jnp.transpose(lam_r, (0, 2, 1))
    bt = jnp.stack([jnp.transpose(b_re, (0, 2, 1)), jnp.transpose(b_im, (0, 2, 1))], axis=1)
    ctr = jnp.tile(jnp.transpose(c_re, (0, 2, 1)), (1, 1, CHUNK))
    cti = jnp.tile(jnp.transpose(c_im, (0, 2, 1)), (1, 1, CHUNK))
    m_mat, w_mat, v_mat, lc = _s5_build(lam_c, lam_r, bt, ctr, cti)

    u_g = jnp.transpose(u.astype(BF16).reshape(bsz, n_chunk, CHUNK, C_GROUPS, C_GROUP), (3, 1, 0, 2, 4))
    y_g = _s5_apply(u_g.reshape(C_GROUPS, n_chunk * bsz, S5_ROWS), m_mat, w_mat, v_mat, lc, bsz=bsz)
    y = jnp.transpose(y_g.reshape(C_GROUPS, n_chunk, bsz, CHUNK, C_GROUP), (2, 1, 3, 0, 4)).reshape(m, C_WIDTH)

    z_of = lambda yy, uu, dd: _gelu_tanh(yy + dd * uu)
    return _matmul([y, u], [d_skip[None, :]], w_glu.astype(BF16),
                   [(y, "tile"), (u, "tile"), (d_skip[None, :], "row"), (b_glu[None, :], "row")],
                   prologue=z_of,
                   epilogue=lambda acc, yy, uu, dd, bb: z_of(yy, uu, dd) * jax.nn.sigmoid(acc + bb),
                   out_dtype=BF16, tm=512, tn=512, name="s5_glu")


def _final_norm_body(x_ref, g_ref, o_ref):
    o_ref[...] = _rms(x_ref[...], g_ref[...])


def _final_norm(x2, g, *, tm=512):
    m, d = x2.shape
    tm = min(tm, m)
    return pl.pallas_call(
        _final_norm_body,
        grid=(m // tm,),
        in_specs=[pl.BlockSpec((tm, d), lambda i: (i, 0)), pl.BlockSpec((1, d), lambda i: (0, 0))],
        out_specs=pl.BlockSpec((tm, d), lambda i: (i, 0)),
        out_shape=jax.ShapeDtypeStruct((m, d), F32),
        compiler_params=_cparams(("parallel",)),
        name="final_norm",
    )(x2, g)


def kernel(x, mem, start_frame, norm_mix, norm_xattn, norm_mem, norm_ffn, final_norm, xattn_wq, xattn_wkv, xattn_wo, ffn_up, ffn_down, even_w_in, even_w_out, dsa_cq_norm, dsa_ckv_norm, dsa_kidx_norm, dsa_w_uq, dsa_w_uk, dsa_w_uv, dsa_w_qidx, rwkv_mu, rwkv_w0, rwkv_w2, rwkv_a0, rwkv_a2, rwkv_g2, rwkv_k_k, rwkv_k_a, rwkv_r_k, rwkv_ln_w, rwkv_ln_b, odd_w_in, odd_w_out, s5_lam_re, s5_lam_im, s5_log_dt, s5_b_re, s5_b_im, s5_c_re, s5_c_im, s5_d, s5_w_glu, s5_b_glu):
    bsz, seq, d = x.shape
    depth = norm_mix.shape[0]
    m = bsz * seq
    x2 = x.reshape(m, d)
    mem2 = mem.reshape(bsz * mem.shape[1], d)
    for layer in range(depth):
        i = layer // 2
        g_mix = norm_mix[layer][None, :]
        if layer % 2 == 0:
            x2 = _even_mixer(x2, bsz, seq, start_frame, g_mix, even_w_in[i], even_w_out[i], dsa_cq_norm[i],
                             dsa_ckv_norm[i], dsa_kidx_norm[i], dsa_w_uq[i], dsa_w_uk[i], dsa_w_uv[i],
                             dsa_w_qidx[i], rwkv_mu[i], rwkv_w0[i], rwkv_w2[i], rwkv_a0[i], rwkv_a2[i],
                             rwkv_g2[i], rwkv_k_k[i], rwkv_k_a[i], rwkv_r_k[i], rwkv_ln_w[i], rwkv_ln_b[i])
        else:
            x2 = _odd_mixer(x2, bsz, seq, g_mix, odd_w_in[i], odd_w_out[i], s5_lam_re[i], s5_lam_im[i],
                            s5_log_dt[i], s5_b_re[i], s5_b_im[i], s5_c_re[i], s5_c_im[i], s5_d[i],
                            s5_w_glu[i], s5_b_glu[i])
        kv = _matmul([mem2], [norm_mem[layer][None, :]], xattn_wkv[layer].astype(BF16), [],
                     prologue=lambda a, g: _rms(a, g), epilogue=_epi_id, out_dtype=BF16, tm=512, tn=512,
                     name="xattn_kv")
        x2 = _xattn(x2.reshape(bsz, seq, d), norm_xattn[layer][None, :], xattn_wq[layer].astype(BF16),
                    kv.reshape(bsz, mem.shape[1], -1), xattn_wo[layer].astype(BF16)).reshape(m, d)
        x2 = _ffn(x2, norm_ffn[layer][None, :], ffn_up[layer].astype(BF16), ffn_down[layer].astype(BF16))
    return _final_norm(x2, final_norm[None, :]).reshape(bsz, seq, d)
```

```python
import functools
import math

import numpy as np
import jax
import jax.numpy as jnp
from jax import lax
from jax.experimental import pallas as pl
from jax.experimental.pallas import tpu as pltpu

F32 = jnp.float32
BF16 = jnp.bfloat16
I32 = jnp.int32

D_MODEL = 2048
CHUNK = 64
ROPE_THETA = 500000.0
NORM_EPS = 1e-5

A_HEADS = 8
A_HEAD_DIM = 128
A_ROPE_DIM = 32
A_V_DIM = 128
A_WIDTH = A_HEADS * A_V_DIM
A_Q_RANK = 512
A_KV_RANK = 256
IDX_HEADS = 16
IDX_DIM = 64
IDX_ROPE_DIM = 16
TOPK_MAX = 256

B_HEAD = 64
B_WIDTH = 1024
B_HEADS = 16
B_DECAY_LORA = 64
B_A_LORA = 64
B_GATE_LORA = 160
B_LN_EPS = 64e-5

C_WIDTH = 2048
C_GROUP = 16
C_GROUPS = 128
C_STATE = 64

X_HEADS = 4
X_HEAD_DIM = 128
FFN_DIM = 4 * D_MODEL

LANES = 128
VMEM_LIMIT = 56 * 1024 * 1024

A_PAD_COLS = A_Q_RANK + A_KV_RANK + 3 * LANES
B_LORA_COLS = 2 * LANES + 2 * LANES
B_PAD_COLS = 3 * B_WIDTH + B_LORA_COLS
KCAT = A_KV_RANK + LANES

INT_MIN = -2147483648
NEG_BIG = -1e30
M_FLOOR = -1e20
COUNT_ROWS = 32


def _cparams(sem):
    return pltpu.CompilerParams(dimension_semantics=sem, vmem_limit_bytes=VMEM_LIMIT)


def _rms(x, g):
    return x * lax.rsqrt(jnp.mean(x * x, axis=-1, keepdims=True) + NORM_EPS) * g


def _dot(a, b):
    return jnp.dot(a, b, preferred_element_type=F32)


def _dot_nt(a, b):
    return lax.dot_general(a, b, (((1,), (1,)), ((), ())), preferred_element_type=F32)


def _dot_tn(a, b):
    return lax.dot_general(a, b, (((0,), (0,)), ((), ())), preferred_element_type=F32)


def _split2(x):
    hi = x.astype(BF16)
    lo = (x - hi.astype(F32)).astype(BF16)
    return hi, lo


def _dotp(a, b, passes, kind="nn"):
    f = {"nn": _dot, "nt": _dot_nt, "tn": _dot_tn}[kind]
    if passes == 1:
        return f(a.astype(BF16), b.astype(BF16))
    ah, al = _split2(a)
    bh, bl = _split2(b)
    return f(ah, bh) + (f(ah, bl) + f(al, bh))


def _mm_body(*refs, n_row, n_const, epi_kinds, single_n, prologue, epilogue):
    rows = refs[:n_row]
    consts = refs[n_row:n_row + n_const]
    w_ref = refs[n_row + n_const]
    epi_refs = iter(refs[n_row + n_const + 1:])
    epis = [rows[kind] if isinstance(kind, int) else next(epi_refs) for kind in epi_kinds]
    if single_n:
        o_ref = refs[-1]
        xs = prologue(*[r[...] for r in rows], *[c[...] for c in consts]).astype(BF16)
    else:
        o_ref, xs_ref = refs[-2], refs[-1]

        @pl.when(pl.program_id(1) == 0)
        def _():
            xs_ref[...] = prologue(*[r[...] for r in rows], *[c[...] for c in consts]).astype(BF16)

        xs = xs_ref[...]
    acc = _dot(xs, w_ref[...])
    o_ref[...] = epilogue(acc, *[e[...] for e in epis]).astype(o_ref.dtype)


def _matmul(rows, consts, w, epis, *, prologue, epilogue, out_dtype, tm, tn, name):
    m = rows[0].shape[0]
    k, n = w.shape
    tm = min(tm, m)
    tn = min(tn, n)
    assert m % tm == 0 and n % tn == 0, (m, tm, n, tn)
    single_n = tn == n
    in_specs = [pl.BlockSpec((tm, r.shape[1]), lambda i, j: (i, 0)) for r in rows]
    in_specs += [pl.BlockSpec(c.shape, lambda i, j: (0, 0)) for c in consts]
    in_specs += [pl.BlockSpec((k, tn), lambda i, j: (0, j))]
    for _, kind in epis:
        if kind == "tile":
            in_specs.append(pl.BlockSpec((tm, tn), lambda i, j: (i, j)))
        elif kind == "row":
            in_specs.append(pl.BlockSpec((1, tn), lambda i, j: (0, j)))
        else:
            assert single_n and rows[kind].shape[1] == n
    body = functools.partial(_mm_body, n_row=len(rows), n_const=len(consts),
                             epi_kinds=tuple(kind for _, kind in epis), single_n=single_n,
                             prologue=prologue, epilogue=epilogue)
    return pl.pallas_call(
        body,
        grid=(m // tm, n // tn),
        in_specs=in_specs,
        out_specs=pl.BlockSpec((tm, tn), lambda i, j: (i, j)),
        out_shape=jax.ShapeDtypeStruct((m, n), out_dtype),
        scratch_shapes=[] if single_n else [pltpu.VMEM((tm, k), BF16)],
        compiler_params=_cparams(("parallel", "arbitrary")),
        name=name,
    )(*rows, *consts, w, *[e for e, kind in epis if not isinstance(kind, int)])


def _epi_id(acc):
    return acc


def _epi_add(acc, res):
    return acc + res


def _pro_id(x):
    return x


def _ffn_body(x_ref, g_ref, up_ref, dn_ref, o_ref, xs_ref):
    @pl.when(pl.program_id(1) == 0)
    def _():
        x = x_ref[...]
        xs_ref[...] = _rms(x, g_ref[...]).astype(BF16)
        o_ref[...] = x

    h = _dot(xs_ref[...], up_ref[...])
    h = jnp.square(jnp.maximum(h, 0.0)).astype(BF16)
    o_ref[...] += _dot(h, dn_ref[...])


def _ffn(x, g, w_up, w_dn, *, tm=1024, tf=512):
    m, d = x.shape
    f = w_up.shape[1]
    tm = min(tm, m)
    return pl.pallas_call(
        _ffn_body,
        grid=(m // tm, f // tf),
        in_specs=[pl.BlockSpec((tm, d), lambda i, j: (i, 0)),
                  pl.BlockSpec((1, d), lambda i, j: (0, 0)),
                  pl.BlockSpec((d, tf), lambda i, j: (0, j)),
                  pl.BlockSpec((tf, d), lambda i, j: (j, 0))],
        out_specs=pl.BlockSpec((tm, d), lambda i, j: (i, 0)),
        out_shape=jax.ShapeDtypeStruct((m, d), F32),
        scratch_shapes=[pltpu.VMEM((tm, d), BF16)],
        compiler_params=_cparams(("parallel", "arbitrary")),
        name="ffn",
    )(x, g, w_up, w_dn)


def _xattn_body(x_ref, g_ref, wq_ref, kv_ref, wo_ref, o_ref):
    x = x_ref[0]
    xn = _rms(x, g_ref[...]).astype(BF16)
    q = _dot(xn, wq_ref[...])
    width = X_HEADS * X_HEAD_DIM
    outs = []
    for h in range(X_HEADS):
        sl = slice(h * X_HEAD_DIM, (h + 1) * X_HEAD_DIM)
        qh = q[:, sl].astype(BF16)
        kh = kv_ref[0, :, sl]
        vh = kv_ref[0, :, width + h * X_HEAD_DIM:width + (h + 1) * X_HEAD_DIM]
        s = _dot_nt(qh, kh) * (X_HEAD_DIM ** -0.5)
        s = s - jnp.max(s, axis=-1, keepdims=True)
        p = jnp.exp(s)
        p = p / jnp.sum(p, axis=-1, keepdims=True)
        outs.append(_dot(p.astype(BF16), vh))
    o = jnp.concatenate(outs, axis=-1).astype(BF16)
    o_ref[0] = x + _dot(o, wo_ref[...])


def _xattn(x3, g, wq, kv, wo, *, tm=512):
    b, s, d = x3.shape
    tm = min(tm, s)
    mlen = kv.shape[1]
    width = X_HEADS * X_HEAD_DIM
    return pl.pallas_call(
        _xattn_body,
        grid=(b, s // tm),
        in_specs=[pl.BlockSpec((1, tm, d), lambda bi, i: (bi, i, 0)),
                  pl.BlockSpec((1, d), lambda bi, i: (0, 0)),
                  pl.BlockSpec((d, width), lambda bi, i: (0, 0)),
                  pl.BlockSpec((1, mlen, 2 * width), lambda bi, i: (bi, 0, 0)),
                  pl.BlockSpec((width, d), lambda bi, i: (0, 0))],
        out_specs=pl.BlockSpec((1, tm, d), lambda bi, i: (bi, i, 0)),
        out_shape=jax.ShapeDtypeStruct((b, s, d), F32),
        compiler_params=_cparams(("parallel", "parallel")),
        name="xattn",
    )(x3, g, wq, kv, wo)


def _rope_tables():
    inv_a = ROPE_THETA ** (-jnp.arange(0, A_ROPE_DIM, 2, dtype=F32) / A_ROPE_DIM)
    inv_i = ROPE_THETA ** (-jnp.arange(0, IDX_ROPE_DIM, 2, dtype=F32) / IDX_ROPE_DIM)
    ha, hi = A_ROPE_DIM // 2, IDX_ROPE_DIM // 2

    def row(*pieces):
        r = jnp.concatenate([jnp.asarray(p, F32) for p in pieces])
        return jnp.pad(r, (0, LANES - r.shape[0]))

    ones, zeros = (lambda n: jnp.ones((n,), F32)), (lambda n: jnp.zeros((n,), F32))
    return jnp.stack([row(inv_a, inv_a), row(ones(2 * ha)), row(-ones(ha)), row(zeros(ha), ones(ha)),
                      row(inv_i, inv_i), row(-ones(hi)), row(zeros(hi), ones(hi)), zeros(LANES)])


def _dsa_proj_body(sf_ref, h_ref, tab_ref, nq_ref, nkv_ref, nki_ref, wuq_ref, wuk_ref, wqi_ref,
                   qcat_ref, kcat_ref, qidx_ref, kidx_ref, widx_ref):
    bi, i = pl.program_id(0), pl.program_id(1)
    tm = h_ref.shape[1]
    ha, hi = A_ROPE_DIM // 2, IDX_ROPE_DIM // 2
    pos = (sf_ref[bi] + i * tm + lax.broadcasted_iota(I32, (tm, LANES), 0)).astype(F32)
    tab = tab_ref[...]
    ang_a = pos * tab[0:1]
    cos_a = jnp.cos(ang_a) * tab[1:2]
    sin_a = jnp.sin(ang_a)
    sin_a1, sin_a2 = sin_a * tab[2:3], sin_a * tab[3:4]
    ang_i = pos * tab[4:5]
    cos_i = jnp.cos(ang_i)
    sin_i = jnp.sin(ang_i)
    sin_i1, sin_i2 = sin_i * tab[5:6], sin_i * tab[6:7]

    def rope_a(xb):
        return xb * cos_a + pltpu.roll(xb, LANES - ha, 1) * sin_a1 + pltpu.roll(xb, ha, 1) * sin_a2

    def rope_i(xb):
        return xb * cos_i + pltpu.roll(xb, LANES - hi, 1) * sin_i1 + pltpu.roll(xb, hi, 1) * sin_i2

    c_q = h_ref[0, :, 0:A_Q_RANK]
    c_kv = h_ref[0, :, A_Q_RANK:A_Q_RANK + A_KV_RANK]
    off = A_Q_RANK + A_KV_RANK
    k_rope = h_ref[0, :, off:off + LANES]
    k_idx = h_ref[0, :, off + LANES:off + 2 * LANES]
    w_idx = h_ref[0, :, off + 2 * LANES:off + 3 * LANES]

    cqn = _rms(c_q, nq_ref[...]).astype(BF16)
    kcat_ref[0, :, 0:A_KV_RANK] = _rms(c_kv, nkv_ref[...]).astype(BF16)
    kcat_ref[0, :, A_KV_RANK:KCAT] = rope_a(k_rope).astype(BF16)

    ki = k_idx * lax.rsqrt(jnp.sum(k_idx * k_idx, axis=-1, keepdims=True) * (1.0 / IDX_DIM) + NORM_EPS)
    kidx_ref[0] = rope_i(ki * nki_ref[...]).astype(BF16)
    widx_ref[0] = w_idx * ((IDX_HEADS ** -0.5) * (IDX_DIM ** -0.5))

    q = _dot(cqn, wuq_ref[...])
    for h in range(A_HEADS):
        qh = q[:, h * A_HEAD_DIM:(h + 1) * A_HEAD_DIM]
        qcat_ref[0, h, :, 0:A_KV_RANK] = _dot(qh.astype(BF16), wuk_ref[h]).astype(BF16)
        qcat_ref[0, h, :, A_KV_RANK:KCAT] = rope_a(qh).astype(BF16)
    qi = _dot(cqn, wqi_ref[...])
    for h in range(IDX_HEADS):
        qidx_ref[0, h] = rope_i(qi[:, h * LANES:(h + 1) * LANES]).astype(BF16)


def _dsa_proj(h_a, start_frame, tab, nq, nkv, nki, wuq, wuk_t, wqi, *, tm=256):
    b, s, _ = h_a.shape
    tm = min(tm, s)
    c3 = lambda bi, i, sf: (bi, i, 0)
    c4 = lambda bi, i, sf: (bi, 0, i, 0)
    z2 = lambda bi, i, sf: (0, 0)
    z3 = lambda bi, i, sf: (0, 0, 0)
    grid_spec = pltpu.PrefetchScalarGridSpec(
        num_scalar_prefetch=1,
        grid=(b, s // tm),
        in_specs=[pl.BlockSpec((1, tm, A_PAD_COLS), c3),
                  pl.BlockSpec(tab.shape, z2),
                  pl.BlockSpec(nq.shape, z2),
                  pl.BlockSpec(nkv.shape, z2),
                  pl.BlockSpec(nki.shape, z2),
                  pl.BlockSpec(wuq.shape, z2),
                  pl.BlockSpec(wuk_t.shape, z3),
                  pl.BlockSpec(wqi.shape, z2)],
        out_specs=[pl.BlockSpec((1, A_HEADS, tm, KCAT), c4),
                   pl.BlockSpec((1, tm, KCAT), c3),
                   pl.BlockSpec((1, IDX_HEADS, tm, LANES), c4),
                   pl.BlockSpec((1, tm, LANES), c3),
                   pl.BlockSpec((1, tm, LANES), c3)],
    )
    return pl.pallas_call(
        _dsa_proj_body,
        grid_spec=grid_spec,
        out_shape=[jax.ShapeDtypeStruct((b, A_HEADS, s, KCAT), BF16),
                   jax.ShapeDtypeStruct((b, s, KCAT), BF16),
                   jax.ShapeDtypeStruct((b, IDX_HEADS, s, LANES), BF16),
                   jax.ShapeDtypeStruct((b, s, LANES), BF16),
                   jax.ShapeDtypeStruct((b, s, LANES), F32)],
        compiler_params=_cparams(("parallel", "parallel")),
        name="dsa_proj",
    )(start_frame, h_a, tab, nq, nkv, nki, wuq, wuk_t, wqi)


def _dsa_attend_body(qidx_ref, widx_ref, kidx_ref, qcat_ref, kcat_ref, wuv_ref, o_ref,
                     key_ref, m_ref, l_ref, a_ref, acc_ref, t_ref, p_ref, *, tq, kt, topk):
    i = pl.program_id(1)
    t0 = i * tq
    nk = (t0 + tq + kt - 1) // kt
    krow = lax.broadcasted_iota(I32, (kt, tq), 0)
    qcol = lax.broadcasted_iota(I32, (kt, tq), 1)
    limit = ((t0 + qcol) // CHUNK + 1) * CHUNK

    wt = widx_ref[0]

    def score_tile(j, carry):
        off = pl.multiple_of(j * kt, kt)
        kid = kidx_ref[0, pl.ds(off, kt), :]
        sc = jnp.zeros((kt, tq), F32)
        for h in range(IDX_HEADS):
            sc = sc + wt[h:h + 1, :] * jnp.maximum(_dot_nt(kid, qidx_ref[0, h]), 0.0)
        bits = pltpu.bitcast(sc, I32)
        key = jnp.where(bits >= 0, bits, bits ^ 0x7FFFFFFF)
        key_ref[j] = jnp.where(off + krow < limit, key, INT_MIN)
        return carry

    lax.fori_loop(0, nk, score_tile, 0)

    def count_ge(cand):
        def body(j, acc):
            hit = jnp.where(key_ref[j] >= cand, 1.0, 0.0)
            return acc + jnp.sum(hit.reshape(kt // COUNT_ROWS, COUNT_ROWS, tq), axis=0)
        acc = lax.fori_loop(0, nk, body, jnp.zeros((COUNT_ROWS, tq), F32))
        return jnp.sum(acc, axis=0, keepdims=True)

    kf = float(topk)
    thr = jnp.where(count_ge(jnp.zeros((1, tq), I32)) >= kf, 0, INT_MIN).astype(I32)

    def bit_step(it, thr):
        cand = thr | jnp.left_shift(jnp.int32(1), 30 - it)
        return jnp.where(count_ge(cand) >= kf, cand, thr)

    thr = lax.fori_loop(0, 31, bit_step, thr)
    thr = jnp.maximum(thr, INT_MIN + 1)

    m_ref[...] = jnp.full(m_ref.shape, M_FLOOR, F32)
    l_ref[...] = jnp.zeros(l_ref.shape, F32)
    acc_ref[...] = jnp.zeros(acc_ref.shape, F32)
    c_exp = (A_HEAD_DIM ** -0.5) * math.log2(math.e)
    eye_q = jnp.where(lax.broadcasted_iota(I32, (tq, tq), 0) == lax.broadcasted_iota(I32, (tq, tq), 1),
                      1.0, 0.0).astype(BF16)

    def attend_tile(j, carry):
        off = pl.multiple_of(j * kt, kt)
        kc = kcat_ref[0, pl.ds(off, kt), :]
        ckv = kc[:, 0:A_KV_RANK]
        bias_t = jnp.where(key_ref[j] >= thr, 0.0, NEG_BIG).astype(BF16)
        bias = _dot_nt(eye_q, bias_t)
        for h in range(A_HEADS):
            t_ref[h] = _dot_nt(qcat_ref[0, h], kc) * c_exp + bias
        for h in range(A_HEADS):
            t = t_ref[h]
            m_old = m_ref[h]
            m_new = jnp.maximum(m_old, jnp.max(t, axis=-1, keepdims=True))
            p = jnp.exp2(t - m_new)
            alpha = jnp.exp2(m_old - m_new)
            l_ref[h] = alpha * l_ref[h] + jnp.sum(p, axis=-1, keepdims=True)
            p_ref[h] = p.astype(BF16)
            a_ref[h] = alpha
            m_ref[h] = m_new
        for h in range(A_HEADS):
            acc_ref[h] = a_ref[h] * acc_ref[h] + _dot(p_ref[h], ckv)
        return carry

    lax.fori_loop(0, nk, attend_tile, 0)

    o_lat = acc_ref[...] / l_ref[...]
    for h in range(A_HEADS):
        o_ref[0, :, h * A_V_DIM:(h + 1) * A_V_DIM] = _dot(o_lat[h].astype(BF16), wuv_ref[h]).astype(o_ref.dtype)


def _dsa_attend(qidx, widx, kidx, qcat, kcat, wuv, *, tq=256, kt=512):
    b, _, s, _ = qidx.shape
    kt = min(kt, s)
    tq = min(tq, s)
    topk = min(TOPK_MAX, s // 4)
    assert s % kt == 0 and s % tq == 0 and tq % CHUNK == 0
    body = functools.partial(_dsa_attend_body, tq=tq, kt=kt, topk=topk)
    widx_t = jnp.transpose(widx[:, :, 0:IDX_HEADS], (0, 2, 1))
    return pl.pallas_call(
        body,
        grid=(b, s // tq),
        in_specs=[pl.BlockSpec((1, IDX_HEADS, tq, LANES), lambda bi, i: (bi, 0, i, 0)),
                  pl.BlockSpec((1, IDX_HEADS, tq), lambda bi, i: (bi, 0, i)),
                  pl.BlockSpec((1, s, LANES), lambda bi, i: (bi, 0, 0)),
                  pl.BlockSpec((1, A_HEADS, tq, KCAT), lambda bi, i: (bi, 0, i, 0)),
                  pl.BlockSpec((1, s, KCAT), lambda bi, i: (bi, 0, 0)),
                  pl.BlockSpec(wuv.shape, lambda bi, i: (0, 0, 0))],
        out_specs=pl.BlockSpec((1, tq, A_WIDTH), lambda bi, i: (bi, i, 0)),
        out_shape=jax.ShapeDtypeStruct((b, s, A_WIDTH), BF16),
        scratch_shapes=[pltpu.VMEM((s // kt, kt, tq), I32),
                        pltpu.VMEM((A_HEADS, tq, 1), F32),
                        pltpu.VMEM((A_HEADS, tq, 1), F32),
                        pltpu.VMEM((A_HEADS, tq, 1), F32),
                        pltpu.VMEM((A_HEADS, tq, A_KV_RANK), F32),
                        pltpu.VMEM((A_HEADS, tq, kt), F32),
                        pltpu.VMEM((A_HEADS, tq, kt), BF16)],
        compiler_params=_cparams(("parallel", "arbitrary")),
        name="dsa_attend",
    )(qidx, widx_t, kidx, qcat, kcat, wuv)


def _rwkv_prep_body(h_ref, prev_ref, mu_ref, w0_ref, a0_ref, w2_ref, a2_ref, g2_ref,
                    rkv_ref, ld_ref, a_ref, g_ref):
    i = pl.program_id(1)
    h = h_ref[0]
    tm = h.shape[0]
    prev = jnp.where(i > 0, prev_ref[0, 7:8, :], 0.0)
    row = lax.broadcasted_iota(I32, h.shape, 0)
    shifted = jnp.where(row == 0, prev, pltpu.roll(h, 1, 0))
    hm = h + (shifted - h) * mu_ref[...]
    w3 = 3 * B_WIDTH
    rkv_ref[0] = hm[:, 0:w3]
    wl = hm[:, w3:w3 + LANES]
    al = hm[:, w3 + LANES:w3 + 2 * LANES]
    gl = hm[:, w3 + 2 * LANES:w3 + 4 * LANES]
    wz = w0_ref[...] + _dot(jnp.tanh(wl).astype(BF16), w2_ref[...])
    log_w = -jax.nn.softplus(-wz) - 0.5
    ld_ref[0] = -jnp.exp(log_w)
    a_ref[0] = jax.nn.sigmoid(a0_ref[...] + _dot(al.astype(BF16), a2_ref[...]))
    g_ref[0] = _dot(jax.nn.sigmoid(gl).astype(BF16), g2_ref[...])


def _rwkv_prep(h_b, mu, w0, a0, w2, a2, g2, *, tm=256):
    b, s, cols = h_b.shape
    tm = min(tm, s)
    c3 = lambda bi, i: (bi, i, 0)
    z2 = lambda bi, i: (0, 0)
    prev_map = lambda bi, i: (bi, jnp.maximum(i * (tm // 8) - 1, 0), 0)
    out = lambda w: jax.ShapeDtypeStruct((b, s, w), F32)
    return pl.pallas_call(
        _rwkv_prep_body,
        grid=(b, s // tm),
        in_specs=[pl.BlockSpec((1, tm, cols), c3),
                  pl.BlockSpec((1, 8, cols), prev_map),
                  pl.BlockSpec(mu.shape, z2), pl.BlockSpec(w0.shape, z2), pl.BlockSpec(a0.shape, z2),
                  pl.BlockSpec(w2.shape, z2), pl.BlockSpec(a2.shape, z2), pl.BlockSpec(g2.shape, z2)],
        out_specs=[pl.BlockSpec((1, tm, 3 * B_WIDTH), c3), pl.BlockSpec((1, tm, B_WIDTH), c3),
                   pl.BlockSpec((1, tm, B_WIDTH), c3), pl.BlockSpec((1, tm, B_WIDTH), c3)],
        out_shape=[out(3 * B_WIDTH), out(B_WIDTH), out(B_WIDTH), out(B_WIDTH)],
        compiler_params=_cparams(("parallel", "parallel")),
        name="rwkv_prep",
    )(h_b, h_b, mu, w0, a0, w2, a2, g2)


RWKV_CHUNK = 64
RWKV_SUB = 16


def _bmm(a, b, kind="nn"):
    spec = {"nn": "bij,bjk->bik", "nt": "bik,bjk->bij"}[kind]
    return jnp.einsum(spec, a.astype(BF16), b.astype(BF16), preferred_element_type=F32)


def _unit_lower_inverse(l_mat, eye, same_blk):
    mm = _bmm
    ld = jnp.where(same_blk, l_mat, 0.0)
    lo = l_mat - ld
    x = eye + ld
    p = mm(ld, ld)
    x = x + mm(x, p)
    p = mm(p, p)
    x = x + mm(x, p)
    p = mm(p, p)
    d = x + mm(x, p)
    n = mm(d, lo)
    y = eye + n
    y = y + mm(y, mm(n, n))
    return mm(y, d)


def _rwkv_scan_body(r_ref, k_ref, v_ref, ld_ref, a_ref, g_ref, par_ref, o_ref, st_ref):
    c = RWKV_CHUNK
    tblk = r_ref.shape[1]
    nc = tblk // c
    nh = r_ref.shape[2] // B_HEAD

    @pl.when(pl.program_id(2) == 0)
    def _():
        st_ref[...] = jnp.zeros(st_ref.shape, F32)

    def heads(x):
        return jnp.concatenate([x[:, hd * B_HEAD:(hd + 1) * B_HEAD].reshape(nc, c, B_HEAD) for hd in range(nh)],
                               axis=0)

    def head_row(i):
        p = par_ref[i:i + 1, :]
        return jnp.concatenate([jnp.broadcast_to(p[:, hd * B_HEAD:(hd + 1) * B_HEAD][None], (nc, 1, B_HEAD))
                                for hd in range(nh)], axis=0)

    ri = lax.broadcasted_iota(I32, (c, c), 0)
    ci = lax.broadcasted_iota(I32, (c, c), 1)
    eye = jnp.where(ri == ci, 1.0, 0.0).astype(F32)[None]
    lower = (ri >= ci)[None]
    strict = (ri > ci)[None]
    same_blk = ((ri // RWKV_SUB) == (ci // RWKV_SUB))[None]

    rb = lax.broadcasted_iota(I32, (tblk, tblk), 0)
    cb = lax.broadcasted_iota(I32, (tblk, tblk), 1)
    tri = jnp.where((rb >= cb) & (rb // c == cb // c), 1.0, 0.0).astype(BF16)
    ld2 = ld_ref[0]
    l1 = ld2.astype(BF16)
    rem = ld2 - l1.astype(F32)
    l2 = rem.astype(BF16)
    l3 = (rem - l2.astype(F32)).astype(BF16)
    cs = heads(_dot(tri, l1) + (_dot(tri, l2) + _dot(tri, l3)))

    r, k, v = heads(r_ref[0]), heads(k_ref[0]), heads(v_ref[0])
    ld, a, g = heads(ld2), heads(a_ref[0]), heads(g_ref[0])
    kk = k * head_row(0)
    kk = kk * lax.rsqrt(jnp.maximum(jnp.sum(kk * kk, axis=-1, keepdims=True), 1e-24))
    km = k * (1.0 + (a - 1.0) * head_row(1))
    bv = kk * a
    cs_end = cs[:, c - 1:c, :]
    e_neg = jnp.exp(-cs)
    e_end = jnp.exp(cs_end - cs)
    at = -kk * jnp.exp(cs - ld)
    rt = r * jnp.exp(cs)
    bt, kt = bv * e_neg, km * e_neg
    bd, kd = bv * e_end, km * e_end
    p_end = jnp.exp(cs_end)

    lab = jnp.where(strict, _bmm(at, bt, "nt"), 0.0)
    lak = jnp.where(strict, _bmm(at, kt, "nt"), 0.0)
    lrb = jnp.where(lower, _bmm(rt, bt, "nt"), 0.0)
    lrk = jnp.where(lower, _bmm(rt, kt, "nt"), 0.0)
    tinv = _unit_lower_inverse(lab, eye, same_blk)
    tat = _bmm(tinv, at)
    c1 = _bmm(tinv, _bmm(lak, v))
    r2 = rt + _bmm(lrb, tat)
    c2 = _bmm(lrk, v) + _bmm(lrb, c1)
    gmat = _bmm(jnp.swapaxes(tat, 1, 2), bd)
    c3 = _bmm(jnp.swapaxes(c1, 1, 2), bd) + _bmm(jnp.swapaxes(v, 1, 2), kd)

    outs = [None] * (nh * nc)
    for hd in range(nh):
        st = st_ref[hd]
        for ch in range(nc):
            i = hd * nc + ch
            sb = st.astype(BF16)
            outs[i] = _dot_nt(r2[i].astype(BF16), sb) + c2[i]
            st = st * p_end[i] + (_dot(sb, gmat[i].astype(BF16)) + c3[i])
        st_ref[hd] = st
    out = jnp.stack(outs, axis=0)

    mean = jnp.mean(out, axis=-1, keepdims=True)
    var = jnp.mean(jnp.square(out - mean), axis=-1, keepdims=True)
    y = (out - mean) * lax.rsqrt(var + B_LN_EPS) * head_row(3) + head_row(4)
    y = (y + jnp.sum(r * km * head_row(2), axis=-1, keepdims=True) * v) * g
    for hd in range(nh):
        o_ref[0, :, hd * B_HEAD:(hd + 1) * B_HEAD] = y[hd * nc:(hd + 1) * nc].reshape(tblk, B_HEAD).astype(o_ref.dtype)


def _rwkv_scan(rkv, ld, a, g, par, *, tblk=512, hw=2 * LANES):
    b, s, _ = ld.shape
    tblk = min(tblk, s)
    npair = B_WIDTH // hw
    blk = (1, tblk, hw)
    return pl.pallas_call(
        _rwkv_scan_body,
        grid=(b, npair, s // tblk),
        in_specs=[pl.BlockSpec(blk, lambda bi, hp, t: (bi, t, hp)),
                  pl.BlockSpec(blk, lambda bi, hp, t: (bi, t, npair + hp)),
                  pl.BlockSpec(blk, lambda bi, hp, t: (bi, t, 2 * npair + hp)),
                  pl.BlockSpec(blk, lambda bi, hp, t: (bi, t, hp)),
                  pl.BlockSpec(blk, lambda bi, hp, t: (bi, t, hp)),
                  pl.BlockSpec(blk, lambda bi, hp, t: (bi, t, hp)),
                  pl.BlockSpec((8, hw), lambda bi, hp, t: (0, hp))],
        out_specs=pl.BlockSpec(blk, lambda bi, hp, t: (bi, t, hp)),
        out_shape=jax.ShapeDtypeStruct((b, s, B_WIDTH), BF16),
        scratch_shapes=[pltpu.VMEM((hw // B_HEAD, B_HEAD, B_HEAD), F32)],
        compiler_params=_cparams(("parallel", "parallel", "arbitrary")),
        name="rwkv_scan",
    )(rkv, rkv, rkv, ld, a, g, par)


S5_ROWS = CHUNK * C_GROUP
S5_PASSES = 3


def _s5_build_body(lam_c_ref, lam_r_ref, bt_ref, ct_ref, m_ref, w_ref, v_ref, lc_ref):
    def zoh(lre, lim, dt):
        lre = jnp.minimum(lre, -1e-4)
        rho, th = lre * dt, lim * dt
        mag = jnp.exp(rho)
        lbr, lbi = mag * jnp.cos(th), mag * jnp.sin(th)
        den = 1.0 / (lre * lre + lim * lim)
        cr = ((lbr - 1.0) * lre + lbi * lim) * den
        cim = (lbi * lre - (lbr - 1.0) * lim) * den
        return rho, th, lbr, lbi, cr, cim

    lc = lam_c_ref[0]
    rho_c, th_c, lbr_c, lbi_c, _, _ = zoh(lc[:, 0:1], lc[:, 1:2], jnp.exp(lc[:, 2:3]))
    lr = lam_r_ref[0]
    rho_r, th_r, _, _, cr_r, ci_r = zoh(lr[0:1], lr[1:2], jnp.exp(lr[2:3]))

    tau = (lax.broadcasted_iota(I32, (C_STATE, S5_ROWS), 1) // C_GROUP).astype(F32)
    mag = jnp.exp(rho_c * tau)
    e_re, e_im = mag * jnp.cos(th_c * tau), mag * jnp.sin(th_c * tau)
    sel = jnp.where(lax.broadcasted_iota(I32, (C_GROUP, S5_ROWS), 1) % C_GROUP
                    == lax.broadcasted_iota(I32, (C_GROUP, S5_ROWS), 0), 1.0, 0.0).astype(BF16)

    def repeat_lanes(c):
        c1 = c.astype(BF16)
        rem = c - c1.astype(F32)
        c2 = rem.astype(BF16)
        c3 = (rem - c2.astype(F32)).astype(BF16)
        return _dot(c1, sel) + (_dot(c2, sel) + _dot(c3, sel))

    ctr, cti = repeat_lanes(ct_ref[0, 0]), repeat_lanes(ct_ref[0, 1])
    g_re = e_re * ctr - e_im * cti
    g_im = e_re * cti + e_im * ctr
    btr, bti = bt_ref[0, 0], bt_ref[0, 1]
    bbr = btr * cr_r - bti * ci_r
    bbi = btr * ci_r + bti * cr_r
    kt = _dotp(bbr, g_re, S5_PASSES) - _dotp(bbi, g_im, S5_PASSES)
    lane = lax.broadcasted_iota(I32, (C_GROUP, S5_ROWS), 1)
    for s in range(CHUNK):
        blk = kt if s == 0 else jnp.where(lane >= s * C_GROUP, pltpu.roll(kt, s * C_GROUP, 1), 0.0)
        m_ref[0, s * C_GROUP:(s + 1) * C_GROUP, :] = blk.astype(m_ref.dtype)

    back = (CHUNK - 1 - lax.broadcasted_iota(I32, (CHUNK, C_STATE), 0)).astype(F32)
    mag2 = jnp.exp(rho_r * back)
    f_re, f_im = mag2 * jnp.cos(th_r * back), mag2 * jnp.sin(th_r * back)
    for s in range(CHUNK):
        fr, fi = f_re[s:s + 1], f_im[s:s + 1]
        w_ref[0, s * C_GROUP:(s + 1) * C_GROUP, :] = jnp.concatenate(
            [bbr * fr - bbi * fi, bbr * fi + bbi * fr], axis=1).astype(w_ref.dtype)

    v_ref[0, 0:C_STATE, :] = (g_re * lbr_c - g_im * lbi_c).astype(v_ref.dtype)
    v_ref[0, C_STATE:2 * C_STATE, :] = (-(g_re * lbi_c + g_im * lbr_c)).astype(v_ref.dtype)

    magc = jnp.exp(rho_r * float(CHUNK))
    pr, pi = magc * jnp.cos(th_r * float(CHUNK)), magc * jnp.sin(th_r * float(CHUNK))
    lc_ref[0] = jnp.concatenate([jnp.concatenate([pr, pr], axis=1),
                                 jnp.concatenate([-pi, pi], axis=1),
                                 jnp.zeros((6, 2 * C_STATE), F32)], axis=0)


def _s5_build(lam_c, lam_r, bt, ct):
    g = lam_c.shape[0]
    i3 = lambda gi: (gi, 0, 0)
    return pl.pallas_call(
        _s5_build_body,
        grid=(g,),
        in_specs=[pl.BlockSpec((1, C_STATE, 8), i3),
                  pl.BlockSpec((1, 8, C_STATE), i3),
                  pl.BlockSpec((1, 2, C_GROUP, C_STATE), lambda gi: (gi, 0, 0, 0)),
                  pl.BlockSpec((1, 2, C_STATE, C_GROUP), lambda gi: (gi, 0, 0, 0))],
        out_specs=[pl.BlockSpec((1, S5_ROWS, S5_ROWS), i3),
                   pl.BlockSpec((1, S5_ROWS, 2 * C_STATE), i3),
                   pl.BlockSpec((1, 2 * C_STATE, S5_ROWS), i3),
                   pl.BlockSpec((1, 8, 2 * C_STATE), i3)],
        out_shape=[jax.ShapeDtypeStruct((g, S5_ROWS, S5_ROWS), BF16),
                   jax.ShapeDtypeStruct((g, S5_ROWS, 2 * C_STATE), BF16),
                   jax.ShapeDtypeStruct((g, 2 * C_STATE, S5_ROWS), BF16),
                   jax.ShapeDtypeStruct((g, 8, 2 * C_STATE), F32)],
        compiler_params=_cparams(("parallel",)),
        name="s5_build",
    )(lam_c, lam_r, bt, ct)


def _s5_apply_body(u_ref, m_ref, w_ref, v_ref, lc_ref, y_ref, z_ref, xs_ref, *, bsz):
    u = u_ref[0]
    z_ref[...] = _dot(u, w_ref[0])
    n_chunk = u.shape[0] // bsz
    lr, li = lc_ref[0, 0:1, :], lc_ref[0, 1:2, :]

    def step(ci, x):
        r0 = pl.multiple_of(ci * bsz, bsz)
        xs_ref[pl.ds(r0, bsz), :] = x
        return x * lr + pltpu.roll(x, C_STATE, 1) * li + z_ref[pl.ds(r0, bsz), :]

    lax.fori_loop(0, n_chunk, step, jnp.zeros((bsz, 2 * C_STATE), F32))
    y_ref[0] = _dot(u, m_ref[0]) + _dot(xs_ref[...].astype(BF16), v_ref[0])


def _s5_apply(u_g, m, w, v, lc, *, bsz):
    g, rows, _ = u_g.shape
    i3 = lambda gi: (gi, 0, 0)
    body = functools.partial(_s5_apply_body, bsz=bsz)
    return pl.pallas_call(
        body,
        grid=(g,),
        in_specs=[pl.BlockSpec((1, rows, S5_ROWS), i3),
                  pl.BlockSpec((1, S5_ROWS, S5_ROWS), i3),
                  pl.BlockSpec((1, S5_ROWS, 2 * C_STATE), i3),
                  pl.BlockSpec((1, 2 * C_STATE, S5_ROWS), i3),
                  pl.BlockSpec((1, 8, 2 * C_STATE), i3)],
        out_specs=pl.BlockSpec((1, rows, S5_ROWS), i3),
        out_shape=jax.ShapeDtypeStruct((g, rows, S5_ROWS), F32),
        scratch_shapes=[pltpu.VMEM((rows, 2 * C_STATE), F32), pltpu.VMEM((rows, 2 * C_STATE), F32)],
        compiler_params=_cparams(("parallel",)),
        name="s5_apply",
    )(u_g, m, w, v, lc)


def _pad_cols(w, width):
    return jnp.pad(w, ((0, 0), (0, width - w.shape[1])))


def _pad_rows(w, height):
    return jnp.pad(w, ((0, height - w.shape[0]), (0, 0)))


def _even_in_weights(w_in):
    o = 0
    pieces = {}
    for name, width in (("cq", A_Q_RANK), ("ckv", A_KV_RANK), ("krope", A_ROPE_DIM), ("kidx", IDX_DIM),
                        ("widx", IDX_HEADS), ("rkv", 3 * B_WIDTH), ("wl", B_DECAY_LORA),
                        ("al", B_A_LORA), ("gl", B_GATE_LORA)):
        pieces[name] = w_in[:, o:o + width]
        o += width
    w_a = jnp.concatenate([pieces["cq"], pieces["ckv"], _pad_cols(pieces["krope"], LANES),
                           _pad_cols(pieces["kidx"], LANES), _pad_cols(pieces["widx"], LANES)], axis=1)
    w_b = jnp.concatenate([pieces["rkv"], _pad_cols(pieces["wl"], LANES), _pad_cols(pieces["al"], LANES),
                           _pad_cols(pieces["gl"], 2 * LANES)], axis=1)
    return w_a.astype(BF16), w_b.astype(BF16)


def _pad_mu(mu):
    o = 3 * B_WIDTH
    wl = mu[o:o + B_DECAY_LORA]
    al = mu[o + B_DECAY_LORA:o + B_DECAY_LORA + B_A_LORA]
    gl = mu[o + B_DECAY_LORA + B_A_LORA:]
    z = lambda n: jnp.zeros((n,), F32)
    return jnp.concatenate([mu[:o], wl, z(LANES - B_DECAY_LORA), al, z(LANES - B_A_LORA),
                            gl, z(2 * LANES - B_GATE_LORA)])[None, :]


def _dsa_mixer(h_a3, start_frame, cq_norm, ckv_norm, kidx_norm, w_uq, w_uk, w_uv, w_qidx):
    wuq = w_uq.reshape(A_Q_RANK, A_HEADS * A_HEAD_DIM).astype(BF16)
    wuk_t = jnp.pad(jnp.transpose(w_uk, (1, 2, 0)), ((0, 0), (A_ROPE_DIM, 0), (0, 0))).astype(BF16)
    wqi = jnp.pad(w_qidx, ((0, 0), (0, 0), (0, LANES - IDX_DIM))).reshape(A_Q_RANK, IDX_HEADS * LANES).astype(BF16)
    wuv = jnp.transpose(w_uv, (1, 0, 2)).astype(BF16)
    qcat, kcat, qidx, kidx, widx = _dsa_proj(
        h_a3, start_frame, _rope_tables(), cq_norm[None, :], ckv_norm[None, :],
        _pad_cols(kidx_norm[None, :], LANES), wuq, wuk_t, wqi)
    return _dsa_attend(qidx, widx, kidx, qcat, kcat, wuv)


def _rwkv_mixer(h_b3, mu, w0, w2, a0, a2, g2, k_k, k_a, r_k, ln_w, ln_b):
    rkv, ld, a, g = _rwkv_prep(h_b3, _pad_mu(mu), w0[None, :], a0[None, :],
                               _pad_rows(w2, LANES).astype(BF16), _pad_rows(a2, LANES).astype(BF16),
                               _pad_rows(g2, 2 * LANES).astype(BF16))
    par = jnp.concatenate([k_k[None, :], k_a[None, :], r_k.reshape(1, B_WIDTH), ln_w[None, :], ln_b[None, :],
                           jnp.zeros((3, B_WIDTH), F32)], axis=0)
    return _rwkv_scan(rkv, ld, a, g, par)


def _even_mixer(x2, bsz, seq, start_frame, g_mix, w_in, w_out, cq_norm, ckv_norm, kidx_norm, w_uq, w_uk, w_uv,
                w_qidx, mu, w0, w2, a0, a2, g2, k_k, k_a, r_k, ln_w, ln_b):
    w_a, w_b = _even_in_weights(w_in)
    rms_pro = lambda x, g: _rms(x, g)
    h_a = _matmul([x2], [g_mix], w_a, [], prologue=rms_pro, epilogue=_epi_id, out_dtype=F32,
                  tm=512, tn=A_PAD_COLS, name="even_in_a")
    h_b = _matmul([x2], [g_mix], w_b, [], prologue=rms_pro, epilogue=_epi_id, out_dtype=F32,
                  tm=512, tn=B_PAD_COLS // 2, name="even_in_b")
    y_a = _dsa_mixer(h_a.reshape(bsz, seq, A_PAD_COLS), start_frame, cq_norm, ckv_norm, kidx_norm,
                     w_uq, w_uk, w_uv, w_qidx)
    y_b = _rwkv_mixer(h_b.reshape(bsz, seq, B_PAD_COLS), mu, w0, w2, a0, a2, g2, k_k, k_a, r_k, ln_w, ln_b)

    m = bsz * seq
    cat_pro = lambda ya, yb: jnp.concatenate([ya, yb], axis=-1)
    return _matmul([y_a.reshape(m, A_WIDTH), y_b.reshape(m, B_WIDTH)], [], w_out.astype(BF16), [(x2, "tile")],
                   prologue=cat_pro, epilogue=_epi_add, out_dtype=F32, tm=512, tn=D_MODEL, name="even_out")


def _gelu_tanh(y):
    return 0.5 * y * (1.0 + jnp.tanh(math.sqrt(2.0 / math.pi) * (y + 0.044715 * (y * y * y))))


def _odd_mixer(x2, bsz, seq, g_mix, w_in, w_out, lam_re, lam_im, log_dt, b_re, b_im, c_re, c_im, d_skip,
               w_glu, b_glu):
    u = _matmul([x2], [g_mix], w_in.astype(BF16), [], prologue=lambda x, g: _rms(x, g), epilogue=_epi_id,
                out_dtype=F32, tm=512, tn=C_WIDTH, name="odd_in")
    gated = _s5_mixer(u, bsz, seq, lam_re, lam_im, log_dt, b_re, b_im, c_re, c_im, d_skip, w_glu, b_glu)
    return _matmul([gated], [], w_out.astype(BF16), [(x2, "tile")], prologue=_pro_id, epilogue=_epi_add,
                   out_dtype=F32, tm=512, tn=D_MODEL, name="odd_out")


def _s5_mixer(u, bsz, seq, lam_re, lam_im, log_dt, b_re, b_im, c_re, c_im, d_skip, w_glu, b_glu):
    m = bsz * seq
    n_chunk = seq // CHUNK
    assert bsz % 8 == 0, "chunk rows are addressed as whole sublane tiles"

    ldt = jnp.broadcast_to(log_dt[:, None], lam_re.shape)
    lam_r = jnp.pad(jnp.stack([lam_re, lam_im, ldt], axis=1), ((0, 0), (0, 5), (0, 0)))
    lam_c = jnp.transpose(lam_r, (0, 2, 1))
    bt = jnp.stack([jnp.transpose(b_re, (0, 2, 1)), jnp.transpose(b_im, (0, 2, 1))], axis=1)
    ct = jnp.stack([jnp.transpose(c_re, (0, 2, 1)), jnp.transpose(c_im, (0, 2, 1))], axis=1)
    m_mat, w_mat, v_mat, lc = _s5_build(lam_c, lam_r, bt, ct)

    u_g = jnp.transpose(u.astype(BF16).reshape(bsz, n_chunk, CHUNK, C_GROUPS, C_GROUP), (3, 1, 0, 2, 4))
    y_g = _s5_apply(u_g.reshape(C_GROUPS, n_chunk * bsz, S5_ROWS), m_mat, w_mat, v_mat, lc, bsz=bsz)
    y = jnp.transpose(y_g.reshape(C_GROUPS, n_chunk, bsz, CHUNK, C_GROUP), (2, 1, 3, 0, 4)).reshape(m, C_WIDTH)

    z_of = lambda yy, uu, dd: _gelu_tanh(yy + dd * uu)
    return _matmul([y, u], [d_skip[None, :]], w_glu.astype(BF16),
                   [(None, 0), (None, 1), (d_skip[None, :], "row"), (b_glu[None, :], "row")],
                   prologue=z_of,
                   epilogue=lambda acc, yy, uu, dd, bb: z_of(yy, uu, dd) * jax.nn.sigmoid(acc + bb),
                   out_dtype=BF16, tm=512, tn=C_WIDTH, name="s5_glu")


def _final_norm_body(x_ref, g_ref, o_ref):
    o_ref[...] = _rms(x_ref[...], g_ref[...])


def _final_norm(x2, g, *, tm=512):
    m, d = x2.shape
    tm = min(tm, m)
    return pl.pallas_call(
        _final_norm_body,
        grid=(m // tm,),
        in_specs=[pl.BlockSpec((tm, d), lambda i: (i, 0)), pl.BlockSpec((1, d), lambda i: (0, 0))],
        out_specs=pl.BlockSpec((tm, d), lambda i: (i, 0)),
        out_shape=jax.ShapeDtypeStruct((m, d), F32),
        compiler_params=_cparams(("parallel",)),
        name="final_norm",
    )(x2, g)


def kernel(x, mem, start_frame, norm_mix, norm_xattn, norm_mem, norm_ffn, final_norm, xattn_wq, xattn_wkv, xattn_wo, ffn_up, ffn_down, even_w_in, even_w_out, dsa_cq_norm, dsa_ckv_norm, dsa_kidx_norm, dsa_w_uq, dsa_w_uk, dsa_w_uv, dsa_w_qidx, rwkv_mu, rwkv_w0, rwkv_w2, rwkv_a0, rwkv_a2, rwkv_g2, rwkv_k_k, rwkv_k_a, rwkv_r_k, rwkv_ln_w, rwkv_ln_b, odd_w_in, odd_w_out, s5_lam_re, s5_lam_im, s5_log_dt, s5_b_re, s5_b_im, s5_c_re, s5_c_im, s5_d, s5_w_glu, s5_b_glu):
    bsz, seq, d = x.shape
    depth = norm_mix.shape[0]
    m = bsz * seq
    x2 = x.reshape(m, d)
    mem2 = mem.reshape(bsz * mem.shape[1], d)
    for layer in range(depth):
        i = layer // 2
        g_mix = norm_mix[layer][None, :]
        if layer % 2 == 0:
            x2 = _even_mixer(x2, bsz, seq, start_frame, g_mix, even_w_in[i], even_w_out[i], dsa_cq_norm[i],
                             dsa_ckv_norm[i], dsa_kidx_norm[i], dsa_w_uq[i], dsa_w_uk[i], dsa_w_uv[i],
                             dsa_w_qidx[i], rwkv_mu[i], rwkv_w0[i], rwkv_w2[i], rwkv_a0[i], rwkv_a2[i],
                             rwkv_g2[i], rwkv_k_k[i], rwkv_k_a[i], rwkv_r_k[i], rwkv_ln_w[i], rwkv_ln_b[i])
        else:
            x2 = _odd_mixer(x2, bsz, seq, g_mix, odd_w_in[i], odd_w_out[i], s5_lam_re[i], s5_lam_im[i],
                            s5_log_dt[i], s5_b_re[i], s5_b_im[i], s5_c_re[i], s5_c_im[i], s5_d[i],
                            s5_w_glu[i], s5_b_glu[i])
        kv = _matmul([mem2], [norm_mem[layer][None, :]], xattn_wkv[layer].astype(BF16), [],
                     prologue=lambda a, g: _rms(a, g), epilogue=_epi_id, out_dtype=BF16, tm=512, tn=1024,
                     name="xattn_kv")
        x2 = _xattn(x2.reshape(bsz, seq, d), norm_xattn[layer][None, :], xattn_wq[layer].astype(BF16),
                    kv.reshape(bsz, mem.shape[1], -1), xattn_wo[layer].astype(BF16)).reshape(m, d)
        x2 = _ffn(x2, norm_ffn[layer][None, :], ffn_up[layer].astype(BF16), ffn_down[layer].astype(BF16))
    return _final_norm(x2, final_norm[None, :]).reshape(bsz, seq, d)
```

```python
import functools
import math

import numpy as np
import jax
import jax.numpy as jnp
from jax import lax
from jax.experimental import pallas as pl
from jax.experimental.pallas import tpu as pltpu

F32 = jnp.float32
BF16 = jnp.bfloat16
I32 = jnp.int32

D_MODEL = 2048
CHUNK = 64
ROPE_THETA = 500000.0
NORM_EPS = 1e-5

A_HEADS = 8
A_HEAD_DIM = 128
A_ROPE_DIM = 32
A_V_DIM = 128
A_WIDTH = A_HEADS * A_V_DIM
A_Q_RANK = 512
A_KV_RANK = 256
IDX_HEADS = 16
IDX_DIM = 64
IDX_ROPE_DIM = 16
TOPK_MAX = 256

B_HEAD = 64
B_WIDTH = 1024
B_HEADS = 16
B_DECAY_LORA = 64
B_A_LORA = 64
B_GATE_LORA = 160
B_LN_EPS = 64e-5

C_WIDTH = 2048
C_GROUP = 16
C_GROUPS = 128
C_STATE = 64

X_HEADS = 4
X_HEAD_DIM = 128
FFN_DIM = 4 * D_MODEL

LANES = 128
VMEM_LIMIT = 56 * 1024 * 1024

A_PAD_COLS = A_Q_RANK + A_KV_RANK + 3 * LANES
B_LORA_COLS = 2 * LANES + 2 * LANES
B_PAD_COLS = 3 * B_WIDTH + B_LORA_COLS
KCAT = A_KV_RANK + LANES

INT_MIN = -2147483648
NEG_BIG = -1e30
M_FLOOR = -1e20
SOFTMAX_C = (A_HEAD_DIM ** -0.5) * math.log2(math.e)
COUNT_ROWS = 32


def _cparams(sem):
    return pltpu.CompilerParams(dimension_semantics=sem, vmem_limit_bytes=VMEM_LIMIT)


def _rms(x, g):
    return x * lax.rsqrt(jnp.mean(x * x, axis=-1, keepdims=True) + NORM_EPS) * g


def _dot(a, b):
    return jnp.dot(a, b, preferred_element_type=F32)


def _dot_nt(a, b):
    return lax.dot_general(a, b, (((1,), (1,)), ((), ())), preferred_element_type=F32)


def _dot_tn(a, b):
    return lax.dot_general(a, b, (((0,), (0,)), ((), ())), preferred_element_type=F32)


def _split2(x):
    hi = x.astype(BF16)
    lo = (x - hi.astype(F32)).astype(BF16)
    return hi, lo


def _dotp(a, b, passes, kind="nn"):
    f = {"nn": _dot, "nt": _dot_nt, "tn": _dot_tn}[kind]
    if passes == 1:
        return f(a.astype(BF16), b.astype(BF16))
    ah, al = _split2(a)
    bh, bl = _split2(b)
    return f(ah, bh) + (f(ah, bl) + f(al, bh))


def _mm_body(*refs, n_row, n_const, epi_kinds, single_n, prologue, epilogue):
    rows = refs[:n_row]
    consts = refs[n_row:n_row + n_const]
    w_ref = refs[n_row + n_const]
    epi_refs = iter(refs[n_row + n_const + 1:])
    epis = [rows[kind] if isinstance(kind, int) else next(epi_refs) for kind in epi_kinds]
    if single_n:
        o_ref = refs[-1]
        xs = prologue(*[r[...] for r in rows], *[c[...] for c in consts]).astype(BF16)
    else:
        o_ref, xs_ref = refs[-2], refs[-1]

        @pl.when(pl.program_id(1) == 0)
        def _():
            xs_ref[...] = prologue(*[r[...] for r in rows], *[c[...] for c in consts]).astype(BF16)

        xs = xs_ref[...]
    acc = _dot(xs, w_ref[...])
    o_ref[...] = epilogue(acc, *[e[...] for e in epis]).astype(o_ref.dtype)


def _matmul(rows, consts, w, epis, *, prologue, epilogue, out_dtype, tm, tn, name):
    m = rows[0].shape[0]
    k, n = w.shape
    tm = min(tm, m)
    tn = min(tn, n)
    assert m % tm == 0 and n % tn == 0, (m, tm, n, tn)
    single_n = tn == n
    in_specs = [pl.BlockSpec((tm, r.shape[1]), lambda i, j: (i, 0)) for r in rows]
    in_specs += [pl.BlockSpec(c.shape, lambda i, j: (0, 0)) for c in consts]
    in_specs += [pl.BlockSpec((k, tn), lambda i, j: (0, j))]
    for _, kind in epis:
        if kind == "tile":
            in_specs.append(pl.BlockSpec((tm, tn), lambda i, j: (i, j)))
        elif kind == "row":
            in_specs.append(pl.BlockSpec((1, tn), lambda i, j: (0, j)))
        else:
            assert single_n and rows[kind].shape[1] == n
    body = functools.partial(_mm_body, n_row=len(rows), n_const=len(consts),
                             epi_kinds=tuple(kind for _, kind in epis), single_n=single_n,
                             prologue=prologue, epilogue=epilogue)
    return pl.pallas_call(
        body,
        grid=(m // tm, n // tn),
        in_specs=in_specs,
        out_specs=pl.BlockSpec((tm, tn), lambda i, j: (i, j)),
        out_shape=jax.ShapeDtypeStruct((m, n), out_dtype),
        scratch_shapes=[] if single_n else [pltpu.VMEM((tm, k), BF16)],
        compiler_params=_cparams(("parallel", "arbitrary")),
        name=name,
    )(*rows, *consts, w, *[e for e, kind in epis if not isinstance(kind, int)])


def _epi_id(acc):
    return acc


def _epi_add(acc, res):
    return acc + res


def _pro_id(x):
    return x


def _ffn_body(x_ref, g_ref, up_ref, dn_ref, o_ref, xs_ref):
    @pl.when(pl.program_id(1) == 0)
    def _():
        x = x_ref[...]
        xs_ref[...] = _rms(x, g_ref[...]).astype(BF16)
        o_ref[...] = x

    h = _dot(xs_ref[...], up_ref[...])
    h = jnp.square(jnp.maximum(h, 0.0)).astype(BF16)
    o_ref[...] += _dot(h, dn_ref[...])


def _ffn(x, g, w_up, w_dn, *, tm=1024, tf=512):
    m, d = x.shape
    f = w_up.shape[1]
    tm = min(tm, m)
    return pl.pallas_call(
        _ffn_body,
        grid=(m // tm, f // tf),
        in_specs=[pl.BlockSpec((tm, d), lambda i, j: (i, 0)),
                  pl.BlockSpec((1, d), lambda i, j: (0, 0)),
                  pl.BlockSpec((d, tf), lambda i, j: (0, j)),
                  pl.BlockSpec((tf, d), lambda i, j: (j, 0))],
        out_specs=pl.BlockSpec((tm, d), lambda i, j: (i, 0)),
        out_shape=jax.ShapeDtypeStruct((m, d), F32),
        scratch_shapes=[pltpu.VMEM((tm, d), BF16)],
        compiler_params=_cparams(("parallel", "arbitrary")),
        name="ffn",
    )(x, g, w_up, w_dn)


def _xattn_body(x_ref, g_ref, wq_ref, kv_ref, wo_ref, o_ref):
    x = x_ref[0]
    xn = _rms(x, g_ref[...]).astype(BF16)
    q = _dot(xn, wq_ref[...])
    width = X_HEADS * X_HEAD_DIM
    outs = []
    for h in range(X_HEADS):
        sl = slice(h * X_HEAD_DIM, (h + 1) * X_HEAD_DIM)
        qh = q[:, sl].astype(BF16)
        kh = kv_ref[0, :, sl]
        vh = kv_ref[0, :, width + h * X_HEAD_DIM:width + (h + 1) * X_HEAD_DIM]
        s = _dot_nt(qh, kh) * (X_HEAD_DIM ** -0.5)
        s = s - jnp.max(s, axis=-1, keepdims=True)
        p = jnp.exp(s)
        p = p / jnp.sum(p, axis=-1, keepdims=True)
        outs.append(_dot(p.astype(BF16), vh))
    o = jnp.concatenate(outs, axis=-1).astype(BF16)
    o_ref[0] = x + _dot(o, wo_ref[...])


def _xattn(x3, g, wq, kv, wo, *, tm=512):
    b, s, d = x3.shape
    tm = min(tm, s)
    mlen = kv.shape[1]
    width = X_HEADS * X_HEAD_DIM
    return pl.pallas_call(
        _xattn_body,
        grid=(b, s // tm),
        in_specs=[pl.BlockSpec((1, tm, d), lambda bi, i: (bi, i, 0)),
                  pl.BlockSpec((1, d), lambda bi, i: (0, 0)),
                  pl.BlockSpec((d, width), lambda bi, i: (0, 0)),
                  pl.BlockSpec((1, mlen, 2 * width), lambda bi, i: (bi, 0, 0)),
                  pl.BlockSpec((width, d), lambda bi, i: (0, 0))],
        out_specs=pl.BlockSpec((1, tm, d), lambda bi, i: (bi, i, 0)),
        out_shape=jax.ShapeDtypeStruct((b, s, d), F32),
        compiler_params=_cparams(("parallel", "parallel")),
        name="xattn",
    )(x3, g, wq, kv, wo)


def _rope_tables():
    inv_a = ROPE_THETA ** (-jnp.arange(0, A_ROPE_DIM, 2, dtype=F32) / A_ROPE_DIM)
    inv_i = ROPE_THETA ** (-jnp.arange(0, IDX_ROPE_DIM, 2, dtype=F32) / IDX_ROPE_DIM)
    ha, hi = A_ROPE_DIM // 2, IDX_ROPE_DIM // 2

    def row(*pieces):
        r = jnp.concatenate([jnp.asarray(p, F32) for p in pieces])
        return jnp.pad(r, (0, LANES - r.shape[0]))

    ones, zeros = (lambda n: jnp.ones((n,), F32)), (lambda n: jnp.zeros((n,), F32))
    return jnp.stack([row(inv_a, inv_a), row(ones(2 * ha)), row(-ones(ha)), row(zeros(ha), ones(ha)),
                      row(inv_i, inv_i), row(-ones(hi)), row(zeros(hi), ones(hi)), zeros(LANES)])


def _dsa_proj_body(sf_ref, h_ref, tab_ref, nq_ref, nkv_ref, nki_ref, wuq_ref, wuk_ref, wqi_ref,
                   qcat_ref, kcat_ref, qidx_ref, kidx_ref, widx_ref):
    bi, i = pl.program_id(0), pl.program_id(1)
    tm = h_ref.shape[1]
    ha, hi = A_ROPE_DIM // 2, IDX_ROPE_DIM // 2
    pos = (sf_ref[bi] + i * tm + lax.broadcasted_iota(I32, (tm, LANES), 0)).astype(F32)
    tab = tab_ref[...]
    ang_a = pos * tab[0:1]
    cos_a = jnp.cos(ang_a) * tab[1:2]
    sin_a = jnp.sin(ang_a)
    sin_a1, sin_a2 = sin_a * tab[2:3], sin_a * tab[3:4]
    ang_i = pos * tab[4:5]
    cos_i = jnp.cos(ang_i)
    sin_i = jnp.sin(ang_i)
    sin_i1, sin_i2 = sin_i * tab[5:6], sin_i * tab[6:7]

    def rope_a(xb):
        return xb * cos_a + pltpu.roll(xb, LANES - ha, 1) * sin_a1 + pltpu.roll(xb, ha, 1) * sin_a2

    def rope_i(xb):
        return xb * cos_i + pltpu.roll(xb, LANES - hi, 1) * sin_i1 + pltpu.roll(xb, hi, 1) * sin_i2

    c_q = h_ref[0, :, 0:A_Q_RANK]
    c_kv = h_ref[0, :, A_Q_RANK:A_Q_RANK + A_KV_RANK]
    off = A_Q_RANK + A_KV_RANK
    k_rope = h_ref[0, :, off:off + LANES]
    k_idx = h_ref[0, :, off + LANES:off + 2 * LANES]
    w_idx = h_ref[0, :, off + 2 * LANES:off + 3 * LANES]

    cqn = _rms(c_q, nq_ref[...]).astype(BF16)
    kcat_ref[0, :, 0:A_KV_RANK] = _rms(c_kv, nkv_ref[...]).astype(BF16)
    kcat_ref[0, :, A_KV_RANK:KCAT] = rope_a(k_rope).astype(BF16)

    ki = k_idx * lax.rsqrt(jnp.sum(k_idx * k_idx, axis=-1, keepdims=True) * (1.0 / IDX_DIM) + NORM_EPS)
    kidx_ref[0] = rope_i(ki * nki_ref[...]).astype(BF16)
    widx_ref[0] = w_idx * ((IDX_HEADS ** -0.5) * (IDX_DIM ** -0.5))

    q = _dot(cqn, wuq_ref[...])
    for h in range(A_HEADS):
        qh = q[:, h * A_HEAD_DIM:(h + 1) * A_HEAD_DIM]
        qcat_ref[0, h, :, 0:A_KV_RANK] = (_dot(qh.astype(BF16), wuk_ref[h]) * SOFTMAX_C).astype(BF16)
        qcat_ref[0, h, :, A_KV_RANK:KCAT] = (rope_a(qh) * SOFTMAX_C).astype(BF16)
    qi = _dot(cqn, wqi_ref[...])
    for h in range(IDX_HEADS):
        qidx_ref[0, h] = rope_i(qi[:, h * LANES:(h + 1) * LANES]).astype(BF16)


def _dsa_proj(h_a, start_frame, tab, nq, nkv, nki, wuq, wuk_t, wqi, *, tm=256):
    b, s, _ = h_a.shape
    tm = min(tm, s)
    c3 = lambda bi, i, sf: (bi, i, 0)
    c4 = lambda bi, i, sf: (bi, 0, i, 0)
    z2 = lambda bi, i, sf: (0, 0)
    z3 = lambda bi, i, sf: (0, 0, 0)
    grid_spec = pltpu.PrefetchScalarGridSpec(
        num_scalar_prefetch=1,
        grid=(b, s // tm),
        in_specs=[pl.BlockSpec((1, tm, A_PAD_COLS), c3),
                  pl.BlockSpec(tab.shape, z2),
                  pl.BlockSpec(nq.shape, z2),
                  pl.BlockSpec(nkv.shape, z2),
                  pl.BlockSpec(nki.shape, z2),
                  pl.BlockSpec(wuq.shape, z2),
                  pl.BlockSpec(wuk_t.shape, z3),
                  pl.BlockSpec(wqi.shape, z2)],
        out_specs=[pl.BlockSpec((1, A_HEADS, tm, KCAT), c4),
                   pl.BlockSpec((1, tm, KCAT), c3),
                   pl.BlockSpec((1, IDX_HEADS, tm, LANES), c4),
                   pl.BlockSpec((1, tm, LANES), c3),
                   pl.BlockSpec((1, tm, LANES), c3)],
    )
    return pl.pallas_call(
        _dsa_proj_body,
        grid_spec=grid_spec,
        out_shape=[jax.ShapeDtypeStruct((b, A_HEADS, s, KCAT), BF16),
                   jax.ShapeDtypeStruct((b, s, KCAT), BF16),
                   jax.ShapeDtypeStruct((b, IDX_HEADS, s, LANES), BF16),
                   jax.ShapeDtypeStruct((b, s, LANES), BF16),
                   jax.ShapeDtypeStruct((b, s, LANES), F32)],
        compiler_params=_cparams(("parallel", "parallel")),
        name="dsa_proj",
    )(start_frame, h_a, tab, nq, nkv, nki, wuq, wuk_t, wqi)


def _dsa_attend_body(qidx_ref, widx_ref, kidx_ref, qcat_ref, kcat_ref, wuv_ref, o_ref,
                     key_ref, m_ref, l_ref, a_ref, acc_ref, t_ref, p_ref, *, tq, kt, topk):
    i = pl.program_id(1)
    t0 = i * tq
    nk = (t0 + tq + kt - 1) // kt
    krow = lax.broadcasted_iota(I32, (kt, tq), 0)
    qcol = lax.broadcasted_iota(I32, (kt, tq), 1)
    limit = ((t0 + qcol) // CHUNK + 1) * CHUNK

    wt = widx_ref[0]

    def score_tile(j, carry):
        off = pl.multiple_of(j * kt, kt)
        kid = kidx_ref[0, pl.ds(off, kt), :]
        sc = jnp.zeros((kt, tq), F32)
        for h in range(IDX_HEADS):
            sc = sc + wt[h:h + 1, :] * jnp.maximum(_dot_nt(kid, qidx_ref[0, h]), 0.0)
        bits = pltpu.bitcast(sc, I32)
        key = jnp.where(bits >= 0, bits, bits ^ 0x7FFFFFFF)
        key_ref[j] = jnp.where(off + krow < limit, key, INT_MIN)
        return carry

    lax.fori_loop(0, nk, score_tile, 0)

    def count_ge(cand):
        def body(j, acc):
            hit = jnp.where(key_ref[j] >= cand, 1.0, 0.0)
            return acc + jnp.sum(hit.reshape(kt // COUNT_ROWS, COUNT_ROWS, tq), axis=0)
        acc = lax.fori_loop(0, nk, body, jnp.zeros((COUNT_ROWS, tq), F32))
        return jnp.sum(acc, axis=0, keepdims=True)

    kf = float(topk)
    thr = jnp.where(count_ge(jnp.zeros((1, tq), I32)) >= kf, 0, INT_MIN).astype(I32)

    def bit_step(it, thr):
        cand = thr | jnp.left_shift(jnp.int32(1), 30 - it)
        return jnp.where(count_ge(cand) >= kf, cand, thr)

    thr = lax.fori_loop(0, 31, bit_step, thr)
    thr = jnp.maximum(thr, INT_MIN + 1)

    m_ref[...] = jnp.full(m_ref.shape, M_FLOOR, F32)
    l_ref[...] = jnp.zeros(l_ref.shape, F32)
    acc_ref[...] = jnp.zeros(acc_ref.shape, F32)
    eye_q = jnp.where(lax.broadcasted_iota(I32, (tq, tq), 0) == lax.broadcasted_iota(I32, (tq, tq), 1),
                      1.0, 0.0).astype(BF16)
    lane_tiles = [slice(c * LANES, (c + 1) * LANES) for c in range(kt // LANES)]
    acc_tiles = [slice(c * LANES, (c + 1) * LANES) for c in range(A_KV_RANK // LANES)]

    def attend_tile(j, carry):
        off = pl.multiple_of(j * kt, kt)
        kc = kcat_ref[0, pl.ds(off, kt), :]
        ckv = kc[:, 0:A_KV_RANK]
        bias_t = jnp.where(key_ref[j] >= thr, 0.0, NEG_BIG).astype(BF16)
        bias = _dot_nt(eye_q, bias_t)
        for h in range(A_HEADS):
            t_ref[h] = _dot_nt(qcat_ref[0, h], kc) + bias
        for h in range(A_HEADS):
            ts = [t_ref[h, :, lt] for lt in lane_tiles]
            m_old = m_ref[h]
            tmax = functools.reduce(jnp.maximum, ts)
            m_new = jnp.maximum(m_old, jnp.max(tmax, axis=-1, keepdims=True))
            ps = [jnp.exp2(t - m_new) for t in ts]
            alpha = jnp.exp2(m_old - m_new)
            l_ref[h] = alpha * l_ref[h] + jnp.sum(functools.reduce(jnp.add, ps), axis=-1, keepdims=True)
            for lt, p in zip(lane_tiles, ps):
                p_ref[h, :, lt] = p.astype(BF16)
            a_ref[h] = alpha
            m_ref[h] = m_new
        for h in range(A_HEADS):
            pv = _dot(p_ref[h], ckv)
            for at in acc_tiles:
                acc_ref[h, :, at] = a_ref[h] * acc_ref[h, :, at] + pv[:, at]
        return carry

    lax.fori_loop(0, nk, attend_tile, 0)

    for h in range(A_HEADS):
        inv_l = 1.0 / l_ref[h]
        o_lat = jnp.concatenate([acc_ref[h, :, at] * inv_l for at in acc_tiles], axis=1)
        o_ref[0, :, h * A_V_DIM:(h + 1) * A_V_DIM] = _dot(o_lat.astype(BF16), wuv_ref[h]).astype(o_ref.dtype)


def _dsa_attend(qidx, widx, kidx, qcat, kcat, wuv, *, tq=256, kt=512):
    b, _, s, _ = qidx.shape
    kt = min(kt, s)
    tq = min(tq, s)
    topk = min(TOPK_MAX, s // 4)
    assert s % kt == 0 and s % tq == 0 and tq % CHUNK == 0
    body = functools.partial(_dsa_attend_body, tq=tq, kt=kt, topk=topk)
    widx_t = jnp.transpose(widx[:, :, 0:IDX_HEADS], (0, 2, 1))
    return pl.pallas_call(
        body,
        grid=(b, s // tq),
        in_specs=[pl.BlockSpec((1, IDX_HEADS, tq, LANES), lambda bi, i: (bi, 0, i, 0)),
                  pl.BlockSpec((1, IDX_HEADS, tq), lambda bi, i: (bi, 0, i)),
                  pl.BlockSpec((1, s, LANES), lambda bi, i: (bi, 0, 0)),
                  pl.BlockSpec((1, A_HEADS, tq, KCAT), lambda bi, i: (bi, 0, i, 0)),
                  pl.BlockSpec((1, s, KCAT), lambda bi, i: (bi, 0, 0)),
                  pl.BlockSpec(wuv.shape, lambda bi, i: (0, 0, 0))],
        out_specs=pl.BlockSpec((1, tq, A_WIDTH), lambda bi, i: (bi, i, 0)),
        out_shape=jax.ShapeDtypeStruct((b, s, A_WIDTH), BF16),
        scratch_shapes=[pltpu.VMEM((s // kt, kt, tq), I32),
                        pltpu.VMEM((A_HEADS, tq, LANES), F32),
                        pltpu.VMEM((A_HEADS, tq, LANES), F32),
                        pltpu.VMEM((A_HEADS, tq, LANES), F32),
                        pltpu.VMEM((A_HEADS, tq, A_KV_RANK), F32),
                        pltpu.VMEM((A_HEADS, tq, kt), F32),
                        pltpu.VMEM((A_HEADS, tq, kt), BF16)],
        compiler_params=_cparams(("parallel", "arbitrary")),
        name="dsa_attend",
    )(qidx, widx_t, kidx, qcat, kcat, wuv)


def _rwkv_prep_body(h_ref, prev_ref, mu_ref, w0_ref, a0_ref, w2_ref, a2_ref, g2_ref,
                    rkv_ref, ld_ref, a_ref, g_ref):
    i = pl.program_id(1)
    h = h_ref[0]
    tm = h.shape[0]
    prev = jnp.where(i > 0, prev_ref[0, 7:8, :], 0.0)
    row = lax.broadcasted_iota(I32, h.shape, 0)
    shifted = jnp.where(row == 0, prev, pltpu.roll(h, 1, 0))
    hm = h + (shifted - h) * mu_ref[...]
    w3 = 3 * B_WIDTH
    rkv_ref[0] = hm[:, 0:w3]
    wl = hm[:, w3:w3 + LANES]
    al = hm[:, w3 + LANES:w3 + 2 * LANES]
    gl = hm[:, w3 + 2 * LANES:w3 + 4 * LANES]
    wz = w0_ref[...] + _dot(jnp.tanh(wl).astype(BF16), w2_ref[...])
    log_w = -jax.nn.softplus(-wz) - 0.5
    ld_ref[0] = -jnp.exp(log_w)
    a_ref[0] = jax.nn.sigmoid(a0_ref[...] + _dot(al.astype(BF16), a2_ref[...]))
    g_ref[0] = _dot(jax.nn.sigmoid(gl).astype(BF16), g2_ref[...])


def _rwkv_prep(h_b, mu, w0, a0, w2, a2, g2, *, tm=256):
    b, s, cols = h_b.shape
    tm = min(tm, s)
    c3 = lambda bi, i: (bi, i, 0)
    z2 = lambda bi, i: (0, 0)
    prev_map = lambda bi, i: (bi, jnp.maximum(i * (tm // 8) - 1, 0), 0)
    out = lambda w: jax.ShapeDtypeStruct((b, s, w), F32)
    return pl.pallas_call(
        _rwkv_prep_body,
        grid=(b, s // tm),
        in_specs=[pl.BlockSpec((1, tm, cols), c3),
                  pl.BlockSpec((1, 8, cols), prev_map),
                  pl.BlockSpec(mu.shape, z2), pl.BlockSpec(w0.shape, z2), pl.BlockSpec(a0.shape, z2),
                  pl.BlockSpec(w2.shape, z2), pl.BlockSpec(a2.shape, z2), pl.BlockSpec(g2.shape, z2)],
        out_specs=[pl.BlockSpec((1, tm, 3 * B_WIDTH), c3), pl.BlockSpec((1, tm, B_WIDTH), c3),
                   pl.BlockSpec((1, tm, B_WIDTH), c3), pl.BlockSpec((1, tm, B_WIDTH), c3)],
        out_shape=[out(3 * B_WIDTH), out(B_WIDTH), out(B_WIDTH), out(B_WIDTH)],
        compiler_params=_cparams(("parallel", "parallel")),
        name="rwkv_prep",
    )(h_b, h_b, mu, w0, a0, w2, a2, g2)


RWKV_CHUNK = 64
RWKV_SUB = 16


def _bmm(a, b, kind="nn"):
    spec = {"nn": "bij,bjk->bik", "nt": "bik,bjk->bij"}[kind]
    return jnp.einsum(spec, a.astype(BF16), b.astype(BF16), preferred_element_type=F32)


def _unit_lower_inverse(l_mat, eye, same_blk):
    mm = _bmm
    ld = jnp.where(same_blk, l_mat, 0.0)
    lo = l_mat - ld
    x = eye + ld
    p = mm(ld, ld)
    x = x + mm(x, p)
    p = mm(p, p)
    x = x + mm(x, p)
    p = mm(p, p)
    d = x + mm(x, p)
    n = mm(d, lo)
    y = eye + n
    y = y + mm(y, mm(n, n))
    return mm(y, d)


def _rwkv_scan_body(r_ref, k_ref, v_ref, ld_ref, a_ref, g_ref, par_ref, o_ref, st_ref):
    c = RWKV_CHUNK
    tblk = r_ref.shape[1]
    nc = tblk // c
    nh = r_ref.shape[2] // B_HEAD

    @pl.when(pl.program_id(2) == 0)
    def _():
        st_ref[...] = jnp.zeros(st_ref.shape, F32)

    def heads(x):
        return jnp.concatenate([x[:, hd * B_HEAD:(hd + 1) * B_HEAD].reshape(nc, c, B_HEAD) for hd in range(nh)],
                               axis=0)

    def head_row(i):
        p = par_ref[i:i + 1, :]
        return jnp.concatenate([jnp.broadcast_to(p[:, hd * B_HEAD:(hd + 1) * B_HEAD][None], (nc, 1, B_HEAD))
                                for hd in range(nh)], axis=0)

    ri = lax.broadcasted_iota(I32, (c, c), 0)
    ci = lax.broadcasted_iota(I32, (c, c), 1)
    eye = jnp.where(ri == ci, 1.0, 0.0).astype(F32)[None]
    lower = (ri >= ci)[None]
    strict = (ri > ci)[None]
    same_blk = ((ri // RWKV_SUB) == (ci // RWKV_SUB))[None]

    rb = lax.broadcasted_iota(I32, (tblk, tblk), 0)
    cb = lax.broadcasted_iota(I32, (tblk, tblk), 1)
    tri = jnp.where((rb >= cb) & (rb // c == cb // c), 1.0, 0.0).astype(BF16)
    ld2 = ld_ref[0]
    l1 = ld2.astype(BF16)
    rem = ld2 - l1.astype(F32)
    l2 = rem.astype(BF16)
    l3 = (rem - l2.astype(F32)).astype(BF16)
    cs = heads(_dot(tri, l1) + (_dot(tri, l2) + _dot(tri, l3)))

    r, k, v = heads(r_ref[0]), heads(k_ref[0]), heads(v_ref[0])
    ld, a, g = heads(ld2), heads(a_ref[0]), heads(g_ref[0])
    kk = k * head_row(0)
    kk = kk * lax.rsqrt(jnp.maximum(jnp.sum(kk * kk, axis=-1, keepdims=True), 1e-24))
    km = k * (1.0 + (a - 1.0) * head_row(1))
    bv = kk * a
    cs_end = cs[:, c - 1:c, :]
    e_neg = jnp.exp(-cs)
    e_end = jnp.exp(cs_end - cs)
    at = -kk * jnp.exp(cs - ld)
    rt = r * jnp.exp(cs)
    bt, kt = bv * e_neg, km * e_neg
    bd, kd = bv * e_end, km * e_end
    p_end_t = jnp.swapaxes(jnp.broadcast_to(jnp.exp(cs_end), cs.shape), 1, 2)

    lab = jnp.where(strict, _bmm(at, bt, "nt"), 0.0)
    lak = jnp.where(strict, _bmm(at, kt, "nt"), 0.0)
    lrb = jnp.where(lower, _bmm(rt, bt, "nt"), 0.0)
    lrk = jnp.where(lower, _bmm(rt, kt, "nt"), 0.0)
    tinv = _unit_lower_inverse(lab, eye, same_blk)
    tat = _bmm(tinv, at)
    c1 = _bmm(tinv, _bmm(lak, v))
    r2 = (rt + _bmm(lrb, tat)).astype(BF16)
    c2 = _bmm(lrk, v) + _bmm(lrb, c1)
    bd_t, kd_t = jnp.swapaxes(bd, 1, 2), jnp.swapaxes(kd, 1, 2)
    gmat_t = _bmm(bd_t, tat).astype(BF16)
    c3_t = _bmm(bd_t, c1) + _bmm(kd_t, v)

    outs = [None] * (nh * nc)
    sts = [st_ref[hd] for hd in range(nh)]
    for ch in range(nc):
        for hd in range(nh):
            i = hd * nc + ch
            sb = sts[hd].astype(BF16)
            outs[i] = _dot(r2[i], sb) + c2[i]
            sts[hd] = sts[hd] * p_end_t[i] + (_dot(gmat_t[i], sb) + c3_t[i])
    for hd in range(nh):
        st_ref[hd] = sts[hd]
    out = jnp.stack(outs, axis=0)

    mean = jnp.mean(out, axis=-1, keepdims=True)
    var = jnp.mean(jnp.square(out - mean), axis=-1, keepdims=True)
    y = (out - mean) * lax.rsqrt(var + B_LN_EPS) * head_row(3) + head_row(4)
    y = (y + jnp.sum(r * km * head_row(2), axis=-1, keepdims=True) * v) * g
    for hd in range(nh):
        o_ref[0, :, hd * B_HEAD:(hd + 1) * B_HEAD] = y[hd * nc:(hd + 1) * nc].reshape(tblk, B_HEAD).astype(o_ref.dtype)


def _rwkv_scan(rkv, ld, a, g, par, *, tblk=512, hw=2 * LANES):
    b, s, _ = ld.shape
    tblk = min(tblk, s)
    npair = B_WIDTH // hw
    blk = (1, tblk, hw)
    return pl.pallas_call(
        _rwkv_scan_body,
        grid=(b, npair, s // tblk),
        in_specs=[pl.BlockSpec(blk, lambda bi, hp, t: (bi, t, hp)),
                  pl.BlockSpec(blk, lambda bi, hp, t: (bi, t, npair + hp)),
                  pl.BlockSpec(blk, lambda bi, hp, t: (bi, t, 2 * npair + hp)),
                  pl.BlockSpec(blk, lambda bi, hp, t: (bi, t, hp)),
                  pl.BlockSpec(blk, lambda bi, hp, t: (bi, t, hp)),
                  pl.BlockSpec(blk, lambda bi, hp, t: (bi, t, hp)),
                  pl.BlockSpec((8, hw), lambda bi, hp, t: (0, hp))],
        out_specs=pl.BlockSpec(blk, lambda bi, hp, t: (bi, t, hp)),
        out_shape=jax.ShapeDtypeStruct((b, s, B_WIDTH), BF16),
        scratch_shapes=[pltpu.VMEM((hw // B_HEAD, B_HEAD, B_HEAD), F32)],
        compiler_params=_cparams(("parallel", "parallel", "arbitrary")),
        name="rwkv_scan",
    )(rkv, rkv, rkv, ld, a, g, par)


S5_CHUNK = LANES
S5_ROWS = S5_CHUNK * C_GROUP
S5_PASSES = 3


def _s5_build_body(lam_c_ref, lam_r_ref, bt_ref, c_ref, ct_ref, m_ref, w_ref, v_ref, lc_ref, kv_ref):
    L = S5_CHUNK

    def zoh(lre, lim, dt):
        lre = jnp.minimum(lre, -1e-4)
        rho, th = lre * dt, lim * dt
        mag = jnp.exp(rho)
        lbr, lbi = mag * jnp.cos(th), mag * jnp.sin(th)
        den = 1.0 / (lre * lre + lim * lim)
        cr = ((lbr - 1.0) * lre + lbi * lim) * den
        cim = (lbi * lre - (lbr - 1.0) * lim) * den
        return rho, th, cr, cim

    lc = lam_c_ref[0]
    rho_c, th_c, _, _ = zoh(lc[:, 0:1], lc[:, 1:2], jnp.exp(lc[:, 2:3]))
    lr = lam_r_ref[0]
    rho_r, th_r, cr_r, ci_r = zoh(lr[0:1], lr[1:2], jnp.exp(lr[2:3]))

    def powers(rho, th, n):
        mag = jnp.exp(rho * n)
        return mag * jnp.cos(th * n), mag * jnp.sin(th * n)

    btr, bti = bt_ref[0, 0], bt_ref[0, 1]
    bbr = btr * cr_r - bti * ci_r
    bbi = btr * ci_r + bti * cr_r
    c_re, c_im = c_ref[0, 0], c_ref[0, 1]

    d_re = jnp.concatenate([c_re * bbr[j:j + 1] - c_im * bbi[j:j + 1] for j in range(C_GROUP)], axis=0)
    d_im = jnp.concatenate([c_re * bbi[j:j + 1] + c_im * bbr[j:j + 1] for j in range(C_GROUP)], axis=0)
    tau = lax.broadcasted_iota(I32, (C_STATE, L), 1).astype(F32)
    e_re, e_im = powers(rho_c, th_c, tau)
    kvec = _dotp(d_re, e_re, S5_PASSES) - _dotp(d_im, e_im, S5_PASSES)

    kv_ref[...] = kvec
    upper = lax.broadcasted_iota(I32, (L, L), 1) >= lax.broadcasted_iota(I32, (L, L), 0)

    def fill_block_row(j, carry):
        r0 = pl.multiple_of(j * L, L)
        for i in range(C_GROUP):
            rows = jnp.broadcast_to(kv_ref[pl.ds(j * C_GROUP + i, 1), :], (L, L))
            blk = jnp.where(upper, pltpu.roll(rows, 0, 1, stride=1, stride_axis=0), 0.0)
            m_ref[0, pl.ds(r0, L), i * L:(i + 1) * L] = blk.astype(m_ref.dtype)
        return carry

    lax.fori_loop(0, C_GROUP, fill_block_row, 0)

    back = (L - 1 - lax.broadcasted_iota(I32, (L, C_STATE), 0)).astype(F32)
    f_re, f_im = powers(rho_r, th_r, back)
    for j in range(C_GROUP):
        br, bi = bbr[j:j + 1], bbi[j:j + 1]
        w_ref[0, j * L:(j + 1) * L, :] = jnp.concatenate(
            [f_re * br - f_im * bi, f_re * bi + f_im * br], axis=1).astype(w_ref.dtype)

    e1_re, e1_im = powers(rho_c, th_c, tau + 1.0)
    ctr, cti = ct_ref[0, 0], ct_ref[0, 1]
    for i in range(C_GROUP):
        cr_i, ci_i = ctr[:, i:i + 1], cti[:, i:i + 1]
        v_ref[0, 0:C_STATE, i * L:(i + 1) * L] = (cr_i * e1_re - ci_i * e1_im).astype(v_ref.dtype)
        v_ref[0, C_STATE:2 * C_STATE, i * L:(i + 1) * L] = (-(cr_i * e1_im + ci_i * e1_re)).astype(v_ref.dtype)

    pr, pi = powers(rho_r, th_r, float(L))
    lc_ref[0] = jnp.concatenate([jnp.concatenate([pr, pr], axis=1),
                                 jnp.concatenate([-pi, pi], axis=1),
                                 jnp.zeros((6, 2 * C_STATE), F32)], axis=0)


def _s5_build(lam_c, lam_r, bt, c, ct):
    g = lam_c.shape[0]
    i3 = lambda gi: (gi, 0, 0)
    i4 = lambda gi: (gi, 0, 0, 0)
    return pl.pallas_call(
        _s5_build_body,
        grid=(g,),
        in_specs=[pl.BlockSpec((1, C_STATE, 8), i3),
                  pl.BlockSpec((1, 8, C_STATE), i3),
                  pl.BlockSpec((1, 2, C_GROUP, C_STATE), i4),
                  pl.BlockSpec((1, 2, C_GROUP, C_STATE), i4),
                  pl.BlockSpec((1, 2, C_STATE, C_GROUP), i4)],
        out_specs=[pl.BlockSpec((1, S5_ROWS, S5_ROWS), i3),
                   pl.BlockSpec((1, S5_ROWS, 2 * C_STATE), i3),
                   pl.BlockSpec((1, 2 * C_STATE, S5_ROWS), i3),
                   pl.BlockSpec((1, 8, 2 * C_STATE), i3)],
        out_shape=[jax.ShapeDtypeStruct((g, S5_ROWS, S5_ROWS), BF16),
                   jax.ShapeDtypeStruct((g, S5_ROWS, 2 * C_STATE), BF16),
                   jax.ShapeDtypeStruct((g, 2 * C_STATE, S5_ROWS), BF16),
                   jax.ShapeDtypeStruct((g, 8, 2 * C_STATE), F32)],
        scratch_shapes=[pltpu.VMEM((C_GROUP * C_GROUP, S5_CHUNK), F32)],
        compiler_params=_cparams(("parallel",)),
        name="s5_build",
    )(lam_c, lam_r, bt, c, ct)


def _s5_apply_body(u_ref, m_ref, w_ref, v_ref, lc_ref, y_ref, z_ref, xs_ref):
    n_chunk, bsz = u_ref.shape[0], u_ref.shape[1]
    rows = n_chunk * bsz
    u = jnp.concatenate([u_ref[:, :, j, :].reshape(rows, S5_CHUNK) for j in range(C_GROUP)], axis=1)
    z_ref[...] = _dot(u, w_ref[0])
    lr, li = lc_ref[0, 0:1, :], lc_ref[0, 1:2, :]

    def step(ci, x):
        r0 = pl.multiple_of(ci * bsz, bsz)
        xs_ref[pl.ds(r0, bsz), :] = x
        return x * lr + pltpu.roll(x, C_STATE, 1) * li + z_ref[pl.ds(r0, bsz), :]

    lax.fori_loop(0, n_chunk, step, jnp.zeros((bsz, 2 * C_STATE), F32))
    y = _dot(u, m_ref[0]) + _dot(xs_ref[...].astype(BF16), v_ref[0])
    for i in range(C_GROUP):
        y_ref[:, :, i, :] = y[:, i * S5_CHUNK:(i + 1) * S5_CHUNK].reshape(n_chunk, bsz, S5_CHUNK)


def _s5_apply(u4, m, w, v, lc):
    n_chunk, bsz, width, _ = u4.shape
    g = width // C_GROUP
    rows = n_chunk * bsz
    i3 = lambda gi: (gi, 0, 0)
    blk = pl.BlockSpec((n_chunk, bsz, C_GROUP, S5_CHUNK), lambda gi: (0, 0, gi, 0))
    return pl.pallas_call(
        _s5_apply_body,
        grid=(g,),
        in_specs=[blk,
                  pl.BlockSpec((1, S5_ROWS, S5_ROWS), i3),
                  pl.BlockSpec((1, S5_ROWS, 2 * C_STATE), i3),
                  pl.BlockSpec((1, 2 * C_STATE, S5_ROWS), i3),
                  pl.BlockSpec((1, 8, 2 * C_STATE), i3)],
        out_specs=blk,
        out_shape=jax.ShapeDtypeStruct(u4.shape, F32),
        scratch_shapes=[pltpu.VMEM((rows, 2 * C_STATE), F32), pltpu.VMEM((rows, 2 * C_STATE), F32)],
        compiler_params=_cparams(("parallel",)),
        name="s5_apply",
    )(u4, m, w, v, lc)


def _s5_in_body(x_ref, g_ref, w_ref, u_ref, u4_ref):
    xs = _rms(x_ref[...], g_ref[...]).astype(BF16)
    u = _dot(xs, w_ref[...])
    u_ref[...] = u
    for k in range(u4_ref.shape[0]):
        u4_ref[k, 0] = u[k * S5_CHUNK:(k + 1) * S5_CHUNK, :].T.astype(BF16)


def _s5_in(x2, g, w, bsz, seq, *, tm=512):
    m, d = x2.shape
    n = w.shape[1]
    tm = min(tm, seq)
    tiles_per_seq = seq // tm
    ck = tm // S5_CHUNK
    return pl.pallas_call(
        _s5_in_body,
        grid=(m // tm,),
        in_specs=[pl.BlockSpec((tm, d), lambda i: (i, 0)),
                  pl.BlockSpec((1, d), lambda i: (0, 0)),
                  pl.BlockSpec((d, n), lambda i: (0, 0))],
        out_specs=[pl.BlockSpec((tm, n), lambda i: (i, 0)),
                   pl.BlockSpec((ck, 1, n, S5_CHUNK), lambda i: (i % tiles_per_seq, i // tiles_per_seq, 0, 0))],
        out_shape=[jax.ShapeDtypeStruct((m, n), F32),
                   jax.ShapeDtypeStruct((seq // S5_CHUNK, bsz, n, S5_CHUNK), BF16)],
        compiler_params=_cparams(("parallel",)),
        name="odd_in",
    )(x2, g, w)


def _pad_cols(w, width):
    return jnp.pad(w, ((0, 0), (0, width - w.shape[1])))


def _pad_rows(w, height):
    return jnp.pad(w, ((0, height - w.shape[0]), (0, 0)))


def _even_in_weights(w_in):
    o = 0
    pieces = {}
    for name, width in (("cq", A_Q_RANK), ("ckv", A_KV_RANK), ("krope", A_ROPE_DIM), ("kidx", IDX_DIM),
                        ("widx", IDX_HEADS), ("rkv", 3 * B_WIDTH), ("wl", B_DECAY_LORA),
                        ("al", B_A_LORA), ("gl", B_GATE_LORA)):
        pieces[name] = w_in[:, o:o + width]
        o += width
    w_a = jnp.concatenate([pieces["cq"], pieces["ckv"], _pad_cols(pieces["krope"], LANES),
                           _pad_cols(pieces["kidx"], LANES), _pad_cols(pieces["widx"], LANES)], axis=1)
    w_b = jnp.concatenate([pieces["rkv"], _pad_cols(pieces["wl"], LANES), _pad_cols(pieces["al"], LANES),
                           _pad_cols(pieces["gl"], 2 * LANES)], axis=1)
    return w_a.astype(BF16), w_b.astype(BF16)


def _pad_mu(mu):
    o = 3 * B_WIDTH
    wl = mu[o:o + B_DECAY_LORA]
    al = mu[o + B_DECAY_LORA:o + B_DECAY_LORA + B_A_LORA]
    gl = mu[o + B_DECAY_LORA + B_A_LORA:]
    z = lambda n: jnp.zeros((n,), F32)
    return jnp.concatenate([mu[:o], wl, z(LANES - B_DECAY_LORA), al, z(LANES - B_A_LORA),
                            gl, z(2 * LANES - B_GATE_LORA)])[None, :]


def _dsa_mixer(h_a3, start_frame, cq_norm, ckv_norm, kidx_norm, w_uq, w_uk, w_uv, w_qidx):
    wuq = w_uq.reshape(A_Q_RANK, A_HEADS * A_HEAD_DIM).astype(BF16)
    wuk_t = jnp.pad(jnp.transpose(w_uk, (1, 2, 0)), ((0, 0), (A_ROPE_DIM, 0), (0, 0))).astype(BF16)
    wqi = jnp.pad(w_qidx, ((0, 0), (0, 0), (0, LANES - IDX_DIM))).reshape(A_Q_RANK, IDX_HEADS * LANES).astype(BF16)
    wuv = jnp.transpose(w_uv, (1, 0, 2)).astype(BF16)
    qcat, kcat, qidx, kidx, widx = _dsa_proj(
        h_a3, start_frame, _rope_tables(), cq_norm[None, :], ckv_norm[None, :],
        _pad_cols(kidx_norm[None, :], LANES), wuq, wuk_t, wqi)
    return _dsa_attend(qidx, widx, kidx, qcat, kcat, wuv)


def _rwkv_mixer(h_b3, mu, w0, w2, a0, a2, g2, k_k, k_a, r_k, ln_w, ln_b):
    rkv, ld, a, g = _rwkv_prep(h_b3, _pad_mu(mu), w0[None, :], a0[None, :],
                               _pad_rows(w2, LANES).astype(BF16), _pad_rows(a2, LANES).astype(BF16),
                               _pad_rows(g2, 2 * LANES).astype(BF16))
    par = jnp.concatenate([k_k[None, :], k_a[None, :], r_k.reshape(1, B_WIDTH), ln_w[None, :], ln_b[None, :],
                           jnp.zeros((3, B_WIDTH), F32)], axis=0)
    return _rwkv_scan(rkv, ld, a, g, par)


def _even_mixer(x2, bsz, seq, start_frame, g_mix, w_in, w_out, cq_norm, ckv_norm, kidx_norm, w_uq, w_uk, w_uv,
                w_qidx, mu, w0, w2, a0, a2, g2, k_k, k_a, r_k, ln_w, ln_b):
    w_a, w_b = _even_in_weights(w_in)
    rms_pro = lambda x, g: _rms(x, g)
    h_a = _matmul([x2], [g_mix], w_a, [], prologue=rms_pro, epilogue=_epi_id, out_dtype=F32,
                  tm=512, tn=A_PAD_COLS, name="even_in_a")
    h_b = _matmul([x2], [g_mix], w_b, [], prologue=rms_pro, epilogue=_epi_id, out_dtype=F32,
                  tm=512, tn=B_PAD_COLS // 2, name="even_in_b")
    y_a = _dsa_mixer(h_a.reshape(bsz, seq, A_PAD_COLS), start_frame, cq_norm, ckv_norm, kidx_norm,
                     w_uq, w_uk, w_uv, w_qidx)
    y_b = _rwkv_mixer(h_b.reshape(bsz, seq, B_PAD_COLS), mu, w0, w2, a0, a2, g2, k_k, k_a, r_k, ln_w, ln_b)

    m = bsz * seq
    cat_pro = lambda ya, yb: jnp.concatenate([ya, yb], axis=-1)
    return _matmul([y_a.reshape(m, A_WIDTH), y_b.reshape(m, B_WIDTH)], [], w_out.astype(BF16), [(x2, "tile")],
                   prologue=cat_pro, epilogue=_epi_add, out_dtype=F32, tm=512, tn=D_MODEL, name="even_out")


def _gelu_tanh(y):
    return 0.5 * y * (1.0 + jnp.tanh(math.sqrt(2.0 / math.pi) * (y + 0.044715 * (y * y * y))))


def _odd_mixer(x2, bsz, seq, g_mix, w_in, w_out, lam_re, lam_im, log_dt, b_re, b_im, c_re, c_im, d_skip,
               w_glu, b_glu):
    u, u4 = _s5_in(x2, g_mix, w_in.astype(BF16), bsz, seq)
    gated = _s5_mixer(u, u4, lam_re, lam_im, log_dt, b_re, b_im, c_re, c_im, d_skip, w_glu, b_glu)
    return _matmul([gated], [], w_out.astype(BF16), [(x2, "tile")], prologue=_pro_id, epilogue=_epi_add,
                   out_dtype=F32, tm=512, tn=D_MODEL, name="odd_out")


def _s5_mixer(u, u4, lam_re, lam_im, log_dt, b_re, b_im, c_re, c_im, d_skip, w_glu, b_glu):
    n_chunk, bsz = u4.shape[0], u4.shape[1]
    m = u.shape[0]
    assert bsz % 8 == 0, "chunk rows are addressed as whole sublane tiles"

    ldt = jnp.broadcast_to(log_dt[:, None], lam_re.shape)
    lam_r = jnp.pad(jnp.stack([lam_re, lam_im, ldt], axis=1), ((0, 0), (0, 5), (0, 0)))
    lam_c = jnp.transpose(lam_r, (0, 2, 1))
    bt = jnp.stack([jnp.transpose(b_re, (0, 2, 1)), jnp.transpose(b_im, (0, 2, 1))], axis=1)
    c = jnp.stack([c_re, c_im], axis=1)
    ct = jnp.stack([jnp.transpose(c_re, (0, 2, 1)), jnp.transpose(c_im, (0, 2, 1))], axis=1)
    m_mat, w_mat, v_mat, lc = _s5_build(lam_c, lam_r, bt, c, ct)

    y4 = _s5_apply(u4, m_mat, w_mat, v_mat, lc)
    y = jnp.transpose(y4, (1, 0, 3, 2)).reshape(m, C_WIDTH)

    z_of = lambda yy, uu, dd: _gelu_tanh(yy + dd * uu)
    return _matmul([y, u], [d_skip[None, :]], w_glu.astype(BF16),
                   [(None, 0), (None, 1), (d_skip[None, :], "row"), (b_glu[None, :], "row")],
                   prologue=z_of,
                   epilogue=lambda acc, yy, uu, dd, bb: z_of(yy, uu, dd) * jax.nn.sigmoid(acc + bb),
                   out_dtype=BF16, tm=512, tn=C_WIDTH, name="s5_glu")


def _final_norm_body(x_ref, g_ref, o_ref):
    o_ref[...] = _rms(x_ref[...], g_ref[...])


def _final_norm(x2, g, *, tm=512):
    m, d = x2.shape
    tm = min(tm, m)
    return pl.pallas_call(
        _final_norm_body,
        grid=(m // tm,),
        in_specs=[pl.BlockSpec((tm, d), lambda i: (i, 0)), pl.BlockSpec((1, d), lambda i: (0, 0))],
        out_specs=pl.BlockSpec((tm, d), lambda i: (i, 0)),
        out_shape=jax.ShapeDtypeStruct((m, d), F32),
        compiler_params=_cparams(("parallel",)),
        name="final_norm",
    )(x2, g)


def kernel(x, mem, start_frame, norm_mix, norm_xattn, norm_mem, norm_ffn, final_norm, xattn_wq, xattn_wkv, xattn_wo, ffn_up, ffn_down, even_w_in, even_w_out, dsa_cq_norm, dsa_ckv_norm, dsa_kidx_norm, dsa_w_uq, dsa_w_uk, dsa_w_uv, dsa_w_qidx, rwkv_mu, rwkv_w0, rwkv_w2, rwkv_a0, rwkv_a2, rwkv_g2, rwkv_k_k, rwkv_k_a, rwkv_r_k, rwkv_ln_w, rwkv_ln_b, odd_w_in, odd_w_out, s5_lam_re, s5_lam_im, s5_log_dt, s5_b_re, s5_b_im, s5_c_re, s5_c_im, s5_d, s5_w_glu, s5_b_glu):
    bsz, seq, d = x.shape
    depth = norm_mix.shape[0]
    m = bsz * seq
    x2 = x.reshape(m, d)
    mem2 = mem.reshape(bsz * mem.shape[1], d)
    for layer in range(depth):
        i = layer // 2
        g_mix = norm_mix[layer][None, :]
        if layer % 2 == 0:
            x2 = _even_mixer(x2, bsz, seq, start_frame, g_mix, even_w_in[i], even_w_out[i], dsa_cq_norm[i],
                             dsa_ckv_norm[i], dsa_kidx_norm[i], dsa_w_uq[i], dsa_w_uk[i], dsa_w_uv[i],
                             dsa_w_qidx[i], rwkv_mu[i], rwkv_w0[i], rwkv_w2[i], rwkv_a0[i], rwkv_a2[i],
                             rwkv_g2[i], rwkv_k_k[i], rwkv_k_a[i], rwkv_r_k[i], rwkv_ln_w[i], rwkv_ln_b[i])
        else:
            x2 = _odd_mixer(x2, bsz, seq, g_mix, odd_w_in[i], odd_w_out[i], s5_lam_re[i], s5_lam_im[i],
                            s5_log_dt[i], s5_b_re[i], s5_b_im[i], s5_c_re[i], s5_c_im[i], s5_d[i],
                            s5_w_glu[i], s5_b_glu[i])
        kv = _matmul([mem2], [norm_mem[layer][None, :]], xattn_wkv[layer].astype(BF16), [],
                     prologue=lambda a, g: _rms(a, g), epilogue=_epi_id, out_dtype=BF16, tm=512, tn=1024,
                     name="xattn_kv")
        x2 = _xattn(x2.reshape(bsz, seq, d), norm_xattn[layer][None, :], xattn_wq[layer].astype(BF16),
                    kv.reshape(bsz, mem.shape[1], -1), xattn_wo[layer].astype(BF16)).reshape(m, d)
        x2 = _ffn(x2, norm_ffn[layer][None, :], ffn_up[layer].astype(BF16), ffn_down[layer].astype(BF16))
    return _final_norm(x2, final_norm[None, :]).reshape(bsz, seq, d)
```

```python
import functools
import math

import numpy as np
import jax
import jax.numpy as jnp
from jax import lax
from jax.experimental import pallas as pl
from jax.experimental.pallas import tpu as pltpu

F32 = jnp.float32
BF16 = jnp.bfloat16
I32 = jnp.int32

D_MODEL = 2048
CHUNK = 64
ROPE_THETA = 500000.0
NORM_EPS = 1e-5

A_HEADS = 8
A_HEAD_DIM = 128
A_ROPE_DIM = 32
A_V_DIM = 128
A_WIDTH = A_HEADS * A_V_DIM
A_Q_RANK = 512
A_KV_RANK = 256
IDX_HEADS = 16
IDX_DIM = 64
IDX_ROPE_DIM = 16
TOPK_MAX = 256

B_HEAD = 64
B_WIDTH = 1024
B_HEADS = 16
B_DECAY_LORA = 64
B_A_LORA = 64
B_GATE_LORA = 160
B_LN_EPS = 64e-5

C_WIDTH = 2048
C_GROUP = 16
C_GROUPS = 128
C_STATE = 64

X_HEADS = 4
X_HEAD_DIM = 128
FFN_DIM = 4 * D_MODEL

LANES = 128
VMEM_LIMIT = 56 * 1024 * 1024

A_PAD_COLS = A_Q_RANK + A_KV_RANK + 3 * LANES
B_LORA_COLS = 2 * LANES + 2 * LANES
B_PAD_COLS = 3 * B_WIDTH + B_LORA_COLS
KCAT = A_KV_RANK + LANES

INT_MIN = -2147483648
NEG_BIG = -1e30
M_FLOOR = -1e20
SOFTMAX_C = (A_HEAD_DIM ** -0.5) * math.log2(math.e)
COUNT_ROWS = 32


def _cparams(sem):
    return pltpu.CompilerParams(dimension_semantics=sem, vmem_limit_bytes=VMEM_LIMIT)


def _rms(x, g):
    return x * lax.rsqrt(jnp.mean(x * x, axis=-1, keepdims=True) + NORM_EPS) * g


def _dot(a, b):
    return jnp.dot(a, b, preferred_element_type=F32)


def _dot_nt(a, b):
    return lax.dot_general(a, b, (((1,), (1,)), ((), ())), preferred_element_type=F32)


def _dot_tn(a, b):
    return lax.dot_general(a, b, (((0,), (0,)), ((), ())), preferred_element_type=F32)


def _split2(x):
    hi = x.astype(BF16)
    lo = (x - hi.astype(F32)).astype(BF16)
    return hi, lo


def _dotp(a, b, passes, kind="nn"):
    f = {"nn": _dot, "nt": _dot_nt, "tn": _dot_tn}[kind]
    if passes == 1:
        return f(a.astype(BF16), b.astype(BF16))
    ah, al = _split2(a)
    bh, bl = _split2(b)
    return f(ah, bh) + (f(ah, bl) + f(al, bh))


def _mm_body(*refs, n_row, n_const, epi_kinds, single_n, prologue, epilogue):
    rows = refs[:n_row]
    consts = refs[n_row:n_row + n_const]
    w_ref = refs[n_row + n_const]
    epi_refs = iter(refs[n_row + n_const + 1:])
    epis = [rows[kind] if isinstance(kind, int) else next(epi_refs) for kind in epi_kinds]
    if single_n:
        o_ref = refs[-1]
        xs = prologue(*[r[...] for r in rows], *[c[...] for c in consts]).astype(BF16)
    else:
        o_ref, xs_ref = refs[-2], refs[-1]

        @pl.when(pl.program_id(1) == 0)
        def _():
            xs_ref[...] = prologue(*[r[...] for r in rows], *[c[...] for c in consts]).astype(BF16)

        xs = xs_ref[...]
    acc = _dot(xs, w_ref[...])
    o_ref[...] = epilogue(acc, *[e[...] for e in epis]).astype(o_ref.dtype)


def _matmul(rows, consts, w, epis, *, prologue, epilogue, out_dtype, tm, tn, name):
    m = rows[0].shape[0]
    k, n = w.shape
    tm = min(tm, m)
    tn = min(tn, n)
    assert m % tm == 0 and n % tn == 0, (m, tm, n, tn)
    single_n = tn == n
    in_specs = [pl.BlockSpec((tm, r.shape[1]), lambda i, j: (i, 0)) for r in rows]
    in_specs += [pl.BlockSpec(c.shape, lambda i, j: (0, 0)) for c in consts]
    in_specs += [pl.BlockSpec((k, tn), lambda i, j: (0, j))]
    for _, kind in epis:
        if kind == "tile":
            in_specs.append(pl.BlockSpec((tm, tn), lambda i, j: (i, j)))
        elif kind == "row":
            in_specs.append(pl.BlockSpec((1, tn), lambda i, j: (0, j)))
        else:
            assert single_n and rows[kind].shape[1] == n
    body = functools.partial(_mm_body, n_row=len(rows), n_const=len(consts),
                             epi_kinds=tuple(kind for _, kind in epis), single_n=single_n,
                             prologue=prologue, epilogue=epilogue)
    return pl.pallas_call(
        body,
        grid=(m // tm, n // tn),
        in_specs=in_specs,
        out_specs=pl.BlockSpec((tm, tn), lambda i, j: (i, j)),
        out_shape=jax.ShapeDtypeStruct((m, n), out_dtype),
        scratch_shapes=[] if single_n else [pltpu.VMEM((tm, k), BF16)],
        compiler_params=_cparams(("parallel", "arbitrary")),
        name=name,
    )(*rows, *consts, w, *[e for e, kind in epis if not isinstance(kind, int)])


def _epi_id(acc):
    return acc


def _epi_add(acc, res):
    return acc + res


def _pro_id(x):
    return x


def _ffn_body(x_ref, g_ref, up_ref, dn_ref, gout_ref, o_ref, xs_ref, *, norm_out):
    j = pl.program_id(1)

    @pl.when(j == 0)
    def _():
        x = x_ref[...]
        xs_ref[...] = _rms(x, g_ref[...]).astype(BF16)
        o_ref[...] = x

    h = _dot(xs_ref[...], up_ref[...])
    h = jnp.square(jnp.maximum(h, 0.0)).astype(BF16)
    o_ref[...] += _dot(h, dn_ref[...])

    if norm_out:
        @pl.when(j == pl.num_programs(1) - 1)
        def _():
            o_ref[...] = _rms(o_ref[...], gout_ref[...])


def _ffn(x, g, w_up, w_dn, g_out=None, *, tm=1024, tf=512):
    m, d = x.shape
    f = w_up.shape[1]
    tm = min(tm, m)
    body = functools.partial(_ffn_body, norm_out=g_out is not None)
    return pl.pallas_call(
        body,
        grid=(m // tm, f // tf),
        in_specs=[pl.BlockSpec((tm, d), lambda i, j: (i, 0)),
                  pl.BlockSpec((1, d), lambda i, j: (0, 0)),
                  pl.BlockSpec((d, tf), lambda i, j: (0, j)),
                  pl.BlockSpec((tf, d), lambda i, j: (j, 0)),
                  pl.BlockSpec((1, d), lambda i, j: (0, 0))],
        out_specs=pl.BlockSpec((tm, d), lambda i, j: (i, 0)),
        out_shape=jax.ShapeDtypeStruct((m, d), F32),
        scratch_shapes=[pltpu.VMEM((tm, d), BF16)],
        compiler_params=_cparams(("parallel", "arbitrary")),
        name="ffn",
    )(x, g, w_up, w_dn, g if g_out is None else g_out)


def _xattn_body(x_ref, g_ref, wq_ref, kv_ref, wo_ref, o_ref):
    x = x_ref[0]
    xn = _rms(x, g_ref[...]).astype(BF16)
    q = _dot(xn, wq_ref[...])
    width = X_HEADS * X_HEAD_DIM
    outs = []
    for h in range(X_HEADS):
        sl = slice(h * X_HEAD_DIM, (h + 1) * X_HEAD_DIM)
        qh = q[:, sl].astype(BF16)
        kh = kv_ref[0, :, sl]
        vh = kv_ref[0, :, width + h * X_HEAD_DIM:width + (h + 1) * X_HEAD_DIM]
        s = _dot_nt(qh, kh) * (X_HEAD_DIM ** -0.5)
        s = s - jnp.max(s, axis=-1, keepdims=True)
        p = jnp.exp(s)
        p = p / jnp.sum(p, axis=-1, keepdims=True)
        outs.append(_dot(p.astype(BF16), vh))
    o = jnp.concatenate(outs, axis=-1).astype(BF16)
    o_ref[0] = x + _dot(o, wo_ref[...])


def _xattn(x3, g, wq, kv, wo, *, tm=512):
    b, s, d = x3.shape
    tm = min(tm, s)
    mlen = kv.shape[1]
    width = X_HEADS * X_HEAD_DIM
    return pl.pallas_call(
        _xattn_body,
        grid=(b, s // tm),
        in_specs=[pl.BlockSpec((1, tm, d), lambda bi, i: (bi, i, 0)),
                  pl.BlockSpec((1, d), lambda bi, i: (0, 0)),
                  pl.BlockSpec((d, width), lambda bi, i: (0, 0)),
                  pl.BlockSpec((1, mlen, 2 * width), lambda bi, i: (bi, 0, 0)),
                  pl.BlockSpec((width, d), lambda bi, i: (0, 0))],
        out_specs=pl.BlockSpec((1, tm, d), lambda bi, i: (bi, i, 0)),
        out_shape=jax.ShapeDtypeStruct((b, s, d), F32),
        compiler_params=_cparams(("parallel", "parallel")),
        name="xattn",
    )(x3, g, wq, kv, wo)


def _rope_tables():
    inv_a = ROPE_THETA ** (-jnp.arange(0, A_ROPE_DIM, 2, dtype=F32) / A_ROPE_DIM)
    inv_i = ROPE_THETA ** (-jnp.arange(0, IDX_ROPE_DIM, 2, dtype=F32) / IDX_ROPE_DIM)
    ha, hi = A_ROPE_DIM // 2, IDX_ROPE_DIM // 2

    def row(*pieces):
        r = jnp.concatenate([jnp.asarray(p, F32) for p in pieces])
        return jnp.pad(r, (0, LANES - r.shape[0]))

    ones, zeros = (lambda n: jnp.ones((n,), F32)), (lambda n: jnp.zeros((n,), F32))
    return jnp.stack([row(inv_a, inv_a), row(ones(2 * ha)), row(-ones(ha)), row(zeros(ha), ones(ha)),
                      row(inv_i, inv_i), row(-ones(hi)), row(zeros(hi), ones(hi)), zeros(LANES)])


def _dsa_proj_body(sf_ref, h_ref, tab_ref, nq_ref, nkv_ref, nki_ref, wuq_ref, wuk_ref, wqi_ref,
                   qcat_ref, kcat_ref, qidx_ref, kidx_ref, widx_ref):
    bi, i = pl.program_id(0), pl.program_id(1)
    tm = h_ref.shape[1]
    ha, hi = A_ROPE_DIM // 2, IDX_ROPE_DIM // 2
    pos = (sf_ref[bi] + i * tm + lax.broadcasted_iota(I32, (tm, LANES), 0)).astype(F32)
    tab = tab_ref[...]
    ang_a = pos * tab[0:1]
    cos_a = jnp.cos(ang_a) * tab[1:2]
    sin_a = jnp.sin(ang_a)
    sin_a1, sin_a2 = sin_a * tab[2:3], sin_a * tab[3:4]
    ang_i = pos * tab[4:5]
    cos_i = jnp.cos(ang_i)
    sin_i = jnp.sin(ang_i)
    sin_i1, sin_i2 = sin_i * tab[5:6], sin_i * tab[6:7]

    def rope_a(xb):
        return xb * cos_a + pltpu.roll(xb, LANES - ha, 1) * sin_a1 + pltpu.roll(xb, ha, 1) * sin_a2

    def rope_i(xb):
        return xb * cos_i + pltpu.roll(xb, LANES - hi, 1) * sin_i1 + pltpu.roll(xb, hi, 1) * sin_i2

    c_q = h_ref[0, :, 0:A_Q_RANK]
    c_kv = h_ref[0, :, A_Q_RANK:A_Q_RANK + A_KV_RANK]
    off = A_Q_RANK + A_KV_RANK
    k_rope = h_ref[0, :, off:off + LANES]
    k_idx = h_ref[0, :, off + LANES:off + 2 * LANES]
    w_idx = h_ref[0, :, off + 2 * LANES:off + 3 * LANES]

    cqn = _rms(c_q, nq_ref[...]).astype(BF16)
    kcat_ref[0, :, 0:A_KV_RANK] = _rms(c_kv, nkv_ref[...]).astype(BF16)
    kcat_ref[0, :, A_KV_RANK:KCAT] = rope_a(k_rope).astype(BF16)

    ki = k_idx * lax.rsqrt(jnp.sum(k_idx * k_idx, axis=-1, keepdims=True) * (1.0 / IDX_DIM) + NORM_EPS)
    kidx_ref[0] = rope_i(ki * nki_ref[...]).astype(BF16)
    widx_ref[0] = w_idx * ((IDX_HEADS ** -0.5) * (IDX_DIM ** -0.5))

    q = _dot(cqn, wuq_ref[...])
    for h in range(A_HEADS):
        qh = q[:, h * A_HEAD_DIM:(h + 1) * A_HEAD_DIM]
        qcat_ref[0, h, :, 0:A_KV_RANK] = (_dot(qh.astype(BF16), wuk_ref[h]) * SOFTMAX_C).astype(BF16)
        qcat_ref[0, h, :, A_KV_RANK:KCAT] = (rope_a(qh) * SOFTMAX_C).astype(BF16)
    qi = _dot(cqn, wqi_ref[...])
    for h in range(IDX_HEADS):
        qidx_ref[0, h] = rope_i(qi[:, h * LANES:(h + 1) * LANES]).astype(BF16)


def _dsa_proj(h_a, start_frame, tab, nq, nkv, nki, wuq, wuk_t, wqi, *, tm=256):
    b, s, _ = h_a.shape
    tm = min(tm, s)
    c3 = lambda bi, i, sf: (bi, i, 0)
    c4 = lambda bi, i, sf: (bi, 0, i, 0)
    z2 = lambda bi, i, sf: (0, 0)
    z3 = lambda bi, i, sf: (0, 0, 0)
    grid_spec = pltpu.PrefetchScalarGridSpec(
        num_scalar_prefetch=1,
        grid=(b, s // tm),
        in_specs=[pl.BlockSpec((1, tm, A_PAD_COLS), c3),
                  pl.BlockSpec(tab.shape, z2),
                  pl.BlockSpec(nq.shape, z2),
                  pl.BlockSpec(nkv.shape, z2),
                  pl.BlockSpec(nki.shape, z2),
                  pl.BlockSpec(wuq.shape, z2),
                  pl.BlockSpec(wuk_t.shape, z3),
                  pl.BlockSpec(wqi.shape, z2)],
        out_specs=[pl.BlockSpec((1, A_HEADS, tm, KCAT), c4),
                   pl.BlockSpec((1, tm, KCAT), c3),
                   pl.BlockSpec((1, IDX_HEADS, tm, LANES), c4),
                   pl.BlockSpec((1, tm, LANES), c3),
                   pl.BlockSpec((1, tm, LANES), c3)],
    )
    return pl.pallas_call(
        _dsa_proj_body,
        grid_spec=grid_spec,
        out_shape=[jax.ShapeDtypeStruct((b, A_HEADS, s, KCAT), BF16),
                   jax.ShapeDtypeStruct((b, s, KCAT), BF16),
                   jax.ShapeDtypeStruct((b, IDX_HEADS, s, LANES), BF16),
                   jax.ShapeDtypeStruct((b, s, LANES), BF16),
                   jax.ShapeDtypeStruct((b, s, LANES), F32)],
        compiler_params=_cparams(("parallel", "parallel")),
        name="dsa_proj",
    )(start_frame, h_a, tab, nq, nkv, nki, wuq, wuk_t, wqi)


def _dsa_attend_body(qidx_ref, widx_ref, kidx_ref, qcat_ref, kcat_ref, wuv_ref, o_ref,
                     key_ref, m_ref, l_ref, a_ref, acc_ref, t_ref, p_ref, *, tq, kt, topk):
    i = pl.program_id(1)
    t0 = i * tq
    nk = (t0 + tq + kt - 1) // kt
    krow = lax.broadcasted_iota(I32, (kt, tq), 0)
    qcol = lax.broadcasted_iota(I32, (kt, tq), 1)
    limit = ((t0 + qcol) // CHUNK + 1) * CHUNK

    wt = widx_ref[0]

    def score_tile(j, carry):
        off = pl.multiple_of(j * kt, kt)
        kid = kidx_ref[0, pl.ds(off, kt), :]
        sc = jnp.zeros((kt, tq), F32)
        for h in range(IDX_HEADS):
            sc = sc + wt[h:h + 1, :] * jnp.maximum(_dot_nt(kid, qidx_ref[0, h]), 0.0)
        bits = pltpu.bitcast(sc, I32)
        key = jnp.where(bits >= 0, bits, bits ^ 0x7FFFFFFF)
        key_ref[j] = jnp.where(off + krow < limit, key, INT_MIN)
        return carry

    lax.fori_loop(0, nk, score_tile, 0)

    def count_ge(cand):
        def body(j, acc):
            hit = jnp.where(key_ref[j] >= cand, 1.0, 0.0)
            return acc + jnp.sum(hit.reshape(kt // COUNT_ROWS, COUNT_ROWS, tq), axis=0)
        acc = lax.fori_loop(0, nk, body, jnp.zeros((COUNT_ROWS, tq), F32))
        return jnp.sum(acc, axis=0, keepdims=True)

    kf = float(topk)
    thr = jnp.where(count_ge(jnp.zeros((1, tq), I32)) >= kf, 0, INT_MIN).astype(I32)

    def bit_step(it, thr):
        cand = thr | jnp.left_shift(jnp.int32(1), 30 - it)
        return jnp.where(count_ge(cand) >= kf, cand, thr)

    thr = lax.fori_loop(0, 31, bit_step, thr)
    thr = jnp.maximum(thr, INT_MIN + 1)

    m_ref[...] = jnp.full(m_ref.shape, M_FLOOR, F32)
    l_ref[...] = jnp.zeros(l_ref.shape, F32)
    acc_ref[...] = jnp.zeros(acc_ref.shape, F32)
    eye_q = jnp.where(lax.broadcasted_iota(I32, (tq, tq), 0) == lax.broadcasted_iota(I32, (tq, tq), 1),
                      1.0, 0.0).astype(BF16)
    lane_tiles = [slice(c * LANES, (c + 1) * LANES) for c in range(kt // LANES)]
    acc_tiles = [slice(c * LANES, (c + 1) * LANES) for c in range(A_KV_RANK // LANES)]

    def attend_tile(j, carry):
        off = pl.multiple_of(j * kt, kt)
        kc = kcat_ref[0, pl.ds(off, kt), :]
        ckv = kc[:, 0:A_KV_RANK]
        bias_t = jnp.where(key_ref[j] >= thr, 0.0, NEG_BIG).astype(BF16)
        bias = _dot_nt(eye_q, bias_t)
        for h in range(A_HEADS):
            t_ref[h] = _dot_nt(qcat_ref[0, h], kc) + bias
        for h in range(A_HEADS):
            ts = [t_ref[h, :, lt] for lt in lane_tiles]
            m_old = m_ref[h]
            tmax = functools.reduce(jnp.maximum, ts)
            m_new = jnp.maximum(m_old, jnp.max(tmax, axis=-1, keepdims=True))
            ps = [jnp.exp2(t - m_new) for t in ts]
            alpha = jnp.exp2(m_old - m_new)
            l_ref[h] = alpha * l_ref[h] + jnp.sum(functools.reduce(jnp.add, ps), axis=-1, keepdims=True)
            for lt, p in zip(lane_tiles, ps):
                p_ref[h, :, lt] = p.astype(BF16)
            a_ref[h] = alpha
            m_ref[h] = m_new
        for h in range(A_HEADS):
            pv = _dot(p_ref[h], ckv)
            for at in acc_tiles:
                acc_ref[h, :, at] = a_ref[h] * acc_ref[h, :, at] + pv[:, at]
        return carry

    lax.fori_loop(0, nk, attend_tile, 0)

    for h in range(A_HEADS):
        inv_l = 1.0 / l_ref[h]
        o_lat = jnp.concatenate([acc_ref[h, :, at] * inv_l for at in acc_tiles], axis=1)
        o_ref[0, :, h * A_V_DIM:(h + 1) * A_V_DIM] = _dot(o_lat.astype(BF16), wuv_ref[h]).astype(o_ref.dtype)


def _dsa_attend(qidx, widx, kidx, qcat, kcat, wuv, *, tq=256, kt=512):
    b, _, s, _ = qidx.shape
    kt = min(kt, s)
    tq = min(tq, s)
    topk = min(TOPK_MAX, s // 4)
    assert s % kt == 0 and s % tq == 0 and tq % CHUNK == 0
    body = functools.partial(_dsa_attend_body, tq=tq, kt=kt, topk=topk)
    widx_t = jnp.transpose(widx[:, :, 0:IDX_HEADS], (0, 2, 1))
    return pl.pallas_call(
        body,
        grid=(b, s // tq),
        in_specs=[pl.BlockSpec((1, IDX_HEADS, tq, LANES), lambda bi, i: (bi, 0, i, 0)),
                  pl.BlockSpec((1, IDX_HEADS, tq), lambda bi, i: (bi, 0, i)),
                  pl.BlockSpec((1, s, LANES), lambda bi, i: (bi, 0, 0)),
                  pl.BlockSpec((1, A_HEADS, tq, KCAT), lambda bi, i: (bi, 0, i, 0)),
                  pl.BlockSpec((1, s, KCAT), lambda bi, i: (bi, 0, 0)),
                  pl.BlockSpec(wuv.shape, lambda bi, i: (0, 0, 0))],
        out_specs=pl.BlockSpec((1, tq, A_WIDTH), lambda bi, i: (bi, i, 0)),
        out_shape=jax.ShapeDtypeStruct((b, s, A_WIDTH), BF16),
        scratch_shapes=[pltpu.VMEM((s // kt, kt, tq), I32),
                        pltpu.VMEM((A_HEADS, tq, LANES), F32),
                        pltpu.VMEM((A_HEADS, tq, LANES), F32),
                        pltpu.VMEM((A_HEADS, tq, LANES), F32),
                        pltpu.VMEM((A_HEADS, tq, A_KV_RANK), F32),
                        pltpu.VMEM((A_HEADS, tq, kt), F32),
                        pltpu.VMEM((A_HEADS, tq, kt), BF16)],
        compiler_params=_cparams(("parallel", "arbitrary")),
        name="dsa_attend",
    )(qidx, widx_t, kidx, qcat, kcat, wuv)


def _rwkv_prep_body(h_ref, prev_ref, mu_ref, w0_ref, a0_ref, w2_ref, a2_ref, g2_ref,
                    rkv_ref, ld_ref, a_ref, g_ref):
    i = pl.program_id(1)
    h = h_ref[0]
    tm = h.shape[0]
    prev = jnp.where(i > 0, prev_ref[0, 7:8, :], 0.0)
    row = lax.broadcasted_iota(I32, h.shape, 0)
    shifted = jnp.where(row == 0, prev, pltpu.roll(h, 1, 0))
    hm = h + (shifted - h) * mu_ref[...]
    w3 = 3 * B_WIDTH
    rkv_ref[0] = hm[:, 0:w3]
    wl = hm[:, w3:w3 + LANES]
    al = hm[:, w3 + LANES:w3 + 2 * LANES]
    gl = hm[:, w3 + 2 * LANES:w3 + 4 * LANES]
    wz = w0_ref[...] + _dot(jnp.tanh(wl).astype(BF16), w2_ref[...])
    log_w = -jax.nn.softplus(-wz) - 0.5
    ld_ref[0] = -jnp.exp(log_w)
    a_ref[0] = jax.nn.sigmoid(a0_ref[...] + _dot(al.astype(BF16), a2_ref[...]))
    g_ref[0] = _dot(jax.nn.sigmoid(gl).astype(BF16), g2_ref[...])


def _rwkv_prep(h_b, mu, w0, a0, w2, a2, g2, *, tm=256):
    b, s, cols = h_b.shape
    tm = min(tm, s)
    c3 = lambda bi, i: (bi, i, 0)
    z2 = lambda bi, i: (0, 0)
    prev_map = lambda bi, i: (bi, jnp.maximum(i * (tm // 8) - 1, 0), 0)
    out = lambda w: jax.ShapeDtypeStruct((b, s, w), F32)
    return pl.pallas_call(
        _rwkv_prep_body,
        grid=(b, s // tm),
        in_specs=[pl.BlockSpec((1, tm, cols), c3),
                  pl.BlockSpec((1, 8, cols), prev_map),
                  pl.BlockSpec(mu.shape, z2), pl.BlockSpec(w0.shape, z2), pl.BlockSpec(a0.shape, z2),
                  pl.BlockSpec(w2.shape, z2), pl.BlockSpec(a2.shape, z2), pl.BlockSpec(g2.shape, z2)],
        out_specs=[pl.BlockSpec((1, tm, 3 * B_WIDTH), c3), pl.BlockSpec((1, tm, B_WIDTH), c3),
                   pl.BlockSpec((1, tm, B_WIDTH), c3), pl.BlockSpec((1, tm, B_WIDTH), c3)],
        out_shape=[out(3 * B_WIDTH), out(B_WIDTH), out(B_WIDTH), out(B_WIDTH)],
        compiler_params=_cparams(("parallel", "parallel")),
        name="rwkv_prep",
    )(h_b, h_b, mu, w0, a0, w2, a2, g2)


RWKV_CHUNK = 64
RWKV_SUB = 16


def _bmm(a, b, kind="nn"):
    spec = {"nn": "bij,bjk->bik", "nt": "bik,bjk->bij"}[kind]
    return jnp.einsum(spec, a.astype(BF16), b.astype(BF16), preferred_element_type=F32)


def _unit_lower_inverse(l_mat, eye, same_blk):
    mm = _bmm
    ld = jnp.where(same_blk, l_mat, 0.0)
    lo = l_mat - ld
    x = eye + ld
    p = mm(ld, ld)
    x = x + mm(x, p)
    p = mm(p, p)
    x = x + mm(x, p)
    p = mm(p, p)
    d = x + mm(x, p)
    n = mm(d, lo)
    y = eye + n
    y = y + mm(y, mm(n, n))
    return mm(y, d)


def _rwkv_scan_body(r_ref, k_ref, v_ref, ld_ref, a_ref, g_ref, par_ref, o_ref, st_ref):
    c = RWKV_CHUNK
    tblk = r_ref.shape[1]
    nc = tblk // c
    nh = r_ref.shape[2] // B_HEAD

    @pl.when(pl.program_id(2) == 0)
    def _():
        st_ref[...] = jnp.zeros(st_ref.shape, F32)

    def heads(x):
        return jnp.concatenate([x[:, hd * B_HEAD:(hd + 1) * B_HEAD].reshape(nc, c, B_HEAD) for hd in range(nh)],
                               axis=0)

    def head_row(i):
        p = par_ref[i:i + 1, :]
        return jnp.concatenate([jnp.broadcast_to(p[:, hd * B_HEAD:(hd + 1) * B_HEAD][None], (nc, 1, B_HEAD))
                                for hd in range(nh)], axis=0)

    ri = lax.broadcasted_iota(I32, (c, c), 0)
    ci = lax.broadcasted_iota(I32, (c, c), 1)
    eye = jnp.where(ri == ci, 1.0, 0.0).astype(F32)[None]
    lower = (ri >= ci)[None]
    strict = (ri > ci)[None]
    same_blk = ((ri // RWKV_SUB) == (ci // RWKV_SUB))[None]

    rb = lax.broadcasted_iota(I32, (tblk, tblk), 0)
    cb = lax.broadcasted_iota(I32, (tblk, tblk), 1)
    tri = jnp.where((rb >= cb) & (rb // c == cb // c), 1.0, 0.0).astype(BF16)
    ld2 = ld_ref[0]
    l1 = ld2.astype(BF16)
    rem = ld2 - l1.astype(F32)
    l2 = rem.astype(BF16)
    l3 = (rem - l2.astype(F32)).astype(BF16)
    cs = heads(_dot(tri, l1) + (_dot(tri, l2) + _dot(tri, l3)))

    r, k, v = heads(r_ref[0]), heads(k_ref[0]), heads(v_ref[0])
    ld, a, g = heads(ld2), heads(a_ref[0]), heads(g_ref[0])
    kk = k * head_row(0)
    kk = kk * lax.rsqrt(jnp.maximum(jnp.sum(kk * kk, axis=-1, keepdims=True), 1e-24))
    km = k * (1.0 + (a - 1.0) * head_row(1))
    bv = kk * a
    cs_end = cs[:, c - 1:c, :]
    e_neg = jnp.exp(-cs)
    e_end = jnp.exp(cs_end - cs)
    at = -kk * jnp.exp(cs - ld)
    rt = r * jnp.exp(cs)
    bt, kt = bv * e_neg, km * e_neg
    bd, kd = bv * e_end, km * e_end
    p_end_t = jnp.swapaxes(jnp.broadcast_to(jnp.exp(cs_end), cs.shape), 1, 2)

    lab = jnp.where(strict, _bmm(at, bt, "nt"), 0.0)
    lak = jnp.where(strict, _bmm(at, kt, "nt"), 0.0)
    lrb = jnp.where(lower, _bmm(rt, bt, "nt"), 0.0)
    lrk = jnp.where(lower, _bmm(rt, kt, "nt"), 0.0)
    tinv = _unit_lower_inverse(lab, eye, same_blk)
    tat = _bmm(tinv, at)
    c1 = _bmm(tinv, _bmm(lak, v))
    r2 = (rt + _bmm(lrb, tat)).astype(BF16)
    c2 = _bmm(lrk, v) + _bmm(lrb, c1)
    bd_t, kd_t = jnp.swapaxes(bd, 1, 2), jnp.swapaxes(kd, 1, 2)
    gmat_t = _bmm(bd_t, tat).astype(BF16)
    c3_t = _bmm(bd_t, c1) + _bmm(kd_t, v)

    outs = [None] * (nh * nc)
    sts = [st_ref[hd] for hd in range(nh)]
    for ch in range(nc):
        for hd in range(nh):
            i = hd * nc + ch
            sb = sts[hd].astype(BF16)
            outs[i] = _dot(r2[i], sb) + c2[i]
            sts[hd] = sts[hd] * p_end_t[i] + (_dot(gmat_t[i], sb) + c3_t[i])
    for hd in range(nh):
        st_ref[hd] = sts[hd]
    out = jnp.stack(outs, axis=0)

    mean = jnp.mean(out, axis=-1, keepdims=True)
    var = jnp.mean(jnp.square(out - mean), axis=-1, keepdims=True)
    y = (out - mean) * lax.rsqrt(var + B_LN_EPS) * head_row(3) + head_row(4)
    y = (y + jnp.sum(r * km * head_row(2), axis=-1, keepdims=True) * v) * g
    for hd in range(nh):
        o_ref[0, :, hd * B_HEAD:(hd + 1) * B_HEAD] = y[hd * nc:(hd + 1) * nc].reshape(tblk, B_HEAD).astype(o_ref.dtype)


def _rwkv_scan(rkv, ld, a, g, par, *, tblk=512, hw=2 * LANES):
    b, s, _ = ld.shape
    tblk = min(tblk, s)
    npair = B_WIDTH // hw
    blk = (1, tblk, hw)
    return pl.pallas_call(
        _rwkv_scan_body,
        grid=(b, npair, s // tblk),
        in_specs=[pl.BlockSpec(blk, lambda bi, hp, t: (bi, t, hp)),
                  pl.BlockSpec(blk, lambda bi, hp, t: (bi, t, npair + hp)),
                  pl.BlockSpec(blk, lambda bi, hp, t: (bi, t, 2 * npair + hp)),
                  pl.BlockSpec(blk, lambda bi, hp, t: (bi, t, hp)),
                  pl.BlockSpec(blk, lambda bi, hp, t: (bi, t, hp)),
                  pl.BlockSpec(blk, lambda bi, hp, t: (bi, t, hp)),
                  pl.BlockSpec((8, hw), lambda bi, hp, t: (0, hp))],
        out_specs=pl.BlockSpec(blk, lambda bi, hp, t: (bi, t, hp)),
        out_shape=jax.ShapeDtypeStruct((b, s, B_WIDTH), BF16),
        scratch_shapes=[pltpu.VMEM((hw // B_HEAD, B_HEAD, B_HEAD), F32)],
        compiler_params=_cparams(("parallel", "parallel", "arbitrary")),
        name="rwkv_scan",
    )(rkv, rkv, rkv, ld, a, g, par)


S5_CHUNK = LANES
S5_ROWS = S5_CHUNK * C_GROUP
S5_PASSES = 3


def _s5_group_body(lam_c_ref, lam_r_ref, bt_ref, c_ref, ct_ref, u_ref, y_ref,
                   kv_ref, m_ref, w_ref, v_ref, z_ref, xs_ref, acc_ref, us_ref):
    L = S5_CHUNK
    n_chunk, bsz = u_ref.shape[0], u_ref.shape[1]
    rows = n_chunk * bsz

    def zoh(lre, lim, dt):
        lre = jnp.minimum(lre, -1e-4)
        rho, th = lre * dt, lim * dt
        mag = jnp.exp(rho)
        lbr, lbi = mag * jnp.cos(th), mag * jnp.sin(th)
        den = 1.0 / (lre * lre + lim * lim)
        cr = ((lbr - 1.0) * lre + lbi * lim) * den
        cim = (lbi * lre - (lbr - 1.0) * lim) * den
        return rho, th, cr, cim

    lc = lam_c_ref[0]
    rho_c, th_c, _, _ = zoh(lc[:, 0:1], lc[:, 1:2], jnp.exp(lc[:, 2:3]))
    lr = lam_r_ref[0]
    rho_r, th_r, cr_r, ci_r = zoh(lr[0:1], lr[1:2], jnp.exp(lr[2:3]))

    def powers(rho, th, n):
        mag = jnp.exp(rho * n)
        return mag * jnp.cos(th * n), mag * jnp.sin(th * n)

    btr, bti = bt_ref[0, 0], bt_ref[0, 1]
    bbr = btr * cr_r - bti * ci_r
    bbi = btr * ci_r + bti * cr_r
    c_re, c_im = c_ref[0, 0], c_ref[0, 1]

    d_re = jnp.concatenate([c_re * bbr[j:j + 1] - c_im * bbi[j:j + 1] for j in range(C_GROUP)], axis=0)
    d_im = jnp.concatenate([c_re * bbi[j:j + 1] + c_im * bbr[j:j + 1] for j in range(C_GROUP)], axis=0)
    tau = lax.broadcasted_iota(I32, (C_STATE, L), 1).astype(F32)
    e_re, e_im = powers(rho_c, th_c, tau)
    kvec = _dotp(d_re, e_re, S5_PASSES) - _dotp(d_im, e_im, S5_PASSES)

    kv_ref[...] = kvec

    back = (L - 1 - lax.broadcasted_iota(I32, (L, C_STATE), 0)).astype(F32)
    f_re, f_im = powers(rho_r, th_r, back)
    for j in range(C_GROUP):
        br, bi = bbr[j:j + 1], bbi[j:j + 1]
        w_ref[j * L:(j + 1) * L, :] = jnp.concatenate(
            [f_re * br - f_im * bi, f_re * bi + f_im * br], axis=1).astype(w_ref.dtype)

    e1_re, e1_im = powers(rho_c, th_c, tau + 1.0)
    ctr, cti = ct_ref[0, 0], ct_ref[0, 1]
    for i in range(C_GROUP):
        cr_i, ci_i = ctr[:, i:i + 1], cti[:, i:i + 1]
        v_ref[0:C_STATE, i * L:(i + 1) * L] = (cr_i * e1_re - ci_i * e1_im).astype(v_ref.dtype)
        v_ref[C_STATE:2 * C_STATE, i * L:(i + 1) * L] = (-(cr_i * e1_im + ci_i * e1_re)).astype(v_ref.dtype)

    pr, pi = powers(rho_r, th_r, float(L))
    lr_row = jnp.concatenate([pr, pr], axis=1)
    li_row = jnp.concatenate([-pi, pi], axis=1)

    for j in range(C_GROUP):
        us_ref[j] = u_ref[:, :, j, :].reshape(rows, L)
    u = jnp.concatenate([us_ref[j] for j in range(C_GROUP)], axis=1)
    z_ref[...] = _dot(u, w_ref[...])

    def step(ci, x):
        r0 = pl.multiple_of(ci * bsz, bsz)
        xs_ref[pl.ds(r0, bsz), :] = x
        return x * lr_row + pltpu.roll(x, C_STATE, 1) * li_row + z_ref[pl.ds(r0, bsz), :]

    lax.fori_loop(0, n_chunk, step, jnp.zeros((bsz, 2 * C_STATE), F32))
    acc_ref[...] = _dot(xs_ref[...].astype(BF16), v_ref[...])

    upper = lax.broadcasted_iota(I32, (L, L), 1) >= lax.broadcasted_iota(I32, (L, L), 0)

    def block_rows(jj, carry):
        for half in range(2):
            j = 2 * jj + half
            for i in range(C_GROUP):
                kv = jnp.broadcast_to(kv_ref[pl.ds(j * C_GROUP + i, 1), :], (L, L))
                blk = jnp.where(upper, pltpu.roll(kv, 0, 1, stride=1, stride_axis=0), 0.0)
                m_ref[half * L:(half + 1) * L, i * L:(i + 1) * L] = blk.astype(m_ref.dtype)
        uj = jnp.concatenate([us_ref[2 * jj], us_ref[2 * jj + 1]], axis=1)
        acc_ref[...] += _dot(uj, m_ref[...])
        return carry

    lax.fori_loop(0, C_GROUP // 2, block_rows, 0)
    for i in range(C_GROUP):
        y_ref[:, :, i, :] = acc_ref[:, i * L:(i + 1) * L].reshape(n_chunk, bsz, L)


def _s5_groups(u4, lam_c, lam_r, bt, c, ct):
    n_chunk, bsz, width, _ = u4.shape
    g = width // C_GROUP
    rows = n_chunk * bsz
    i3 = lambda gi: (gi, 0, 0)
    i4 = lambda gi: (gi, 0, 0, 0)
    blk = pl.BlockSpec((n_chunk, bsz, C_GROUP, S5_CHUNK), lambda gi: (0, 0, gi, 0))
    return pl.pallas_call(
        _s5_group_body,
        grid=(g,),
        in_specs=[pl.BlockSpec((1, C_STATE, 8), i3),
                  pl.BlockSpec((1, 8, C_STATE), i3),
                  pl.BlockSpec((1, 2, C_GROUP, C_STATE), i4),
                  pl.BlockSpec((1, 2, C_GROUP, C_STATE), i4),
                  pl.BlockSpec((1, 2, C_STATE, C_GROUP), i4),
                  blk],
        out_specs=blk,
        out_shape=jax.ShapeDtypeStruct(u4.shape, F32),
        scratch_shapes=[pltpu.VMEM((C_GROUP * C_GROUP, S5_CHUNK), F32),
                        pltpu.VMEM((2 * S5_CHUNK, S5_ROWS), BF16),
                        pltpu.VMEM((S5_ROWS, 2 * C_STATE), BF16),
                        pltpu.VMEM((2 * C_STATE, S5_ROWS), BF16),
                        pltpu.VMEM((rows, 2 * C_STATE), F32),
                        pltpu.VMEM((rows, 2 * C_STATE), F32),
                        pltpu.VMEM((rows, S5_ROWS), F32),
                        pltpu.VMEM((C_GROUP, rows, S5_CHUNK), BF16)],
        compiler_params=_cparams(("parallel",)),
        name="s5_groups",
    )(lam_c, lam_r, bt, c, ct, u4)


def _s5_in_body(x_ref, g_ref, w_ref, u_ref, u4_ref):
    xs = _rms(x_ref[...], g_ref[...]).astype(BF16)
    u = _dot(xs, w_ref[...])
    u_ref[...] = u
    for k in range(u4_ref.shape[0]):
        u4_ref[k, 0] = u[k * S5_CHUNK:(k + 1) * S5_CHUNK, :].T.astype(BF16)


def _s5_in(x2, g, w, bsz, seq, *, tm=512):
    m, d = x2.shape
    n = w.shape[1]
    tm = min(tm, seq)
    tiles_per_seq = seq // tm
    ck = tm // S5_CHUNK
    return pl.pallas_call(
        _s5_in_body,
        grid=(m // tm,),
        in_specs=[pl.BlockSpec((tm, d), lambda i: (i, 0)),
                  pl.BlockSpec((1, d), lambda i: (0, 0)),
                  pl.BlockSpec((d, n), lambda i: (0, 0))],
        out_specs=[pl.BlockSpec((tm, n), lambda i: (i, 0)),
                   pl.BlockSpec((ck, 1, n, S5_CHUNK), lambda i: (i % tiles_per_seq, i // tiles_per_seq, 0, 0))],
        out_shape=[jax.ShapeDtypeStruct((m, n), F32),
                   jax.ShapeDtypeStruct((seq // S5_CHUNK, bsz, n, S5_CHUNK), BF16)],
        compiler_params=_cparams(("parallel",)),
        name="odd_in",
    )(x2, g, w)


def _pad_cols(w, width):
    return jnp.pad(w, ((0, 0), (0, width - w.shape[1])))


def _pad_rows(w, height):
    return jnp.pad(w, ((0, height - w.shape[0]), (0, 0)))


def _even_in_weights(w_in):
    o = 0
    pieces = {}
    for name, width in (("cq", A_Q_RANK), ("ckv", A_KV_RANK), ("krope", A_ROPE_DIM), ("kidx", IDX_DIM),
                        ("widx", IDX_HEADS), ("rkv", 3 * B_WIDTH), ("wl", B_DECAY_LORA),
                        ("al", B_A_LORA), ("gl", B_GATE_LORA)):
        pieces[name] = w_in[:, o:o + width]
        o += width
    w_a = jnp.concatenate([pieces["cq"], pieces["ckv"], _pad_cols(pieces["krope"], LANES),
                           _pad_cols(pieces["kidx"], LANES), _pad_cols(pieces["widx"], LANES)], axis=1)
    w_b = jnp.concatenate([pieces["rkv"], _pad_cols(pieces["wl"], LANES), _pad_cols(pieces["al"], LANES),
                           _pad_cols(pieces["gl"], 2 * LANES)], axis=1)
    return w_a.astype(BF16), w_b.astype(BF16)


def _pad_mu(mu):
    o = 3 * B_WIDTH
    wl = mu[o:o + B_DECAY_LORA]
    al = mu[o + B_DECAY_LORA:o + B_DECAY_LORA + B_A_LORA]
    gl = mu[o + B_DECAY_LORA + B_A_LORA:]
    z = lambda n: jnp.zeros((n,), F32)
    return jnp.concatenate([mu[:o], wl, z(LANES - B_DECAY_LORA), al, z(LANES - B_A_LORA),
                            gl, z(2 * LANES - B_GATE_LORA)])[None, :]


def _dsa_mixer(h_a3, start_frame, cq_norm, ckv_norm, kidx_norm, w_uq, w_uk, w_uv, w_qidx):
    wuq = w_uq.reshape(A_Q_RANK, A_HEADS * A_HEAD_DIM).astype(BF16)
    wuk_t = jnp.pad(jnp.transpose(w_uk, (1, 2, 0)), ((0, 0), (A_ROPE_DIM, 0), (0, 0))).astype(BF16)
    wqi = jnp.pad(w_qidx, ((0, 0), (0, 0), (0, LANES - IDX_DIM))).reshape(A_Q_RANK, IDX_HEADS * LANES).astype(BF16)
    wuv = jnp.transpose(w_uv, (1, 0, 2)).astype(BF16)
    qcat, kcat, qidx, kidx, widx = _dsa_proj(
        h_a3, start_frame, _rope_tables(), cq_norm[None, :], ckv_norm[None, :],
        _pad_cols(kidx_norm[None, :], LANES), wuq, wuk_t, wqi)
    return _dsa_attend(qidx, widx, kidx, qcat, kcat, wuv)


def _rwkv_mixer(h_b3, mu, w0, w2, a0, a2, g2, k_k, k_a, r_k, ln_w, ln_b):
    rkv, ld, a, g = _rwkv_prep(h_b3, _pad_mu(mu), w0[None, :], a0[None, :],
                               _pad_rows(w2, LANES).astype(BF16), _pad_rows(a2, LANES).astype(BF16),
                               _pad_rows(g2, 2 * LANES).astype(BF16))
    par = jnp.concatenate([k_k[None, :], k_a[None, :], r_k.reshape(1, B_WIDTH), ln_w[None, :], ln_b[None, :],
                           jnp.zeros((3, B_WIDTH), F32)], axis=0)
    return _rwkv_scan(rkv, ld, a, g, par)


def _even_mixer(x2, bsz, seq, start_frame, g_mix, w_in, w_out, cq_norm, ckv_norm, kidx_norm, w_uq, w_uk, w_uv,
                w_qidx, mu, w0, w2, a0, a2, g2, k_k, k_a, r_k, ln_w, ln_b):
    w_a, w_b = _even_in_weights(w_in)
    rms_pro = lambda x, g: _rms(x, g)
    h_a = _matmul([x2], [g_mix], w_a, [], prologue=rms_pro, epilogue=_epi_id, out_dtype=F32,
                  tm=512, tn=A_PAD_COLS, name="even_in_a")
    h_b = _matmul([x2], [g_mix], w_b, [], prologue=rms_pro, epilogue=_epi_id, out_dtype=F32,
                  tm=512, tn=B_PAD_COLS // 2, name="even_in_b")
    y_a = _dsa_mixer(h_a.reshape(bsz, seq, A_PAD_COLS), start_frame, cq_norm, ckv_norm, kidx_norm,
                     w_uq, w_uk, w_uv, w_qidx)
    y_b = _rwkv_mixer(h_b.reshape(bsz, seq, B_PAD_COLS), mu, w0, w2, a0, a2, g2, k_k, k_a, r_k, ln_w, ln_b)

    m = bsz * seq
    cat_pro = lambda ya, yb: jnp.concatenate([ya, yb], axis=-1)
    return _matmul([y_a.reshape(m, A_WIDTH), y_b.reshape(m, B_WIDTH)], [], w_out.astype(BF16), [(x2, "tile")],
                   prologue=cat_pro, epilogue=_epi_add, out_dtype=F32, tm=512, tn=D_MODEL, name="even_out")


def _gelu_tanh(y):
    return 0.5 * y * (1.0 + jnp.tanh(math.sqrt(2.0 / math.pi) * (y + 0.044715 * (y * y * y))))


def _odd_mixer(x2, bsz, seq, g_mix, w_in, w_out, lam_re, lam_im, log_dt, b_re, b_im, c_re, c_im, d_skip,
               w_glu, b_glu):
    u, u4 = _s5_in(x2, g_mix, w_in.astype(BF16), bsz, seq)
    gated = _s5_mixer(u, u4, lam_re, lam_im, log_dt, b_re, b_im, c_re, c_im, d_skip, w_glu, b_glu)
    return _matmul([gated], [], w_out.astype(BF16), [(x2, "tile")], prologue=_pro_id, epilogue=_epi_add,
                   out_dtype=F32, tm=512, tn=D_MODEL, name="odd_out")


def _s5_mixer(u, u4, lam_re, lam_im, log_dt, b_re, b_im, c_re, c_im, d_skip, w_glu, b_glu):
    n_chunk, bsz = u4.shape[0], u4.shape[1]
    m = u.shape[0]
    assert bsz % 8 == 0, "chunk rows are addressed as whole sublane tiles"

    ldt = jnp.broadcast_to(log_dt[:, None], lam_re.shape)
    lam_r = jnp.pad(jnp.stack([lam_re, lam_im, ldt], axis=1), ((0, 0), (0, 5), (0, 0)))
    lam_c = jnp.transpose(lam_r, (0, 2, 1))
    bt = jnp.stack([jnp.transpose(b_re, (0, 2, 1)), jnp.transpose(b_im, (0, 2, 1))], axis=1)
    c = jnp.stack([c_re, c_im], axis=1)
    ct = jnp.stack([jnp.transpose(c_re, (0, 2, 1)), jnp.transpose(c_im, (0, 2, 1))], axis=1)
    y4 = _s5_groups(u4, lam_c, lam_r, bt, c, ct)
    y = jnp.transpose(y4, (1, 0, 3, 2)).reshape(m, C_WIDTH)

    z_of = lambda yy, uu, dd: _gelu_tanh(yy + dd * uu)
    return _matmul([y, u], [d_skip[None, :]], w_glu.astype(BF16),
                   [(None, 0), (None, 1), (d_skip[None, :], "row"), (b_glu[None, :], "row")],
                   prologue=z_of,
                   epilogue=lambda acc, yy, uu, dd, bb: z_of(yy, uu, dd) * jax.nn.sigmoid(acc + bb),
                   out_dtype=BF16, tm=512, tn=C_WIDTH, name="s5_glu")


def kernel(x, mem, start_frame, norm_mix, norm_xattn, norm_mem, norm_ffn, final_norm, xattn_wq, xattn_wkv, xattn_wo, ffn_up, ffn_down, even_w_in, even_w_out, dsa_cq_norm, dsa_ckv_norm, dsa_kidx_norm, dsa_w_uq, dsa_w_uk, dsa_w_uv, dsa_w_qidx, rwkv_mu, rwkv_w0, rwkv_w2, rwkv_a0, rwkv_a2, rwkv_g2, rwkv_k_k, rwkv_k_a, rwkv_r_k, rwkv_ln_w, rwkv_ln_b, odd_w_in, odd_w_out, s5_lam_re, s5_lam_im, s5_log_dt, s5_b_re, s5_b_im, s5_c_re, s5_c_im, s5_d, s5_w_glu, s5_b_glu):
    bsz, seq, d = x.shape
    depth = norm_mix.shape[0]
    m = bsz * seq
    x2 = x.reshape(m, d)
    mem2 = mem.reshape(bsz * mem.shape[1], d)
    for layer in range(depth):
        i = layer // 2
        g_mix = norm_mix[layer][None, :]
        if layer % 2 == 0:
            x2 = _even_mixer(x2, bsz, seq, start_frame, g_mix, even_w_in[i], even_w_out[i], dsa_cq_norm[i],
                             dsa_ckv_norm[i], dsa_kidx_norm[i], dsa_w_uq[i], dsa_w_uk[i], dsa_w_uv[i],
                             dsa_w_qidx[i], rwkv_mu[i], rwkv_w0[i], rwkv_w2[i], rwkv_a0[i], rwkv_a2[i],
                             rwkv_g2[i], rwkv_k_k[i], rwkv_k_a[i], rwkv_r_k[i], rwkv_ln_w[i], rwkv_ln_b[i])
        else:
            x2 = _odd_mixer(x2, bsz, seq, g_mix, odd_w_in[i], odd_w_out[i], s5_lam_re[i], s5_lam_im[i],
                            s5_log_dt[i], s5_b_re[i], s5_b_im[i], s5_c_re[i], s5_c_im[i], s5_d[i],
                            s5_w_glu[i], s5_b_glu[i])
        kv = _matmul([mem2], [norm_mem[layer][None, :]], xattn_wkv[layer].astype(BF16), [],
                     prologue=lambda a, g: _rms(a, g), epilogue=_epi_id, out_dtype=BF16, tm=512, tn=1024,
                     name="xattn_kv")
        x2 = _xattn(x2.reshape(bsz, seq, d), norm_xattn[layer][None, :], xattn_wq[layer].astype(BF16),
                    kv.reshape(bsz, mem.shape[1], -1), xattn_wo[layer].astype(BF16)).reshape(m, d)
        x2 = _ffn(x2, norm_ffn[layer][None, :], ffn_up[layer].astype(BF16), ffn_down[layer].astype(BF16),
                  final_norm[None, :] if layer == depth - 1 else None)
    return x2.reshape(bsz, seq, d)
```

```python
import functools
import math

import numpy as np
import jax
import jax.numpy as jnp
from jax import lax
from jax.experimental import pallas as pl
from jax.experimental.pallas import tpu as pltpu

F32 = jnp.float32
BF16 = jnp.bfloat16
I32 = jnp.int32

D_MODEL = 2048
CHUNK = 64
ROPE_THETA = 500000.0
NORM_EPS = 1e-5

A_HEADS = 8
A_HEAD_DIM = 128
A_ROPE_DIM = 32
A_V_DIM = 128
A_WIDTH = A_HEADS * A_V_DIM
A_Q_RANK = 512
A_KV_RANK = 256
IDX_HEADS = 16
IDX_DIM = 64
IDX_ROPE_DIM = 16
TOPK_MAX = 256

B_HEAD = 64
B_WIDTH = 1024
B_HEADS = 16
B_DECAY_LORA = 64
B_A_LORA = 64
B_GATE_LORA = 160
B_LN_EPS = 64e-5

C_WIDTH = 2048
C_GROUP = 16
C_GROUPS = 128
C_STATE = 64

X_HEADS = 4
X_HEAD_DIM = 128
FFN_DIM = 4 * D_MODEL

LANES = 128
VMEM_LIMIT = 56 * 1024 * 1024

A_PAD_COLS = A_Q_RANK + A_KV_RANK + 3 * LANES
B_LORA_COLS = 2 * LANES + 2 * LANES
B_PAD_COLS = 3 * B_WIDTH + B_LORA_COLS

INT_MIN = -2147483648
NEG_BIG = -1e30
M_FLOOR = -1e20
SOFTMAX_C = (A_HEAD_DIM ** -0.5) * math.log2(math.e)
COUNT_ROWS = 32


def _cparams(sem):
    return pltpu.CompilerParams(dimension_semantics=sem, vmem_limit_bytes=VMEM_LIMIT)


def _rms(x, g):
    return x * lax.rsqrt(jnp.mean(x * x, axis=-1, keepdims=True) + NORM_EPS) * g


def _dot(a, b):
    return jnp.dot(a, b, preferred_element_type=F32)


def _dot_nt(a, b):
    return lax.dot_general(a, b, (((1,), (1,)), ((), ())), preferred_element_type=F32)


def _dot_tn(a, b):
    return lax.dot_general(a, b, (((0,), (0,)), ((), ())), preferred_element_type=F32)


def _split2(x):
    hi = x.astype(BF16)
    lo = (x - hi.astype(F32)).astype(BF16)
    return hi, lo


def _dotp(a, b, passes, kind="nn"):
    f = {"nn": _dot, "nt": _dot_nt, "tn": _dot_tn}[kind]
    if passes == 1:
        return f(a.astype(BF16), b.astype(BF16))
    ah, al = _split2(a)
    bh, bl = _split2(b)
    return f(ah, bh) + (f(ah, bl) + f(al, bh))


def _mm_body(*refs, n_row, n_const, epi_kinds, single_n, prologue, epilogue):
    rows = refs[:n_row]
    consts = refs[n_row:n_row + n_const]
    w_ref = refs[n_row + n_const]
    epi_refs = iter(refs[n_row + n_const + 1:])
    epis = [rows[kind] if isinstance(kind, int) else next(epi_refs) for kind in epi_kinds]
    if single_n:
        o_ref = refs[-1]
        xs = prologue(*[r[...] for r in rows], *[c[...] for c in consts]).astype(BF16)
    else:
        o_ref, xs_ref = refs[-2], refs[-1]

        @pl.when(pl.program_id(1) == 0)
        def _():
            xs_ref[...] = prologue(*[r[...] for r in rows], *[c[...] for c in consts]).astype(BF16)

        xs = xs_ref[...]
    acc = _dot(xs, w_ref[...])
    o_ref[...] = epilogue(acc, *[e[...] for e in epis]).astype(o_ref.dtype)


def _matmul(rows, consts, w, epis, *, prologue, epilogue, out_dtype, tm, tn, name):
    m = rows[0].shape[0]
    k, n = w.shape
    tm = min(tm, m)
    tn = min(tn, n)
    assert m % tm == 0 and n % tn == 0, (m, tm, n, tn)
    single_n = tn == n
    in_specs = [pl.BlockSpec((tm, r.shape[1]), lambda i, j: (i, 0)) for r in rows]
    in_specs += [pl.BlockSpec(c.shape, lambda i, j: (0, 0)) for c in consts]
    in_specs += [pl.BlockSpec((k, tn), lambda i, j: (0, j))]
    for _, kind in epis:
        if kind == "tile":
            in_specs.append(pl.BlockSpec((tm, tn), lambda i, j: (i, j)))
        elif kind == "row":
            in_specs.append(pl.BlockSpec((1, tn), lambda i, j: (0, j)))
        else:
            assert single_n and rows[kind].shape[1] == n
    body = functools.partial(_mm_body, n_row=len(rows), n_const=len(consts),
                             epi_kinds=tuple(kind for _, kind in epis), single_n=single_n,
                             prologue=prologue, epilogue=epilogue)
    return pl.pallas_call(
        body,
        grid=(m // tm, n // tn),
        in_specs=in_specs,
        out_specs=pl.BlockSpec((tm, tn), lambda i, j: (i, j)),
        out_shape=jax.ShapeDtypeStruct((m, n), out_dtype),
        scratch_shapes=[] if single_n else [pltpu.VMEM((tm, k), BF16)],
        compiler_params=_cparams(("parallel", "arbitrary")),
        name=name,
    )(*rows, *consts, w, *[e for e, kind in epis if not isinstance(kind, int)])


def _epi_id(acc):
    return acc


def _epi_add(acc, res):
    return acc + res


def _pro_id(x):
    return x


def _ffn_body(x_ref, g_ref, up_ref, dn_ref, gout_ref, o_ref, xs_ref, *, norm_out):
    j = pl.program_id(1)

    @pl.when(j == 0)
    def _():
        x = x_ref[...]
        xs_ref[...] = _rms(x, g_ref[...]).astype(BF16)
        o_ref[...] = x

    h = _dot(xs_ref[...], up_ref[...])
    h = jnp.square(jnp.maximum(h, 0.0)).astype(BF16)
    o_ref[...] += _dot(h, dn_ref[...])

    if norm_out:
        @pl.when(j == pl.num_programs(1) - 1)
        def _():
            o_ref[...] = _rms(o_ref[...], gout_ref[...])


def _ffn(x, g, w_up, w_dn, g_out=None, *, tm=1024, tf=512):
    m, d = x.shape
    f = w_up.shape[1]
    tm = min(tm, m)
    body = functools.partial(_ffn_body, norm_out=g_out is not None)
    return pl.pallas_call(
        body,
        grid=(m // tm, f // tf),
        in_specs=[pl.BlockSpec((tm, d), lambda i, j: (i, 0)),
                  pl.BlockSpec((1, d), lambda i, j: (0, 0)),
                  pl.BlockSpec((d, tf), lambda i, j: (0, j)),
                  pl.BlockSpec((tf, d), lambda i, j: (j, 0)),
                  pl.BlockSpec((1, d), lambda i, j: (0, 0))],
        out_specs=pl.BlockSpec((tm, d), lambda i, j: (i, 0)),
        out_shape=jax.ShapeDtypeStruct((m, d), F32),
        scratch_shapes=[pltpu.VMEM((tm, d), BF16)],
        compiler_params=_cparams(("parallel", "arbitrary")),
        name="ffn",
    )(x, g, w_up, w_dn, g if g_out is None else g_out)


def _xattn_body(x_ref, g_ref, wq_ref, kv_ref, wo_ref, o_ref):
    x = x_ref[0]
    xn = _rms(x, g_ref[...]).astype(BF16)
    q = _dot(xn, wq_ref[...])
    width = X_HEADS * X_HEAD_DIM
    outs = []
    for h in range(X_HEADS):
        sl = slice(h * X_HEAD_DIM, (h + 1) * X_HEAD_DIM)
        qh = q[:, sl].astype(BF16)
        kh = kv_ref[0, :, sl]
        vh = kv_ref[0, :, width + h * X_HEAD_DIM:width + (h + 1) * X_HEAD_DIM]
        s = _dot_nt(qh, kh) * (X_HEAD_DIM ** -0.5)
        s = s - jnp.max(s, axis=-1, keepdims=True)
        p = jnp.exp(s)
        p = p / jnp.sum(p, axis=-1, keepdims=True)
        outs.append(_dot(p.astype(BF16), vh))
    o = jnp.concatenate(outs, axis=-1).astype(BF16)
    o_ref[0] = x + _dot(o, wo_ref[...])


def _xattn(x3, g, wq, kv, wo, *, tm=512):
    b, s, d = x3.shape
    tm = min(tm, s)
    mlen = kv.shape[1]
    width = X_HEADS * X_HEAD_DIM
    return pl.pallas_call(
        _xattn_body,
        grid=(b, s // tm),
        in_specs=[pl.BlockSpec((1, tm, d), lambda bi, i: (bi, i, 0)),
                  pl.BlockSpec((1, d), lambda bi, i: (0, 0)),
                  pl.BlockSpec((d, width), lambda bi, i: (0, 0)),
                  pl.BlockSpec((1, mlen, 2 * width), lambda bi, i: (bi, 0, 0)),
                  pl.BlockSpec((width, d), lambda bi, i: (0, 0))],
        out_specs=pl.BlockSpec((1, tm, d), lambda bi, i: (bi, i, 0)),
        out_shape=jax.ShapeDtypeStruct((b, s, d), F32),
        compiler_params=_cparams(("parallel", "parallel")),
        name="xattn",
    )(x3, g, wq, kv, wo)


def _rope_tables():
    inv_a = ROPE_THETA ** (-jnp.arange(0, A_ROPE_DIM, 2, dtype=F32) / A_ROPE_DIM)
    inv_i = ROPE_THETA ** (-jnp.arange(0, IDX_ROPE_DIM, 2, dtype=F32) / IDX_ROPE_DIM)
    ha, hi = A_ROPE_DIM // 2, IDX_ROPE_DIM // 2

    def row(*pieces):
        r = jnp.concatenate([jnp.asarray(p, F32) for p in pieces])
        return jnp.pad(r, (0, LANES - r.shape[0]))

    ones, zeros = (lambda n: jnp.ones((n,), F32)), (lambda n: jnp.zeros((n,), F32))
    return jnp.stack([row(inv_a, inv_a), row(ones(2 * ha)), row(-ones(ha)), row(zeros(ha), ones(ha)),
                      row(inv_i, inv_i), row(-ones(hi)), row(zeros(hi), ones(hi)), zeros(LANES)])


def _dsa_proj_body(sf_ref, h_ref, tab_ref, nq_ref, nkv_ref, nki_ref, wuq_ref, wuk_ref, wqi_ref,
                   qcat_ref, kcat_ref, kfull_ref, qidx_ref, kidx_ref, widx_ref):
    bi, i = pl.program_id(0), pl.program_id(1)
    tm = h_ref.shape[1]
    ha, hi = A_ROPE_DIM // 2, IDX_ROPE_DIM // 2
    pos = (sf_ref[bi] + i * tm + lax.broadcasted_iota(I32, (tm, LANES), 0)).astype(F32)
    tab = tab_ref[...]
    ang_a = pos * tab[0:1]
    cos_a = jnp.cos(ang_a) * tab[1:2]
    sin_a = jnp.sin(ang_a)
    sin_a1, sin_a2 = sin_a * tab[2:3], sin_a * tab[3:4]
    ang_i = pos * tab[4:5]
    cos_i = jnp.cos(ang_i)
    sin_i = jnp.sin(ang_i)
    sin_i1, sin_i2 = sin_i * tab[5:6], sin_i * tab[6:7]

    def rope_a(xb):
        return xb * cos_a + pltpu.roll(xb, LANES - ha, 1) * sin_a1 + pltpu.roll(xb, ha, 1) * sin_a2

    def rope_i(xb):
        return xb * cos_i + pltpu.roll(xb, LANES - hi, 1) * sin_i1 + pltpu.roll(xb, hi, 1) * sin_i2

    c_q = h_ref[0, :, 0:A_Q_RANK]
    c_kv = h_ref[0, :, A_Q_RANK:A_Q_RANK + A_KV_RANK]
    off = A_Q_RANK + A_KV_RANK
    k_rope = h_ref[0, :, off:off + LANES]
    k_idx = h_ref[0, :, off + LANES:off + 2 * LANES]
    w_idx = h_ref[0, :, off + 2 * LANES:off + 3 * LANES]

    cqn = _rms(c_q, nq_ref[...]).astype(BF16)
    ckv = _rms(c_kv, nkv_ref[...]).astype(BF16)
    kcat_ref[0] = ckv
    k_rot = rope_a(k_rope)
    for h in range(A_HEADS):
        kfull_ref[0, h] = (_dot(ckv, wuk_ref[h]) + k_rot).astype(BF16)

    ki = k_idx * lax.rsqrt(jnp.sum(k_idx * k_idx, axis=-1, keepdims=True) * (1.0 / IDX_DIM) + NORM_EPS)
    kidx_ref[0] = rope_i(ki * nki_ref[...]).astype(BF16)
    widx_ref[0] = w_idx * ((IDX_HEADS ** -0.5) * (IDX_DIM ** -0.5))

    q = _dot(cqn, wuq_ref[...])
    nope = 1.0 - tab[1:2]
    for h in range(A_HEADS):
        qh = q[:, h * A_HEAD_DIM:(h + 1) * A_HEAD_DIM]
        qcat_ref[0, h] = ((rope_a(qh) + qh * nope) * SOFTMAX_C).astype(BF16)
    qi = _dot(cqn, wqi_ref[...])
    for h in range(IDX_HEADS):
        qidx_ref[0, h] = rope_i(qi[:, h * LANES:(h + 1) * LANES]).astype(BF16)


def _dsa_proj(h_a, start_frame, tab, nq, nkv, nki, wuq, wuk_t, wqi, *, tm=256):
    b, s, _ = h_a.shape
    tm = min(tm, s)
    c3 = lambda bi, i, sf: (bi, i, 0)
    c4 = lambda bi, i, sf: (bi, 0, i, 0)
    z2 = lambda bi, i, sf: (0, 0)
    z3 = lambda bi, i, sf: (0, 0, 0)
    grid_spec = pltpu.PrefetchScalarGridSpec(
        num_scalar_prefetch=1,
        grid=(b, s // tm),
        in_specs=[pl.BlockSpec((1, tm, A_PAD_COLS), c3),
                  pl.BlockSpec(tab.shape, z2),
                  pl.BlockSpec(nq.shape, z2),
                  pl.BlockSpec(nkv.shape, z2),
                  pl.BlockSpec(nki.shape, z2),
                  pl.BlockSpec(wuq.shape, z2),
                  pl.BlockSpec(wuk_t.shape, z3),
                  pl.BlockSpec(wqi.shape, z2)],
        out_specs=[pl.BlockSpec((1, A_HEADS, tm, A_HEAD_DIM), c4),
                   pl.BlockSpec((1, tm, A_KV_RANK), c3),
                   pl.BlockSpec((1, A_HEADS, tm, A_HEAD_DIM), c4),
                   pl.BlockSpec((1, IDX_HEADS, tm, LANES), c4),
                   pl.BlockSpec((1, tm, LANES), c3),
                   pl.BlockSpec((1, tm, LANES), c3)],
    )
    return pl.pallas_call(
        _dsa_proj_body,
        grid_spec=grid_spec,
        out_shape=[jax.ShapeDtypeStruct((b, A_HEADS, s, A_HEAD_DIM), BF16),
                   jax.ShapeDtypeStruct((b, s, A_KV_RANK), BF16),
                   jax.ShapeDtypeStruct((b, A_HEADS, s, A_HEAD_DIM), BF16),
                   jax.ShapeDtypeStruct((b, IDX_HEADS, s, LANES), BF16),
                   jax.ShapeDtypeStruct((b, s, LANES), BF16),
                   jax.ShapeDtypeStruct((b, s, LANES), F32)],
        compiler_params=_cparams(("parallel", "parallel")),
        name="dsa_proj",
    )(start_frame, h_a, tab, nq, nkv, nki, wuq, wuk_t, wqi)


def _dsa_attend_body(qidx_ref, widx_ref, kidx_ref, qcat_ref, kcat_ref, kfull_ref, wuv_ref, o_ref,
                     key_ref, m_ref, l_ref, a_ref, acc_ref, t_ref, p_ref, *, tq, kt, topk):
    i = pl.program_id(1)
    t0 = i * tq
    nk = (t0 + tq + kt - 1) // kt
    krow = lax.broadcasted_iota(I32, (kt, tq), 0)
    qcol = lax.broadcasted_iota(I32, (kt, tq), 1)
    limit = ((t0 + qcol) // CHUNK + 1) * CHUNK

    wt = widx_ref[0]

    def score_tile(j, carry):
        off = pl.multiple_of(j * kt, kt)
        kid = kidx_ref[0, pl.ds(off, kt), :]
        sc = jnp.zeros((kt, tq), F32)
        for h in range(IDX_HEADS):
            sc = sc + wt[h:h + 1, :] * jnp.maximum(_dot_nt(kid, qidx_ref[0, h]), 0.0)
        bits = pltpu.bitcast(sc, I32)
        key = jnp.where(bits >= 0, bits, bits ^ 0x7FFFFFFF)
        key_ref[j] = jnp.where(off + krow < limit, key, INT_MIN)
        return carry

    lax.fori_loop(0, nk, score_tile, 0)

    def count_ge(cand):
        def body(j, acc):
            hit = jnp.where(key_ref[j] >= cand, 1.0, 0.0)
            return acc + jnp.sum(hit.reshape(kt // COUNT_ROWS, COUNT_ROWS, tq), axis=0)
        acc = lax.fori_loop(0, nk, body, jnp.zeros((COUNT_ROWS, tq), F32))
        return jnp.sum(acc, axis=0, keepdims=True)

    kf = float(topk)
    thr = jnp.where(count_ge(jnp.zeros((1, tq), I32)) >= kf, 0, INT_MIN).astype(I32)

    def bit_step(it, thr):
        cand = thr | jnp.left_shift(jnp.int32(1), 30 - it)
        return jnp.where(count_ge(cand) >= kf, cand, thr)

    thr = lax.fori_loop(0, 31, bit_step, thr)
    thr = jnp.maximum(thr, INT_MIN + 1)

    m_ref[...] = jnp.full(m_ref.shape, M_FLOOR, F32)
    l_ref[...] = jnp.zeros(l_ref.shape, F32)
    acc_ref[...] = jnp.zeros(acc_ref.shape, F32)
    eye_q = jnp.where(lax.broadcasted_iota(I32, (tq, tq), 0) == lax.broadcasted_iota(I32, (tq, tq), 1),
                      1.0, 0.0).astype(BF16)
    lane_tiles = [slice(c * LANES, (c + 1) * LANES) for c in range(kt // LANES)]
    acc_tiles = [slice(c * LANES, (c + 1) * LANES) for c in range(A_KV_RANK // LANES)]

    def attend_tile(j, carry):
        off = pl.multiple_of(j * kt, kt)
        ckv = kcat_ref[0, pl.ds(off, kt), :]
        bias_t = jnp.where(key_ref[j] >= thr, 0.0, NEG_BIG).astype(BF16)
        bias = _dot_nt(eye_q, bias_t)
        for h in range(A_HEADS):
            t_ref[h] = _dot_nt(qcat_ref[0, h], kfull_ref[0, h, pl.ds(off, kt), :]) + bias
        for h in range(A_HEADS):
            ts = [t_ref[h, :, lt] for lt in lane_tiles]
            m_old = m_ref[h]
            tmax = functools.reduce(jnp.maximum, ts)
            m_new = jnp.maximum(m_old, jnp.max(tmax, axis=-1, keepdims=True))
            ps = [jnp.exp2(t - m_new) for t in ts]
            alpha = jnp.exp2(m_old - m_new)
            l_ref[h] = alpha * l_ref[h] + jnp.sum(functools.reduce(jnp.add, ps), axis=-1, keepdims=True)
            for lt, p in zip(lane_tiles, ps):
                p_ref[h, :, lt] = p.astype(BF16)
            a_ref[h] = alpha
            m_ref[h] = m_new
        for h in range(A_HEADS):
            pv = _dot(p_ref[h], ckv)
            for at in acc_tiles:
                acc_ref[h, :, at] = a_ref[h] * acc_ref[h, :, at] + pv[:, at]
        return carry

    lax.fori_loop(0, nk, attend_tile, 0)

    for h in range(A_HEADS):
        inv_l = 1.0 / l_ref[h]
        o_lat = jnp.concatenate([acc_ref[h, :, at] * inv_l for at in acc_tiles], axis=1)
        o_ref[0, :, h * A_V_DIM:(h + 1) * A_V_DIM] = _dot(o_lat.astype(BF16), wuv_ref[h]).astype(o_ref.dtype)


def _dsa_attend(qidx, widx, kidx, qcat, kcat, kfull, wuv, *, tq=256, kt=512):
    b, _, s, _ = qidx.shape
    kt = min(kt, s)
    tq = min(tq, s)
    topk = min(TOPK_MAX, s // 4)
    assert s % kt == 0 and s % tq == 0 and tq % CHUNK == 0
    body = functools.partial(_dsa_attend_body, tq=tq, kt=kt, topk=topk)
    widx_t = jnp.transpose(widx[:, :, 0:IDX_HEADS], (0, 2, 1))
    return pl.pallas_call(
        body,
        grid=(b, s // tq),
        in_specs=[pl.BlockSpec((1, IDX_HEADS, tq, LANES), lambda bi, i: (bi, 0, i, 0)),
                  pl.BlockSpec((1, IDX_HEADS, tq), lambda bi, i: (bi, 0, i)),
                  pl.BlockSpec((1, s, LANES), lambda bi, i: (bi, 0, 0)),
                  pl.BlockSpec((1, A_HEADS, tq, A_HEAD_DIM), lambda bi, i: (bi, 0, i, 0)),
                  pl.BlockSpec((1, s, A_KV_RANK), lambda bi, i: (bi, 0, 0)),
                  pl.BlockSpec((1, A_HEADS, s, A_HEAD_DIM), lambda bi, i: (bi, 0, 0, 0)),
                  pl.BlockSpec(wuv.shape, lambda bi, i: (0, 0, 0))],
        out_specs=pl.BlockSpec((1, tq, A_WIDTH), lambda bi, i: (bi, i, 0)),
        out_shape=jax.ShapeDtypeStruct((b, s, A_WIDTH), BF16),
        scratch_shapes=[pltpu.VMEM((s // kt, kt, tq), I32),
                        pltpu.VMEM((A_HEADS, tq, LANES), F32),
                        pltpu.VMEM((A_HEADS, tq, LANES), F32),
                        pltpu.VMEM((A_HEADS, tq, LANES), F32),
                        pltpu.VMEM((A_HEADS, tq, A_KV_RANK), F32),
                        pltpu.VMEM((A_HEADS, tq, kt), F32),
                        pltpu.VMEM((A_HEADS, tq, kt), BF16)],
        compiler_params=_cparams(("parallel", "arbitrary")),
        name="dsa_attend",
    )(qidx, widx_t, kidx, qcat, kcat, kfull, wuv)


def _rwkv_prep_body(h_ref, prev_ref, mu_ref, w0_ref, a0_ref, w2_ref, a2_ref, g2_ref,
                    rkv_ref, ld_ref, a_ref, g_ref):
    i = pl.program_id(1)
    h = h_ref[0]
    tm = h.shape[0]
    prev = jnp.where(i > 0, prev_ref[0, 7:8, :], 0.0)
    row = lax.broadcasted_iota(I32, h.shape, 0)
    shifted = jnp.where(row == 0, prev, pltpu.roll(h, 1, 0))
    hm = h + (shifted - h) * mu_ref[...]
    w3 = 3 * B_WIDTH
    rkv_ref[0] = hm[:, 0:w3]
    wl = hm[:, w3:w3 + LANES]
    al = hm[:, w3 + LANES:w3 + 2 * LANES]
    gl = hm[:, w3 + 2 * LANES:w3 + 4 * LANES]
    wz = w0_ref[...] + _dot(jnp.tanh(wl).astype(BF16), w2_ref[...])
    log_w = -jax.nn.softplus(-wz) - 0.5
    ld_ref[0] = -jnp.exp(log_w)
    a_ref[0] = jax.nn.sigmoid(a0_ref[...] + _dot(al.astype(BF16), a2_ref[...]))
    g_ref[0] = _dot(jax.nn.sigmoid(gl).astype(BF16), g2_ref[...])


def _rwkv_prep(h_b, mu, w0, a0, w2, a2, g2, *, tm=256):
    b, s, cols = h_b.shape
    tm = min(tm, s)
    c3 = lambda bi, i: (bi, i, 0)
    z2 = lambda bi, i: (0, 0)
    prev_map = lambda bi, i: (bi, jnp.maximum(i * (tm // 8) - 1, 0), 0)
    out = lambda w: jax.ShapeDtypeStruct((b, s, w), F32)
    return pl.pallas_call(
        _rwkv_prep_body,
        grid=(b, s // tm),
        in_specs=[pl.BlockSpec((1, tm, cols), c3),
                  pl.BlockSpec((1, 8, cols), prev_map),
                  pl.BlockSpec(mu.shape, z2), pl.BlockSpec(w0.shape, z2), pl.BlockSpec(a0.shape, z2),
                  pl.BlockSpec(w2.shape, z2), pl.BlockSpec(a2.shape, z2), pl.BlockSpec(g2.shape, z2)],
        out_specs=[pl.BlockSpec((1, tm, 3 * B_WIDTH), c3), pl.BlockSpec((1, tm, B_WIDTH), c3),
                   pl.BlockSpec((1, tm, B_WIDTH), c3), pl.BlockSpec((1, tm, B_WIDTH), c3)],
        out_shape=[out(3 * B_WIDTH), out(B_WIDTH), out(B_WIDTH), out(B_WIDTH)],
        compiler_params=_cparams(("parallel", "parallel")),
        name="rwkv_prep",
    )(h_b, h_b, mu, w0, a0, w2, a2, g2)


RWKV_CHUNK = 64
RWKV_SUB = 16


def _bmm(a, b, kind="nn"):
    spec = {"nn": "bij,bjk->bik", "nt": "bik,bjk->bij"}[kind]
    return jnp.einsum(spec, a.astype(BF16), b.astype(BF16), preferred_element_type=F32)


def _unit_lower_inverse(l_mat, eye, same_blk):
    mm = _bmm
    ld = jnp.where(same_blk, l_mat, 0.0)
    lo = l_mat - ld
    x = eye + ld
    p = mm(ld, ld)
    x = x + mm(x, p)
    p = mm(p, p)
    x = x + mm(x, p)
    p = mm(p, p)
    d = x + mm(x, p)
    n = mm(d, lo)
    y = eye + n
    y = y + mm(y, mm(n, n))
    return mm(y, d)


def _rwkv_scan_body(r_ref, k_ref, v_ref, ld_ref, a_ref, g_ref, par_ref, o_ref, st_ref):
    c = RWKV_CHUNK
    tblk = r_ref.shape[1]
    nc = tblk // c
    nh = r_ref.shape[2] // B_HEAD

    @pl.when(pl.program_id(2) == 0)
    def _():
        st_ref[...] = jnp.zeros(st_ref.shape, F32)

    def heads(x):
        return jnp.concatenate([x[:, hd * B_HEAD:(hd + 1) * B_HEAD].reshape(nc, c, B_HEAD) for hd in range(nh)],
                               axis=0)

    def head_row(i):
        p = par_ref[i:i + 1, :]
        return jnp.concatenate([jnp.broadcast_to(p[:, hd * B_HEAD:(hd + 1) * B_HEAD][None], (nc, 1, B_HEAD))
                                for hd in range(nh)], axis=0)

    ri = lax.broadcasted_iota(I32, (c, c), 0)
    ci = lax.broadcasted_iota(I32, (c, c), 1)
    eye = jnp.where(ri == ci, 1.0, 0.0).astype(F32)[None]
    lower = (ri >= ci)[None]
    strict = (ri > ci)[None]
    same_blk = ((ri // RWKV_SUB) == (ci // RWKV_SUB))[None]

    rb = lax.broadcasted_iota(I32, (tblk, tblk), 0)
    cb = lax.broadcasted_iota(I32, (tblk, tblk), 1)
    tri = jnp.where((rb >= cb) & (rb // c == cb // c), 1.0, 0.0).astype(BF16)
    ld2 = ld_ref[0]
    l1 = ld2.astype(BF16)
    rem = ld2 - l1.astype(F32)
    l2 = rem.astype(BF16)
    l3 = (rem - l2.astype(F32)).astype(BF16)
    cs = heads(_dot(tri, l1) + (_dot(tri, l2) + _dot(tri, l3)))

    r, k, v = heads(r_ref[0]), heads(k_ref[0]), heads(v_ref[0])
    ld, a, g = heads(ld2), heads(a_ref[0]), heads(g_ref[0])
    kk = k * head_row(0)
    kk = kk * lax.rsqrt(jnp.maximum(jnp.sum(kk * kk, axis=-1, keepdims=True), 1e-24))
    km = k * (1.0 + (a - 1.0) * head_row(1))
    bv = kk * a
    cs_end = cs[:, c - 1:c, :]
    e_neg = jnp.exp(-cs)
    e_end = jnp.exp(cs_end - cs)
    at = -kk * jnp.exp(cs - ld)
    rt = r * jnp.exp(cs)
    bt, kt = bv * e_neg, km * e_neg
    bd, kd = bv * e_end, km * e_end
    p_end_t = jnp.swapaxes(jnp.broadcast_to(jnp.exp(cs_end), cs.shape), 1, 2)

    lab = jnp.where(strict, _bmm(at, bt, "nt"), 0.0)
    lak = jnp.where(strict, _bmm(at, kt, "nt"), 0.0)
    lrb = jnp.where(lower, _bmm(rt, bt, "nt"), 0.0)
    lrk = jnp.where(lower, _bmm(rt, kt, "nt"), 0.0)
    tinv = _unit_lower_inverse(lab, eye, same_blk)
    tat = _bmm(tinv, at)
    c1 = _bmm(tinv, _bmm(lak, v))
    r2 = (rt + _bmm(lrb, tat)).astype(BF16)
    c2 = _bmm(lrk, v) + _bmm(lrb, c1)
    bd_t, kd_t = jnp.swapaxes(bd, 1, 2), jnp.swapaxes(kd, 1, 2)
    gmat_t = _bmm(bd_t, tat).astype(BF16)
    c3_t = _bmm(bd_t, c1) + _bmm(kd_t, v)

    outs = [None] * (nh * nc)
    sts = [st_ref[hd] for hd in range(nh)]
    for ch in range(nc):
        for hd in range(nh):
            i = hd * nc + ch
            sb = sts[hd].astype(BF16)
            outs[i] = _dot(r2[i], sb) + c2[i]
            sts[hd] = sts[hd] * p_end_t[i] + (_dot(gmat_t[i], sb) + c3_t[i])
    for hd in range(nh):
        st_ref[hd] = sts[hd]
    out = jnp.stack(outs, axis=0)

    mean = jnp.mean(out, axis=-1, keepdims=True)
    var = jnp.mean(jnp.square(out - mean), axis=-1, keepdims=True)
    y = (out - mean) * lax.rsqrt(var + B_LN_EPS) * head_row(3) + head_row(4)
    y = (y + jnp.sum(r * km * head_row(2), axis=-1, keepdims=True) * v) * g
    for hd in range(nh):
        o_ref[0, :, hd * B_HEAD:(hd + 1) * B_HEAD] = y[hd * nc:(hd + 1) * nc].reshape(tblk, B_HEAD).astype(o_ref.dtype)


def _rwkv_scan(rkv, ld, a, g, par, *, tblk=512, hw=2 * LANES):
    b, s, _ = ld.shape
    tblk = min(tblk, s)
    npair = B_WIDTH // hw
    blk = (1, tblk, hw)
    return pl.pallas_call(
        _rwkv_scan_body,
        grid=(b, npair, s // tblk),
        in_specs=[pl.BlockSpec(blk, lambda bi, hp, t: (bi, t, hp)),
                  pl.BlockSpec(blk, lambda bi, hp, t: (bi, t, npair + hp)),
                  pl.BlockSpec(blk, lambda bi, hp, t: (bi, t, 2 * npair + hp)),
                  pl.BlockSpec(blk, lambda bi, hp, t: (bi, t, hp)),
                  pl.BlockSpec(blk, lambda bi, hp, t: (bi, t, hp)),
                  pl.BlockSpec(blk, lambda bi, hp, t: (bi, t, hp)),
                  pl.BlockSpec((8, hw), lambda bi, hp, t: (0, hp))],
        out_specs=pl.BlockSpec(blk, lambda bi, hp, t: (bi, t, hp)),
        out_shape=jax.ShapeDtypeStruct((b, s, B_WIDTH), BF16),
        scratch_shapes=[pltpu.VMEM((hw // B_HEAD, B_HEAD, B_HEAD), F32)],
        compiler_params=_cparams(("parallel", "parallel", "arbitrary")),
        name="rwkv_scan",
    )(rkv, rkv, rkv, ld, a, g, par)


S5_CHUNK = LANES
S5_ROWS = S5_CHUNK * C_GROUP
S5_PASSES = 3


def _s5_group_body(lam_c_ref, lam_r_ref, bt_ref, c_ref, ct_ref, u_ref, y_ref,
                   kv_ref, m_ref, w_ref, v_ref, acc_ref, us_ref):
    L = S5_CHUNK
    n_chunk, bsz = u_ref.shape[0], u_ref.shape[1]
    rows = n_chunk * bsz

    def zoh(lre, lim, dt):
        lre = jnp.minimum(lre, -1e-4)
        rho, th = lre * dt, lim * dt
        mag = jnp.exp(rho)
        lbr, lbi = mag * jnp.cos(th), mag * jnp.sin(th)
        den = 1.0 / (lre * lre + lim * lim)
        cr = ((lbr - 1.0) * lre + lbi * lim) * den
        cim = (lbi * lre - (lbr - 1.0) * lim) * den
        return rho, th, cr, cim

    lc = lam_c_ref[0]
    rho_c, th_c, _, _ = zoh(lc[:, 0:1], lc[:, 1:2], jnp.exp(lc[:, 2:3]))
    lr = lam_r_ref[0]
    rho_r, th_r, cr_r, ci_r = zoh(lr[0:1], lr[1:2], jnp.exp(lr[2:3]))

    def powers(rho, th, n):
        mag = jnp.exp(rho * n)
        return mag * jnp.cos(th * n), mag * jnp.sin(th * n)

    btr, bti = bt_ref[0, 0], bt_ref[0, 1]
    bbr = btr * cr_r - bti * ci_r
    bbi = btr * ci_r + bti * cr_r
    c_re, c_im = c_ref[0, 0], c_ref[0, 1]

    d_re = jnp.concatenate([c_re * bbr[j:j + 1] - c_im * bbi[j:j + 1] for j in range(C_GROUP)], axis=0)
    d_im = jnp.concatenate([c_re * bbi[j:j + 1] + c_im * bbr[j:j + 1] for j in range(C_GROUP)], axis=0)
    tau = lax.broadcasted_iota(I32, (C_STATE, L), 1).astype(F32)
    e_re, e_im = powers(rho_c, th_c, tau)
    kvec = _dotp(d_re, e_re, S5_PASSES) - _dotp(d_im, e_im, S5_PASSES)

    kv_ref[...] = kvec

    back = (L - 1 - lax.broadcasted_iota(I32, (L, C_STATE), 0)).astype(F32)
    f_re, f_im = powers(rho_r, th_r, back)
    for j in range(C_GROUP):
        br, bi = bbr[j:j + 1], bbi[j:j + 1]
        w_ref[j * L:(j + 1) * L, :] = jnp.concatenate(
            [f_re * br - f_im * bi, f_re * bi + f_im * br], axis=1).astype(w_ref.dtype)

    e1_re, e1_im = powers(rho_c, th_c, tau + 1.0)
    ctr, cti = ct_ref[0, 0], ct_ref[0, 1]
    for i in range(C_GROUP):
        cr_i, ci_i = ctr[:, i:i + 1], cti[:, i:i + 1]
        v_ref[0:C_STATE, i * L:(i + 1) * L] = (cr_i * e1_re - ci_i * e1_im).astype(v_ref.dtype)
        v_ref[C_STATE:2 * C_STATE, i * L:(i + 1) * L] = (-(cr_i * e1_im + ci_i * e1_re)).astype(v_ref.dtype)

    for j in range(C_GROUP):
        us_ref[j] = u_ref[:, :, j, :].reshape(rows, L)
    u = jnp.concatenate([us_ref[j] for j in range(C_GROUP)], axis=1)
    x = _dot(u, w_ref[...])

    def shift_chunks(a, d):
        return jnp.concatenate([jnp.zeros((d * bsz, 2 * C_STATE), F32), a[:rows - d * bsz]], axis=0)

    d = 1
    while d < n_chunk:
        pr, pi = powers(rho_r, th_r, float(L * d))
        prev = shift_chunks(x, d)
        x = x + (prev * jnp.concatenate([pr, pr], axis=1)
                 + pltpu.roll(prev, C_STATE, 1) * jnp.concatenate([-pi, pi], axis=1))
        d *= 2
    x_in = shift_chunks(x, 1)
    acc_ref[...] = _dot(x_in.astype(BF16), v_ref[...])

    upper = lax.broadcasted_iota(I32, (L, L), 1) >= lax.broadcasted_iota(I32, (L, L), 0)

    def block_rows(jj, carry):
        for half in range(2):
            j = 2 * jj + half
            for i in range(C_GROUP):
                kv = jnp.broadcast_to(kv_ref[pl.ds(j * C_GROUP + i, 1), :], (L, L))
                blk = jnp.where(upper, pltpu.roll(kv, 0, 1, stride=1, stride_axis=0), 0.0)
                m_ref[half * L:(half + 1) * L, i * L:(i + 1) * L] = blk.astype(m_ref.dtype)
        uj = jnp.concatenate([us_ref[2 * jj], us_ref[2 * jj + 1]], axis=1)
        acc_ref[...] += _dot(uj, m_ref[...])
        return carry

    lax.fori_loop(0, C_GROUP // 2, block_rows, 0)
    for i in range(C_GROUP):
        y_ref[:, :, i, :] = acc_ref[:, i * L:(i + 1) * L].reshape(n_chunk, bsz, L)


def _s5_groups(u4, lam_c, lam_r, bt, c, ct):
    n_chunk, bsz, width, _ = u4.shape
    g = width // C_GROUP
    rows = n_chunk * bsz
    i3 = lambda gi: (gi, 0, 0)
    i4 = lambda gi: (gi, 0, 0, 0)
    blk = pl.BlockSpec((n_chunk, bsz, C_GROUP, S5_CHUNK), lambda gi: (0, 0, gi, 0))
    return pl.pallas_call(
        _s5_group_body,
        grid=(g,),
        in_specs=[pl.BlockSpec((1, C_STATE, 8), i3),
                  pl.BlockSpec((1, 8, C_STATE), i3),
                  pl.BlockSpec((1, 2, C_GROUP, C_STATE), i4),
                  pl.BlockSpec((1, 2, C_GROUP, C_STATE), i4),
                  pl.BlockSpec((1, 2, C_STATE, C_GROUP), i4),
                  blk],
        out_specs=blk,
        out_shape=jax.ShapeDtypeStruct(u4.shape, F32),
        scratch_shapes=[pltpu.VMEM((C_GROUP * C_GROUP, S5_CHUNK), F32),
                        pltpu.VMEM((2 * S5_CHUNK, S5_ROWS), BF16),
                        pltpu.VMEM((S5_ROWS, 2 * C_STATE), BF16),
                        pltpu.VMEM((2 * C_STATE, S5_ROWS), BF16),
                        pltpu.VMEM((rows, S5_ROWS), F32),
                        pltpu.VMEM((C_GROUP, rows, S5_CHUNK), BF16)],
        compiler_params=_cparams(("parallel",)),
        name="s5_groups",
    )(lam_c, lam_r, bt, c, ct, u4)


def _s5_in_body(x_ref, g_ref, w_ref, u_ref, u4_ref):
    xs = _rms(x_ref[...], g_ref[...]).astype(BF16)
    u = _dot(xs, w_ref[...])
    u_ref[...] = u
    for k in range(u4_ref.shape[0]):
        u4_ref[k, 0] = u[k * S5_CHUNK:(k + 1) * S5_CHUNK, :].T.astype(BF16)


def _s5_in(x2, g, w, bsz, seq, *, tm=512):
    m, d = x2.shape
    n = w.shape[1]
    tm = min(tm, seq)
    tiles_per_seq = seq // tm
    ck = tm // S5_CHUNK
    return pl.pallas_call(
        _s5_in_body,
        grid=(m // tm,),
        in_specs=[pl.BlockSpec((tm, d), lambda i: (i, 0)),
                  pl.BlockSpec((1, d), lambda i: (0, 0)),
                  pl.BlockSpec((d, n), lambda i: (0, 0))],
        out_specs=[pl.BlockSpec((tm, n), lambda i: (i, 0)),
                   pl.BlockSpec((ck, 1, n, S5_CHUNK), lambda i: (i % tiles_per_seq, i // tiles_per_seq, 0, 0))],
        out_shape=[jax.ShapeDtypeStruct((m, n), F32),
                   jax.ShapeDtypeStruct((seq // S5_CHUNK, bsz, n, S5_CHUNK), BF16)],
        compiler_params=_cparams(("parallel",)),
        name="odd_in",
    )(x2, g, w)


def _pad_cols(w, width):
    return jnp.pad(w, ((0, 0), (0, width - w.shape[1])))


def _pad_rows(w, height):
    return jnp.pad(w, ((0, height - w.shape[0]), (0, 0)))


def _even_in_weights(w_in):
    o = 0
    pieces = {}
    for name, width in (("cq", A_Q_RANK), ("ckv", A_KV_RANK), ("krope", A_ROPE_DIM), ("kidx", IDX_DIM),
                        ("widx", IDX_HEADS), ("rkv", 3 * B_WIDTH), ("wl", B_DECAY_LORA),
                        ("al", B_A_LORA), ("gl", B_GATE_LORA)):
        pieces[name] = w_in[:, o:o + width]
        o += width
    w_a = jnp.concatenate([pieces["cq"], pieces["ckv"], _pad_cols(pieces["krope"], LANES),
                           _pad_cols(pieces["kidx"], LANES), _pad_cols(pieces["widx"], LANES)], axis=1)
    w_b = jnp.concatenate([pieces["rkv"], _pad_cols(pieces["wl"], LANES), _pad_cols(pieces["al"], LANES),
                           _pad_cols(pieces["gl"], 2 * LANES)], axis=1)
    return w_a.astype(BF16), w_b.astype(BF16)


def _pad_mu(mu):
    o = 3 * B_WIDTH
    wl = mu[o:o + B_DECAY_LORA]
    al = mu[o + B_DECAY_LORA:o + B_DECAY_LORA + B_A_LORA]
    gl = mu[o + B_DECAY_LORA + B_A_LORA:]
    z = lambda n: jnp.zeros((n,), F32)
    return jnp.concatenate([mu[:o], wl, z(LANES - B_DECAY_LORA), al, z(LANES - B_A_LORA),
                            gl, z(2 * LANES - B_GATE_LORA)])[None, :]


def _dsa_mixer(h_a3, start_frame, cq_norm, ckv_norm, kidx_norm, w_uq, w_uk, w_uv, w_qidx):
    wuq = w_uq.reshape(A_Q_RANK, A_HEADS * A_HEAD_DIM).astype(BF16)
    wuk_h = jnp.pad(jnp.transpose(w_uk, (1, 0, 2)), ((0, 0), (0, 0), (A_ROPE_DIM, 0))).astype(BF16)
    wqi = jnp.pad(w_qidx, ((0, 0), (0, 0), (0, LANES - IDX_DIM))).reshape(A_Q_RANK, IDX_HEADS * LANES).astype(BF16)
    wuv = jnp.transpose(w_uv, (1, 0, 2)).astype(BF16)
    qcat, kcat, kfull, qidx, kidx, widx = _dsa_proj(
        h_a3, start_frame, _rope_tables(), cq_norm[None, :], ckv_norm[None, :],
        _pad_cols(kidx_norm[None, :], LANES), wuq, wuk_h, wqi)
    return _dsa_attend(qidx, widx, kidx, qcat, kcat, kfull, wuv)


def _rwkv_mixer(h_b3, mu, w0, w2, a0, a2, g2, k_k, k_a, r_k, ln_w, ln_b):
    rkv, ld, a, g = _rwkv_prep(h_b3, _pad_mu(mu), w0[None, :], a0[None, :],
                               _pad_rows(w2, LANES).astype(BF16), _pad_rows(a2, LANES).astype(BF16),
                               _pad_rows(g2, 2 * LANES).astype(BF16))
    par = jnp.concatenate([k_k[None, :], k_a[None, :], r_k.reshape(1, B_WIDTH), ln_w[None, :], ln_b[None, :],
                           jnp.zeros((3, B_WIDTH), F32)], axis=0)
    return _rwkv_scan(rkv, ld, a, g, par)


def _even_mixer(x2, bsz, seq, start_frame, g_mix, w_in, w_out, cq_norm, ckv_norm, kidx_norm, w_uq, w_uk, w_uv,
                w_qidx, mu, w0, w2, a0, a2, g2, k_k, k_a, r_k, ln_w, ln_b):
    w_a, w_b = _even_in_weights(w_in)
    rms_pro = lambda x, g: _rms(x, g)
    h_a = _matmul([x2], [g_mix], w_a, [], prologue=rms_pro, epilogue=_epi_id, out_dtype=F32,
                  tm=512, tn=A_PAD_COLS, name="even_in_a")
    h_b = _matmul([x2], [g_mix], w_b, [], prologue=rms_pro, epilogue=_epi_id, out_dtype=F32,
                  tm=512, tn=B_PAD_COLS // 2, name="even_in_b")
    y_a = _dsa_mixer(h_a.reshape(bsz, seq, A_PAD_COLS), start_frame, cq_norm, ckv_norm, kidx_norm,
                     w_uq, w_uk, w_uv, w_qidx)
    y_b = _rwkv_mixer(h_b.reshape(bsz, seq, B_PAD_COLS), mu, w0, w2, a0, a2, g2, k_k, k_a, r_k, ln_w, ln_b)

    m = bsz * seq
    cat_pro = lambda ya, yb: jnp.concatenate([ya, yb], axis=-1)
    return _matmul([y_a.reshape(m, A_WIDTH), y_b.reshape(m, B_WIDTH)], [], w_out.astype(BF16), [(x2, "tile")],
                   prologue=cat_pro, epilogue=_epi_add, out_dtype=F32, tm=512, tn=D_MODEL, name="even_out")


def _gelu_tanh(y):
    return 0.5 * y * (1.0 + jnp.tanh(math.sqrt(2.0 / math.pi) * (y + 0.044715 * (y * y * y))))


def _odd_mixer(x2, bsz, seq, g_mix, w_in, w_out, lam_re, lam_im, log_dt, b_re, b_im, c_re, c_im, d_skip,
               w_glu, b_glu):
    u, u4 = _s5_in(x2, g_mix, w_in.astype(BF16), bsz, seq)
    gated = _s5_mixer(u, u4, lam_re, lam_im, log_dt, b_re, b_im, c_re, c_im, d_skip, w_glu, b_glu)
    return _matmul([gated], [], w_out.astype(BF16), [(x2, "tile")], prologue=_pro_id, epilogue=_epi_add,
                   out_dtype=F32, tm=512, tn=D_MODEL, name="odd_out")


def _s5_mixer(u, u4, lam_re, lam_im, log_dt, b_re, b_im, c_re, c_im, d_skip, w_glu, b_glu):
    n_chunk, bsz = u4.shape[0], u4.shape[1]
    m = u.shape[0]
    assert bsz % 8 == 0, "chunk rows are addressed as whole sublane tiles"

    ldt = jnp.broadcast_to(log_dt[:, None], lam_re.shape)
    lam_r = jnp.pad(jnp.stack([lam_re, lam_im, ldt], axis=1), ((0, 0), (0, 5), (0, 0)))
    lam_c = jnp.transpose(lam_r, (0, 2, 1))
    bt = jnp.stack([jnp.transpose(b_re, (0, 2, 1)), jnp.transpose(b_im, (0, 2, 1))], axis=1)
    c = jnp.stack([c_re, c_im], axis=1)
    ct = jnp.stack([jnp.transpose(c_re, (0, 2, 1)), jnp.transpose(c_im, (0, 2, 1))], axis=1)
    y4 = _s5_groups(u4, lam_c, lam_r, bt, c, ct)
    y = jnp.transpose(y4, (1, 0, 3, 2)).reshape(m, C_WIDTH)

    z_of = lambda yy, uu, dd: _gelu_tanh(yy + dd * uu)
    return _matmul([y, u], [d_skip[None, :]], w_glu.astype(BF16),
                   [(None, 0), (None, 1), (d_skip[None, :], "row"), (b_glu[None, :], "row")],
                   prologue=z_of,
                   epilogue=lambda acc, yy, uu, dd, bb: z_of(yy, uu, dd) * jax.nn.sigmoid(acc + bb),
                   out_dtype=BF16, tm=512, tn=C_WIDTH, name="s5_glu")


def kernel(x, mem, start_frame, norm_mix, norm_xattn, norm_mem, norm_ffn, final_norm, xattn_wq, xattn_wkv, xattn_wo, ffn_up, ffn_down, even_w_in, even_w_out, dsa_cq_norm, dsa_ckv_norm, dsa_kidx_norm, dsa_w_uq, dsa_w_uk, dsa_w_uv, dsa_w_qidx, rwkv_mu, rwkv_w0, rwkv_w2, rwkv_a0, rwkv_a2, rwkv_g2, rwkv_k_k, rwkv_k_a, rwkv_r_k, rwkv_ln_w, rwkv_ln_b, odd_w_in, odd_w_out, s5_lam_re, s5_lam_im, s5_log_dt, s5_b_re, s5_b_im, s5_c_re, s5_c_im, s5_d, s5_w_glu, s5_b_glu):
    bsz, seq, d = x.shape
    depth = norm_mix.shape[0]
    m = bsz * seq
    x2 = x.reshape(m, d)
    mem2 = mem.reshape(bsz * mem.shape[1], d)
    for layer in range(depth):
        i = layer // 2
        g_mix = norm_mix[layer][None, :]
        if layer % 2 == 0:
            x2 = _even_mixer(x2, bsz, seq, start_frame, g_mix, even_w_in[i], even_w_out[i], dsa_cq_norm[i],
                             dsa_ckv_norm[i], dsa_kidx_norm[i], dsa_w_uq[i], dsa_w_uk[i], dsa_w_uv[i],
                             dsa_w_qidx[i], rwkv_mu[i], rwkv_w0[i], rwkv_w2[i], rwkv_a0[i], rwkv_a2[i],
                             rwkv_g2[i], rwkv_k_k[i], rwkv_k_a[i], rwkv_r_k[i], rwkv_ln_w[i], rwkv_ln_b[i])
        else:
            x2 = _odd_mixer(x2, bsz, seq, g_mix, odd_w_in[i], odd_w_out[i], s5_lam_re[i], s5_lam_im[i],
                            s5_log_dt[i], s5_b_re[i], s5_b_im[i], s5_c_re[i], s5_c_im[i], s5_d[i],
                            s5_w_glu[i], s5_b_glu[i])
        kv = _matmul([mem2], [norm_mem[layer][None, :]], xattn_wkv[layer].astype(BF16), [],
                     prologue=lambda a, g: _rms(a, g), epilogue=_epi_id, out_dtype=BF16, tm=512, tn=1024,
                     name="xattn_kv")
        x2 = _xattn(x2.reshape(bsz, seq, d), norm_xattn[layer][None, :], xattn_wq[layer].astype(BF16),
                    kv.reshape(bsz, mem.shape[1], -1), xattn_wo[layer].astype(BF16)).reshape(m, d)
        x2 = _ffn(x2, norm_ffn[layer][None, :], ffn_up[layer].astype(BF16), ffn_down[layer].astype(BF16),
                  final_norm[None, :] if layer == depth - 1 else None)
    return x2.reshape(bsz, seq, d)
```

```python
import functools
import math

import numpy as np
import jax
import jax.numpy as jnp
from jax import lax
from jax.experimental import pallas as pl
from jax.experimental.pallas import tpu as pltpu

F32 = jnp.float32
BF16 = jnp.bfloat16
I32 = jnp.int32

D_MODEL = 2048
CHUNK = 64
ROPE_THETA = 500000.0
NORM_EPS = 1e-5

A_HEADS = 8
A_HEAD_DIM = 128
A_ROPE_DIM = 32
A_V_DIM = 128
A_WIDTH = A_HEADS * A_V_DIM
A_Q_RANK = 512
A_KV_RANK = 256
IDX_HEADS = 16
IDX_DIM = 64
IDX_ROPE_DIM = 16
TOPK_MAX = 256

B_HEAD = 64
B_WIDTH = 1024
B_HEADS = 16
B_DECAY_LORA = 64
B_A_LORA = 64
B_GATE_LORA = 160
B_LN_EPS = 64e-5

C_WIDTH = 2048
C_GROUP = 16
C_GROUPS = 128
C_STATE = 64

X_HEADS = 4
X_HEAD_DIM = 128
FFN_DIM = 4 * D_MODEL

LANES = 128
VMEM_LIMIT = 56 * 1024 * 1024

A_PAD_COLS = A_Q_RANK + A_KV_RANK + 3 * LANES
B_LORA_COLS = 2 * LANES + 2 * LANES
B_PAD_COLS = 3 * B_WIDTH + B_LORA_COLS

INT_MIN = -2147483648
NEG_BIG = -1e30
M_FLOOR = -1e20
SOFTMAX_C = (A_HEAD_DIM ** -0.5) * math.log2(math.e)
COUNT_ROWS = 32


def _cparams(sem):
    return pltpu.CompilerParams(dimension_semantics=sem, vmem_limit_bytes=VMEM_LIMIT)


def _rms(x, g):
    return x * lax.rsqrt(jnp.mean(x * x, axis=-1, keepdims=True) + NORM_EPS) * g


def _dot(a, b):
    return jnp.dot(a, b, preferred_element_type=F32)


def _dot_nt(a, b):
    return lax.dot_general(a, b, (((1,), (1,)), ((), ())), preferred_element_type=F32)


def _dot_tn(a, b):
    return lax.dot_general(a, b, (((0,), (0,)), ((), ())), preferred_element_type=F32)


def _split2(x):
    hi = x.astype(BF16)
    lo = (x - hi.astype(F32)).astype(BF16)
    return hi, lo


def _dotp(a, b, passes, kind="nn"):
    f = {"nn": _dot, "nt": _dot_nt, "tn": _dot_tn}[kind]
    if passes == 1:
        return f(a.astype(BF16), b.astype(BF16))
    ah, al = _split2(a)
    bh, bl = _split2(b)
    return f(ah, bh) + (f(ah, bl) + f(al, bh))


def _mm_body(*refs, n_row, n_const, epi_kinds, single_n, prologue, epilogue):
    rows = refs[:n_row]
    consts = refs[n_row:n_row + n_const]
    w_ref = refs[n_row + n_const]
    epi_refs = iter(refs[n_row + n_const + 1:])
    epis = [rows[kind] if isinstance(kind, int) else next(epi_refs) for kind in epi_kinds]
    if single_n:
        o_ref = refs[-1]
        xs = prologue(*[r[...] for r in rows], *[c[...] for c in consts]).astype(BF16)
    else:
        o_ref, xs_ref = refs[-2], refs[-1]

        @pl.when(pl.program_id(1) == 0)
        def _():
            xs_ref[...] = prologue(*[r[...] for r in rows], *[c[...] for c in consts]).astype(BF16)

        xs = xs_ref[...]
    acc = _dot(xs, w_ref[...])
    o_ref[...] = epilogue(acc, *[e[...] for e in epis]).astype(o_ref.dtype)


def _matmul(rows, consts, w, epis, *, prologue, epilogue, out_dtype, tm, tn, name):
    m = rows[0].shape[0]
    k, n = w.shape
    tm = min(tm, m)
    tn = min(tn, n)
    assert m % tm == 0 and n % tn == 0, (m, tm, n, tn)
    single_n = tn == n
    in_specs = [pl.BlockSpec((tm, r.shape[1]), lambda i, j: (i, 0)) for r in rows]
    in_specs += [pl.BlockSpec(c.shape, lambda i, j: (0, 0)) for c in consts]
    in_specs += [pl.BlockSpec((k, tn), lambda i, j: (0, j))]
    for _, kind in epis:
        if kind == "tile":
            in_specs.append(pl.BlockSpec((tm, tn), lambda i, j: (i, j)))
        elif kind == "row":
            in_specs.append(pl.BlockSpec((1, tn), lambda i, j: (0, j)))
        else:
            assert single_n and rows[kind].shape[1] == n
    body = functools.partial(_mm_body, n_row=len(rows), n_const=len(consts),
                             epi_kinds=tuple(kind for _, kind in epis), single_n=single_n,
                             prologue=prologue, epilogue=epilogue)
    return pl.pallas_call(
        body,
        grid=(m // tm, n // tn),
        in_specs=in_specs,
        out_specs=pl.BlockSpec((tm, tn), lambda i, j: (i, j)),
        out_shape=jax.ShapeDtypeStruct((m, n), out_dtype),
        scratch_shapes=[] if single_n else [pltpu.VMEM((tm, k), BF16)],
        compiler_params=_cparams(("parallel", "arbitrary")),
        name=name,
    )(*rows, *consts, w, *[e for e, kind in epis if not isinstance(kind, int)])


def _epi_id(acc):
    return acc


def _epi_add(acc, res):
    return acc + res


def _pro_id(x):
    return x


def _ffn_body(x_ref, g_ref, up_ref, dn_ref, gout_ref, o_ref, xs_ref, *, norm_out):
    j = pl.program_id(1)

    @pl.when(j == 0)
    def _():
        x = x_ref[...]
        xs_ref[...] = _rms(x, g_ref[...]).astype(BF16)
        o_ref[...] = x

    h = _dot(xs_ref[...], up_ref[...])
    h = jnp.square(jnp.maximum(h, 0.0)).astype(BF16)
    o_ref[...] += _dot(h, dn_ref[...])

    if norm_out:
        @pl.when(j == pl.num_programs(1) - 1)
        def _():
            o_ref[...] = _rms(o_ref[...], gout_ref[...])


def _ffn(x, g, w_up, w_dn, g_out=None, *, tm=1024, tf=512):
    m, d = x.shape
    f = w_up.shape[1]
    tm = min(tm, m)
    body = functools.partial(_ffn_body, norm_out=g_out is not None)
    return pl.pallas_call(
        body,
        grid=(m // tm, f // tf),
        in_specs=[pl.BlockSpec((tm, d), lambda i, j: (i, 0)),
                  pl.BlockSpec((1, d), lambda i, j: (0, 0)),
                  pl.BlockSpec((d, tf), lambda i, j: (0, j)),
                  pl.BlockSpec((tf, d), lambda i, j: (j, 0)),
                  pl.BlockSpec((1, d), lambda i, j: (0, 0))],
        out_specs=pl.BlockSpec((tm, d), lambda i, j: (i, 0)),
        out_shape=jax.ShapeDtypeStruct((m, d), F32),
        scratch_shapes=[pltpu.VMEM((tm, d), BF16)],
        compiler_params=_cparams(("parallel", "arbitrary")),
        name="ffn",
    )(x, g, w_up, w_dn, g if g_out is None else g_out)


def _xattn_body(x_ref, g_ref, wq_ref, kv_ref, wo_ref, o_ref):
    x = x_ref[0]
    xn = _rms(x, g_ref[...]).astype(BF16)
    q = _dot(xn, wq_ref[...])
    width = X_HEADS * X_HEAD_DIM
    outs = []
    for h in range(X_HEADS):
        sl = slice(h * X_HEAD_DIM, (h + 1) * X_HEAD_DIM)
        qh = q[:, sl].astype(BF16)
        kh = kv_ref[0, :, sl]
        vh = kv_ref[0, :, width + h * X_HEAD_DIM:width + (h + 1) * X_HEAD_DIM]
        s = _dot_nt(qh, kh) * (X_HEAD_DIM ** -0.5)
        s = s - jnp.max(s, axis=-1, keepdims=True)
        p = jnp.exp(s)
        p = p / jnp.sum(p, axis=-1, keepdims=True)
        outs.append(_dot(p.astype(BF16), vh))
    o = jnp.concatenate(outs, axis=-1).astype(BF16)
    o_ref[0] = x + _dot(o, wo_ref[...])


def _xattn(x3, g, wq, kv, wo, *, tm=512):
    b, s, d = x3.shape
    tm = min(tm, s)
    mlen = kv.shape[1]
    width = X_HEADS * X_HEAD_DIM
    return pl.pallas_call(
        _xattn_body,
        grid=(b, s // tm),
        in_specs=[pl.BlockSpec((1, tm, d), lambda bi, i: (bi, i, 0)),
                  pl.BlockSpec((1, d), lambda bi, i: (0, 0)),
                  pl.BlockSpec((d, width), lambda bi, i: (0, 0)),
                  pl.BlockSpec((1, mlen, 2 * width), lambda bi, i: (bi, 0, 0)),
                  pl.BlockSpec((width, d), lambda bi, i: (0, 0))],
        out_specs=pl.BlockSpec((1, tm, d), lambda bi, i: (bi, i, 0)),
        out_shape=jax.ShapeDtypeStruct((b, s, d), F32),
        compiler_params=_cparams(("parallel", "parallel")),
        name="xattn",
    )(x3, g, wq, kv, wo)


def _rope_tables():
    inv_a = ROPE_THETA ** (-jnp.arange(0, A_ROPE_DIM, 2, dtype=F32) / A_ROPE_DIM)
    inv_i = ROPE_THETA ** (-jnp.arange(0, IDX_ROPE_DIM, 2, dtype=F32) / IDX_ROPE_DIM)
    ha, hi = A_ROPE_DIM // 2, IDX_ROPE_DIM // 2

    def row(*pieces):
        r = jnp.concatenate([jnp.asarray(p, F32) for p in pieces])
        return jnp.pad(r, (0, LANES - r.shape[0]))

    ones, zeros = (lambda n: jnp.ones((n,), F32)), (lambda n: jnp.zeros((n,), F32))
    return jnp.stack([row(inv_a, inv_a), row(ones(2 * ha)), row(-ones(ha)), row(zeros(ha), ones(ha)),
                      row(inv_i, inv_i), row(-ones(hi)), row(zeros(hi), ones(hi)), zeros(LANES)])


def _dsa_proj_body(sf_ref, h_ref, tab_ref, nq_ref, nkv_ref, nki_ref, wuq_ref, wuk_ref, wqi_ref,
                   qcat_ref, kcat_ref, kfull_ref, qidx_ref, kidx_ref, widx_ref):
    bi, i = pl.program_id(0), pl.program_id(1)
    tm = h_ref.shape[1]
    ha, hi = A_ROPE_DIM // 2, IDX_ROPE_DIM // 2
    pos = (sf_ref[bi] + i * tm + lax.broadcasted_iota(I32, (tm, LANES), 0)).astype(F32)
    tab = tab_ref[...]
    ang_a = pos * tab[0:1]
    cos_a = jnp.cos(ang_a) * tab[1:2]
    sin_a = jnp.sin(ang_a)
    sin_a1, sin_a2 = sin_a * tab[2:3], sin_a * tab[3:4]
    ang_i = pos * tab[4:5]
    cos_i = jnp.cos(ang_i)
    sin_i = jnp.sin(ang_i)
    sin_i1, sin_i2 = sin_i * tab[5:6], sin_i * tab[6:7]

    def rope_a(xb):
        return xb * cos_a + pltpu.roll(xb, LANES - ha, 1) * sin_a1 + pltpu.roll(xb, ha, 1) * sin_a2

    def rope_i(xb):
        return xb * cos_i + pltpu.roll(xb, LANES - hi, 1) * sin_i1 + pltpu.roll(xb, hi, 1) * sin_i2

    c_q = h_ref[0, :, 0:A_Q_RANK]
    c_kv = h_ref[0, :, A_Q_RANK:A_Q_RANK + A_KV_RANK]
    off = A_Q_RANK + A_KV_RANK
    k_rope = h_ref[0, :, off:off + LANES]
    k_idx = h_ref[0, :, off + LANES:off + 2 * LANES]
    w_idx = h_ref[0, :, off + 2 * LANES:off + 3 * LANES]

    cqn = _rms(c_q, nq_ref[...]).astype(BF16)
    ckv = _rms(c_kv, nkv_ref[...]).astype(BF16)
    kcat_ref[0] = ckv
    k_rot = rope_a(k_rope)
    for h in range(A_HEADS):
        kfull_ref[0, h] = (_dot(ckv, wuk_ref[h]) + k_rot).astype(BF16)

    ki = k_idx * lax.rsqrt(jnp.sum(k_idx * k_idx, axis=-1, keepdims=True) * (1.0 / IDX_DIM) + NORM_EPS)
    kidx_ref[0] = rope_i(ki * nki_ref[...]).astype(BF16)
    widx_ref[0] = w_idx * ((IDX_HEADS ** -0.5) * (IDX_DIM ** -0.5))

    q = _dot(cqn, wuq_ref[...])
    nope = 1.0 - tab[1:2]
    for h in range(A_HEADS):
        qh = q[:, h * A_HEAD_DIM:(h + 1) * A_HEAD_DIM]
        qcat_ref[0, h] = ((rope_a(qh) + qh * nope) * SOFTMAX_C).astype(BF16)
    qi = _dot(cqn, wqi_ref[...])
    for h in range(IDX_HEADS):
        qidx_ref[0, h] = rope_i(qi[:, h * LANES:(h + 1) * LANES]).astype(BF16)


def _dsa_proj(h_a, start_frame, tab, nq, nkv, nki, wuq, wuk_t, wqi, *, tm=256):
    b, s, _ = h_a.shape
    tm = min(tm, s)
    c3 = lambda bi, i, sf: (bi, i, 0)
    c4 = lambda bi, i, sf: (bi, 0, i, 0)
    z2 = lambda bi, i, sf: (0, 0)
    z3 = lambda bi, i, sf: (0, 0, 0)
    grid_spec = pltpu.PrefetchScalarGridSpec(
        num_scalar_prefetch=1,
        grid=(b, s // tm),
        in_specs=[pl.BlockSpec((1, tm, A_PAD_COLS), c3),
                  pl.BlockSpec(tab.shape, z2),
                  pl.BlockSpec(nq.shape, z2),
                  pl.BlockSpec(nkv.shape, z2),
                  pl.BlockSpec(nki.shape, z2),
                  pl.BlockSpec(wuq.shape, z2),
                  pl.BlockSpec(wuk_t.shape, z3),
                  pl.BlockSpec(wqi.shape, z2)],
        out_specs=[pl.BlockSpec((1, A_HEADS, tm, A_HEAD_DIM), c4),
                   pl.BlockSpec((1, tm, A_KV_RANK), c3),
                   pl.BlockSpec((1, A_HEADS, tm, A_HEAD_DIM), c4),
                   pl.BlockSpec((1, IDX_HEADS, tm, LANES), c4),
                   pl.BlockSpec((1, tm, LANES), c3),
                   pl.BlockSpec((1, tm, LANES), c3)],
    )
    return pl.pallas_call(
        _dsa_proj_body,
        grid_spec=grid_spec,
        out_shape=[jax.ShapeDtypeStruct((b, A_HEADS, s, A_HEAD_DIM), BF16),
                   jax.ShapeDtypeStruct((b, s, A_KV_RANK), BF16),
                   jax.ShapeDtypeStruct((b, A_HEADS, s, A_HEAD_DIM), BF16),
                   jax.ShapeDtypeStruct((b, IDX_HEADS, s, LANES), BF16),
                   jax.ShapeDtypeStruct((b, s, LANES), BF16),
                   jax.ShapeDtypeStruct((b, s, LANES), F32)],
        compiler_params=_cparams(("parallel", "parallel")),
        name="dsa_proj",
    )(start_frame, h_a, tab, nq, nkv, nki, wuq, wuk_t, wqi)


def _dsa_attend_body(qidx_ref, widx_ref, kidx_ref, qcat_ref, kcat_ref, kfull_ref, wuv_ref, o_ref,
                     key_ref, m_ref, l_ref, a_ref, acc_ref, t_ref, p_ref, *, tq, kt, topk):
    i = pl.program_id(1)
    t0 = i * tq
    nk = (t0 + tq + kt - 1) // kt
    krow = lax.broadcasted_iota(I32, (kt, tq), 0)
    qcol = lax.broadcasted_iota(I32, (kt, tq), 1)
    limit = ((t0 + qcol) // CHUNK + 1) * CHUNK

    wt = widx_ref[0]

    def score_tile(j, carry):
        off = pl.multiple_of(j * kt, kt)
        kid = kidx_ref[0, pl.ds(off, kt), :]
        sc = jnp.zeros((kt, tq), F32)
        for h in range(IDX_HEADS):
            sc = sc + wt[h:h + 1, :] * jnp.maximum(_dot_nt(kid, qidx_ref[0, h]), 0.0)
        bits = pltpu.bitcast(sc, I32)
        key = jnp.where(bits >= 0, bits, bits ^ 0x7FFFFFFF)
        key_ref[j] = jnp.where(off + krow < limit, key, INT_MIN)
        return carry

    lax.fori_loop(0, nk, score_tile, 0)

    def count_ge(cand):
        def body(j, acc):
            hit = jnp.where(key_ref[j] >= cand, 1.0, 0.0)
            return acc + jnp.sum(hit.reshape(kt // COUNT_ROWS, COUNT_ROWS, tq), axis=0)
        acc = lax.fori_loop(0, nk, body, jnp.zeros((COUNT_ROWS, tq), F32))
        return jnp.sum(acc, axis=0, keepdims=True)

    kf = float(topk)
    thr = jnp.where(count_ge(jnp.zeros((1, tq), I32)) >= kf, 0, INT_MIN).astype(I32)

    def bit_step(it, thr):
        cand = thr | jnp.left_shift(jnp.int32(1), 30 - it)
        return jnp.where(count_ge(cand) >= kf, cand, thr)

    thr = lax.fori_loop(0, 31, bit_step, thr)
    thr = jnp.maximum(thr, INT_MIN + 1)

    n_ge = count_ge(thr)

    @pl.when(jnp.max(n_ge) > kf)
    def _():
        need = kf - count_ge(thr + 1)
        tri = jnp.where(lax.broadcasted_iota(I32, (kt, kt), 0) >= lax.broadcasted_iota(I32, (kt, kt), 1),
                        1.0, 0.0).astype(BF16)

        def drop_late_ties(j, seen):
            key = key_ref[j]
            tied = key == thr
            rank = _dot(tri, jnp.where(tied, 1.0, 0.0).astype(BF16)) + seen
            key_ref[j] = jnp.where(tied, jnp.where(rank > need, INT_MIN, key), key)
            return rank[kt - 1:kt, :]

        lax.fori_loop(0, nk, drop_late_ties, jnp.zeros((1, tq), F32))

    m_ref[...] = jnp.full(m_ref.shape, M_FLOOR, F32)
    l_ref[...] = jnp.zeros(l_ref.shape, F32)
    acc_ref[...] = jnp.zeros(acc_ref.shape, F32)
    eye_q = jnp.where(lax.broadcasted_iota(I32, (tq, tq), 0) == lax.broadcasted_iota(I32, (tq, tq), 1),
                      1.0, 0.0).astype(BF16)
    lane_tiles = [slice(c * LANES, (c + 1) * LANES) for c in range(kt // LANES)]
    acc_tiles = [slice(c * LANES, (c + 1) * LANES) for c in range(A_KV_RANK // LANES)]

    def attend_tile(j, carry):
        off = pl.multiple_of(j * kt, kt)
        ckv = kcat_ref[0, pl.ds(off, kt), :]
        bias_t = jnp.where(key_ref[j] >= thr, 0.0, NEG_BIG).astype(BF16)
        bias = _dot_nt(eye_q, bias_t)
        for h in range(A_HEADS):
            t_ref[h] = _dot_nt(qcat_ref[0, h], kfull_ref[0, h, pl.ds(off, kt), :]) + bias
        for h in range(A_HEADS):
            ts = [t_ref[h, :, lt] for lt in lane_tiles]
            m_old = m_ref[h]
            tmax = functools.reduce(jnp.maximum, ts)
            m_new = jnp.maximum(m_old, jnp.max(tmax, axis=-1, keepdims=True))
            ps = [jnp.exp2(t - m_new) for t in ts]
            alpha = jnp.exp2(m_old - m_new)
            l_ref[h] = alpha * l_ref[h] + jnp.sum(functools.reduce(jnp.add, ps), axis=-1, keepdims=True)
            for lt, p in zip(lane_tiles, ps):
                p_ref[h, :, lt] = p.astype(BF16)
            a_ref[h] = alpha
            m_ref[h] = m_new
        for h in range(A_HEADS):
            pv = _dot(p_ref[h], ckv)
            for at in acc_tiles:
                acc_ref[h, :, at] = a_ref[h] * acc_ref[h, :, at] + pv[:, at]
        return carry

    lax.fori_loop(0, nk, attend_tile, 0)

    for h in range(A_HEADS):
        inv_l = 1.0 / l_ref[h]
        o_lat = jnp.concatenate([acc_ref[h, :, at] * inv_l for at in acc_tiles], axis=1)
        o_ref[0, :, h * A_V_DIM:(h + 1) * A_V_DIM] = _dot(o_lat.astype(BF16), wuv_ref[h]).astype(o_ref.dtype)


def _dsa_attend(qidx, widx, kidx, qcat, kcat, kfull, wuv, *, tq=256, kt=512):
    b, _, s, _ = qidx.shape
    kt = min(kt, s)
    tq = min(tq, s)
    topk = min(TOPK_MAX, s // 4)
    assert s % kt == 0 and s % tq == 0 and tq % CHUNK == 0
    body = functools.partial(_dsa_attend_body, tq=tq, kt=kt, topk=topk)
    widx_t = jnp.transpose(widx[:, :, 0:IDX_HEADS], (0, 2, 1))
    return pl.pallas_call(
        body,
        grid=(b, s // tq),
        in_specs=[pl.BlockSpec((1, IDX_HEADS, tq, LANES), lambda bi, i: (bi, 0, i, 0)),
                  pl.BlockSpec((1, IDX_HEADS, tq), lambda bi, i: (bi, 0, i)),
                  pl.BlockSpec((1, s, LANES), lambda bi, i: (bi, 0, 0)),
                  pl.BlockSpec((1, A_HEADS, tq, A_HEAD_DIM), lambda bi, i: (bi, 0, i, 0)),
                  pl.BlockSpec((1, s, A_KV_RANK), lambda bi, i: (bi, 0, 0)),
                  pl.BlockSpec((1, A_HEADS, s, A_HEAD_DIM), lambda bi, i: (bi, 0, 0, 0)),
                  pl.BlockSpec(wuv.shape, lambda bi, i: (0, 0, 0))],
        out_specs=pl.BlockSpec((1, tq, A_WIDTH), lambda bi, i: (bi, i, 0)),
        out_shape=jax.ShapeDtypeStruct((b, s, A_WIDTH), BF16),
        scratch_shapes=[pltpu.VMEM((s // kt, kt, tq), I32),
                        pltpu.VMEM((A_HEADS, tq, LANES), F32),
                        pltpu.VMEM((A_HEADS, tq, LANES), F32),
                        pltpu.VMEM((A_HEADS, tq, LANES), F32),
                        pltpu.VMEM((A_HEADS, tq, A_KV_RANK), F32),
                        pltpu.VMEM((A_HEADS, tq, kt), F32),
                        pltpu.VMEM((A_HEADS, tq, kt), BF16)],
        compiler_params=_cparams(("parallel", "arbitrary")),
        name="dsa_attend",
    )(qidx, widx_t, kidx, qcat, kcat, kfull, wuv)


def _rwkv_prep_body(h_ref, prev_ref, mu_ref, w0_ref, a0_ref, w2_ref, a2_ref, g2_ref,
                    rkv_ref, ld_ref, a_ref, g_ref):
    i = pl.program_id(1)
    h = h_ref[0]
    tm = h.shape[0]
    prev = jnp.where(i > 0, prev_ref[0, 7:8, :], 0.0)
    row = lax.broadcasted_iota(I32, h.shape, 0)
    shifted = jnp.where(row == 0, prev, pltpu.roll(h, 1, 0))
    hm = h + (shifted - h) * mu_ref[...]
    w3 = 3 * B_WIDTH
    rkv_ref[0] = hm[:, 0:w3]
    wl = hm[:, w3:w3 + LANES]
    al = hm[:, w3 + LANES:w3 + 2 * LANES]
    gl = hm[:, w3 + 2 * LANES:w3 + 4 * LANES]
    wz = w0_ref[...] + _dot(jnp.tanh(wl).astype(BF16), w2_ref[...])
    log_w = -jax.nn.softplus(-wz) - 0.5
    ld_ref[0] = -jnp.exp(log_w)
    a_ref[0] = jax.nn.sigmoid(a0_ref[...] + _dot(al.astype(BF16), a2_ref[...]))
    g_ref[0] = _dot(jax.nn.sigmoid(gl).astype(BF16), g2_ref[...])


def _rwkv_prep(h_b, mu, w0, a0, w2, a2, g2, *, tm=256):
    b, s, cols = h_b.shape
    tm = min(tm, s)
    c3 = lambda bi, i: (bi, i, 0)
    z2 = lambda bi, i: (0, 0)
    prev_map = lambda bi, i: (bi, jnp.maximum(i * (tm // 8) - 1, 0), 0)
    out = lambda w: jax.ShapeDtypeStruct((b, s, w), F32)
    return pl.pallas_call(
        _rwkv_prep_body,
        grid=(b, s // tm),
        in_specs=[pl.BlockSpec((1, tm, cols), c3),
                  pl.BlockSpec((1, 8, cols), prev_map),
                  pl.BlockSpec(mu.shape, z2), pl.BlockSpec(w0.shape, z2), pl.BlockSpec(a0.shape, z2),
                  pl.BlockSpec(w2.shape, z2), pl.BlockSpec(a2.shape, z2), pl.BlockSpec(g2.shape, z2)],
        out_specs=[pl.BlockSpec((1, tm, 3 * B_WIDTH), c3), pl.BlockSpec((1, tm, B_WIDTH), c3),
                   pl.BlockSpec((1, tm, B_WIDTH), c3), pl.BlockSpec((1, tm, B_WIDTH), c3)],
        out_shape=[out(3 * B_WIDTH), out(B_WIDTH), out(B_WIDTH), out(B_WIDTH)],
        compiler_params=_cparams(("parallel", "parallel")),
        name="rwkv_prep",
    )(h_b, h_b, mu, w0, a0, w2, a2, g2)


RWKV_CHUNK = 64
RWKV_SUB = 16


def _bmm(a, b, kind="nn"):
    spec = {"nn": "bij,bjk->bik", "nt": "bik,bjk->bij"}[kind]
    return jnp.einsum(spec, a.astype(BF16), b.astype(BF16), preferred_element_type=F32)


def _unit_lower_inverse(l_mat, eye, same_blk):
    mm = _bmm
    ld = jnp.where(same_blk, l_mat, 0.0)
    lo = l_mat - ld
    x = eye + ld
    p = mm(ld, ld)
    x = x + mm(x, p)
    p = mm(p, p)
    x = x + mm(x, p)
    p = mm(p, p)
    d = x + mm(x, p)
    n = mm(d, lo)
    y = eye + n
    y = y + mm(y, mm(n, n))
    return mm(y, d)


def _rwkv_scan_body(r_ref, k_ref, v_ref, ld_ref, a_ref, g_ref, par_ref, o_ref, st_ref):
    c = RWKV_CHUNK
    tblk = r_ref.shape[1]
    nc = tblk // c
    nh = r_ref.shape[2] // B_HEAD

    @pl.when(pl.program_id(2) == 0)
    def _():
        st_ref[...] = jnp.zeros(st_ref.shape, F32)

    def heads(x):
        return jnp.concatenate([x[:, hd * B_HEAD:(hd + 1) * B_HEAD].reshape(nc, c, B_HEAD) for hd in range(nh)],
                               axis=0)

    def head_row(i):
        p = par_ref[i:i + 1, :]
        return jnp.concatenate([jnp.broadcast_to(p[:, hd * B_HEAD:(hd + 1) * B_HEAD][None], (nc, 1, B_HEAD))
                                for hd in range(nh)], axis=0)

    ri = lax.broadcasted_iota(I32, (c, c), 0)
    ci = lax.broadcasted_iota(I32, (c, c), 1)
    eye = jnp.where(ri == ci, 1.0, 0.0).astype(F32)[None]
    lower = (ri >= ci)[None]
    strict = (ri > ci)[None]
    same_blk = ((ri // RWKV_SUB) == (ci // RWKV_SUB))[None]

    rb = lax.broadcasted_iota(I32, (tblk, tblk), 0)
    cb = lax.broadcasted_iota(I32, (tblk, tblk), 1)
    tri = jnp.where((rb >= cb) & (rb // c == cb // c), 1.0, 0.0).astype(BF16)
    ld2 = ld_ref[0]
    l1 = ld2.astype(BF16)
    rem = ld2 - l1.astype(F32)
    l2 = rem.astype(BF16)
    l3 = (rem - l2.astype(F32)).astype(BF16)
    cs = heads(_dot(tri, l1) + (_dot(tri, l2) + _dot(tri, l3)))

    r, k, v = heads(r_ref[0]), heads(k_ref[0]), heads(v_ref[0])
    ld, a, g = heads(ld2), heads(a_ref[0]), heads(g_ref[0])
    kk = k * head_row(0)
    kk = kk * lax.rsqrt(jnp.maximum(jnp.sum(kk * kk, axis=-1, keepdims=True), 1e-24))
    km = k * (1.0 + (a - 1.0) * head_row(1))
    bv = kk * a
    cs_end = cs[:, c - 1:c, :]
    e_neg = jnp.exp(-cs)
    e_end = jnp.exp(cs_end - cs)
    at = -kk * jnp.exp(cs - ld)
    rt = r * jnp.exp(cs)
    bt, kt = bv * e_neg, km * e_neg
    bd, kd = bv * e_end, km * e_end
    p_end_t = jnp.swapaxes(jnp.broadcast_to(jnp.exp(cs_end), cs.shape), 1, 2)

    lab = jnp.where(strict, _bmm(at, bt, "nt"), 0.0)
    lak = jnp.where(strict, _bmm(at, kt, "nt"), 0.0)
    lrb = jnp.where(lower, _bmm(rt, bt, "nt"), 0.0)
    lrk = jnp.where(lower, _bmm(rt, kt, "nt"), 0.0)
    tinv = _unit_lower_inverse(lab, eye, same_blk)
    tat = _bmm(tinv, at)
    c1 = _bmm(tinv, _bmm(lak, v))
    r2 = (rt + _bmm(lrb, tat)).astype(BF16)
    c2 = _bmm(lrk, v) + _bmm(lrb, c1)
    bd_t, kd_t = jnp.swapaxes(bd, 1, 2), jnp.swapaxes(kd, 1, 2)
    gmat_t = _bmm(bd_t, tat).astype(BF16)
    c3_t = _bmm(bd_t, c1) + _bmm(kd_t, v)

    outs = [None] * (nh * nc)
    sts = [st_ref[hd] for hd in range(nh)]
    for ch in range(nc):
        for hd in range(nh):
            i = hd * nc + ch
            sb = sts[hd].astype(BF16)
            outs[i] = _dot(r2[i], sb) + c2[i]
            sts[hd] = sts[hd] * p_end_t[i] + (_dot(gmat_t[i], sb) + c3_t[i])
    for hd in range(nh):
        st_ref[hd] = sts[hd]
    out = jnp.stack(outs, axis=0)

    mean = jnp.mean(out, axis=-1, keepdims=True)
    var = jnp.mean(jnp.square(out - mean), axis=-1, keepdims=True)
    y = (out - mean) * lax.rsqrt(var + B_LN_EPS) * head_row(3) + head_row(4)
    y = (y + jnp.sum(r * km * head_row(2), axis=-1, keepdims=True) * v) * g
    for hd in range(nh):
        o_ref[0, :, hd * B_HEAD:(hd + 1) * B_HEAD] = y[hd * nc:(hd + 1) * nc].reshape(tblk, B_HEAD).astype(o_ref.dtype)


def _rwkv_scan(rkv, ld, a, g, par, *, tblk=512, hw=2 * LANES):
    b, s, _ = ld.shape
    tblk = min(tblk, s)
    npair = B_WIDTH // hw
    blk = (1, tblk, hw)
    return pl.pallas_call(
        _rwkv_scan_body,
        grid=(b, npair, s // tblk),
        in_specs=[pl.BlockSpec(blk, lambda bi, hp, t: (bi, t, hp)),
                  pl.BlockSpec(blk, lambda bi, hp, t: (bi, t, npair + hp)),
                  pl.BlockSpec(blk, lambda bi, hp, t: (bi, t, 2 * npair + hp)),
                  pl.BlockSpec(blk, lambda bi, hp, t: (bi, t, hp)),
                  pl.BlockSpec(blk, lambda bi, hp, t: (bi, t, hp)),
                  pl.BlockSpec(blk, lambda bi, hp, t: (bi, t, hp)),
                  pl.BlockSpec((8, hw), lambda bi, hp, t: (0, hp))],
        out_specs=pl.BlockSpec(blk, lambda bi, hp, t: (bi, t, hp)),
        out_shape=jax.ShapeDtypeStruct((b, s, B_WIDTH), BF16),
        scratch_shapes=[pltpu.VMEM((hw // B_HEAD, B_HEAD, B_HEAD), F32)],
        compiler_params=_cparams(("parallel", "parallel", "arbitrary")),
        name="rwkv_scan",
    )(rkv, rkv, rkv, ld, a, g, par)


S5_CHUNK = LANES
S5_ROWS = S5_CHUNK * C_GROUP
S5_PASSES = 3


def _s5_group_body(lam_c_ref, lam_r_ref, bt_ref, c_ref, ct_ref, u_ref, y_ref,
                   kv_ref, m_ref, w_ref, v_ref, acc_ref, us_ref):
    L = S5_CHUNK
    n_chunk, bsz = u_ref.shape[0], u_ref.shape[1]
    rows = n_chunk * bsz

    def zoh(lre, lim, dt):
        lre = jnp.minimum(lre, -1e-4)
        rho, th = lre * dt, lim * dt
        mag = jnp.exp(rho)
        lbr, lbi = mag * jnp.cos(th), mag * jnp.sin(th)
        den = 1.0 / (lre * lre + lim * lim)
        cr = ((lbr - 1.0) * lre + lbi * lim) * den
        cim = (lbi * lre - (lbr - 1.0) * lim) * den
        return rho, th, cr, cim

    lc = lam_c_ref[0]
    rho_c, th_c, _, _ = zoh(lc[:, 0:1], lc[:, 1:2], jnp.exp(lc[:, 2:3]))
    lr = lam_r_ref[0]
    rho_r, th_r, cr_r, ci_r = zoh(lr[0:1], lr[1:2], jnp.exp(lr[2:3]))

    def powers(rho, th, n):
        mag = jnp.exp(rho * n)
        return mag * jnp.cos(th * n), mag * jnp.sin(th * n)

    btr, bti = bt_ref[0, 0], bt_ref[0, 1]
    bbr = btr * cr_r - bti * ci_r
    bbi = btr * ci_r + bti * cr_r
    c_re, c_im = c_ref[0, 0], c_ref[0, 1]

    d_re = jnp.concatenate([c_re * bbr[j:j + 1] - c_im * bbi[j:j + 1] for j in range(C_GROUP)], axis=0)
    d_im = jnp.concatenate([c_re * bbi[j:j + 1] + c_im * bbr[j:j + 1] for j in range(C_GROUP)], axis=0)
    tau = lax.broadcasted_iota(I32, (C_STATE, L), 1).astype(F32)
    e_re, e_im = powers(rho_c, th_c, tau)
    kvec = _dotp(d_re, e_re, S5_PASSES) - _dotp(d_im, e_im, S5_PASSES)

    kv_ref[...] = kvec

    back = (L - 1 - lax.broadcasted_iota(I32, (L, C_STATE), 0)).astype(F32)
    f_re, f_im = powers(rho_r, th_r, back)
    for j in range(C_GROUP):
        br, bi = bbr[j:j + 1], bbi[j:j + 1]
        w_ref[j * L:(j + 1) * L, :] = jnp.concatenate(
            [f_re * br - f_im * bi, f_re * bi + f_im * br], axis=1).astype(w_ref.dtype)

    e1_re, e1_im = powers(rho_c, th_c, tau + 1.0)
    ctr, cti = ct_ref[0, 0], ct_ref[0, 1]
    for i in range(C_GROUP):
        cr_i, ci_i = ctr[:, i:i + 1], cti[:, i:i + 1]
        v_ref[0:C_STATE, i * L:(i + 1) * L] = (cr_i * e1_re - ci_i * e1_im).astype(v_ref.dtype)
        v_ref[C_STATE:2 * C_STATE, i * L:(i + 1) * L] = (-(cr_i * e1_im + ci_i * e1_re)).astype(v_ref.dtype)

    for j in range(C_GROUP):
        us_ref[j] = u_ref[:, :, j, :].reshape(rows, L)
    u = jnp.concatenate([us_ref[j] for j in range(C_GROUP)], axis=1)
    x = _dot(u, w_ref[...])

    def shift_chunks(a, d):
        return jnp.concatenate([jnp.zeros((d * bsz, 2 * C_STATE), F32), a[:rows - d * bsz]], axis=0)

    d = 1
    while d < n_chunk:
        pr, pi = powers(rho_r, th_r, float(L * d))
        prev = shift_chunks(x, d)
        x = x + (prev * jnp.concatenate([pr, pr], axis=1)
                 + pltpu.roll(prev, C_STATE, 1) * jnp.concatenate([-pi, pi], axis=1))
        d *= 2
    x_in = shift_chunks(x, 1)
    acc_ref[...] = _dot(x_in.astype(BF16), v_ref[...])

    upper = lax.broadcasted_iota(I32, (L, L), 1) >= lax.broadcasted_iota(I32, (L, L), 0)

    def block_rows(jj, carry):
        for half in range(2):
            j = 2 * jj + half
            for i in range(C_GROUP):
                kv = jnp.broadcast_to(kv_ref[pl.ds(j * C_GROUP + i, 1), :], (L, L))
                blk = jnp.where(upper, pltpu.roll(kv, 0, 1, stride=1, stride_axis=0), 0.0)
                m_ref[half * L:(half + 1) * L, i * L:(i + 1) * L] = blk.astype(m_ref.dtype)
        uj = jnp.concatenate([us_ref[2 * jj], us_ref[2 * jj + 1]], axis=1)
        acc_ref[...] += _dot(uj, m_ref[...])
        return carry

    lax.fori_loop(0, C_GROUP // 2, block_rows, 0)
    for i in range(C_GROUP):
        y_ref[:, :, i, :] = acc_ref[:, i * L:(i + 1) * L].reshape(n_chunk, bsz, L)


def _s5_groups(u4, lam_c, lam_r, bt, c, ct):
    n_chunk, bsz, width, _ = u4.shape
    g = width // C_GROUP
    rows = n_chunk * bsz
    i3 = lambda gi: (gi, 0, 0)
    i4 = lambda gi: (gi, 0, 0, 0)
    blk = pl.BlockSpec((n_chunk, bsz, C_GROUP, S5_CHUNK), lambda gi: (0, 0, gi, 0))
    return pl.pallas_call(
        _s5_group_body,
        grid=(g,),
        in_specs=[pl.BlockSpec((1, C_STATE, 8), i3),
                  pl.BlockSpec((1, 8, C_STATE), i3),
                  pl.BlockSpec((1, 2, C_GROUP, C_STATE), i4),
                  pl.BlockSpec((1, 2, C_GROUP, C_STATE), i4),
                  pl.BlockSpec((1, 2, C_STATE, C_GROUP), i4),
                  blk],
        out_specs=blk,
        out_shape=jax.ShapeDtypeStruct(u4.shape, F32),
        scratch_shapes=[pltpu.VMEM((C_GROUP * C_GROUP, S5_CHUNK), F32),
                        pltpu.VMEM((2 * S5_CHUNK, S5_ROWS), BF16),
                        pltpu.VMEM((S5_ROWS, 2 * C_STATE), BF16),
                        pltpu.VMEM((2 * C_STATE, S5_ROWS), BF16),
                        pltpu.VMEM((rows, S5_ROWS), F32),
                        pltpu.VMEM((C_GROUP, rows, S5_CHUNK), BF16)],
        compiler_params=_cparams(("parallel",)),
        name="s5_groups",
    )(lam_c, lam_r, bt, c, ct, u4)


def _s5_in_body(x_ref, g_ref, w_ref, u_ref, u4_ref):
    xs = _rms(x_ref[...], g_ref[...]).astype(BF16)
    u = _dot(xs, w_ref[...])
    u_ref[...] = u
    for k in range(u4_ref.shape[0]):
        u4_ref[k, 0] = u[k * S5_CHUNK:(k + 1) * S5_CHUNK, :].T.astype(BF16)


def _s5_in(x2, g, w, bsz, seq, *, tm=512):
    m, d = x2.shape
    n = w.shape[1]
    tm = min(tm, seq)
    tiles_per_seq = seq // tm
    ck = tm // S5_CHUNK
    return pl.pallas_call(
        _s5_in_body,
        grid=(m // tm,),
        in_specs=[pl.BlockSpec((tm, d), lambda i: (i, 0)),
                  pl.BlockSpec((1, d), lambda i: (0, 0)),
                  pl.BlockSpec((d, n), lambda i: (0, 0))],
        out_specs=[pl.BlockSpec((tm, n), lambda i: (i, 0)),
                   pl.BlockSpec((ck, 1, n, S5_CHUNK), lambda i: (i % tiles_per_seq, i // tiles_per_seq, 0, 0))],
        out_shape=[jax.ShapeDtypeStruct((m, n), F32),
                   jax.ShapeDtypeStruct((seq // S5_CHUNK, bsz, n, S5_CHUNK), BF16)],
        compiler_params=_cparams(("parallel",)),
        name="odd_in",
    )(x2, g, w)


def _pad_cols(w, width):
    return jnp.pad(w, ((0, 0), (0, width - w.shape[1])))


def _pad_rows(w, height):
    return jnp.pad(w, ((0, height - w.shape[0]), (0, 0)))


def _even_in_weights(w_in):
    o = 0
    pieces = {}
    for name, width in (("cq", A_Q_RANK), ("ckv", A_KV_RANK), ("krope", A_ROPE_DIM), ("kidx", IDX_DIM),
                        ("widx", IDX_HEADS), ("rkv", 3 * B_WIDTH), ("wl", B_DECAY_LORA),
                        ("al", B_A_LORA), ("gl", B_GATE_LORA)):
        pieces[name] = w_in[:, o:o + width]
        o += width
    w_a = jnp.concatenate([pieces["cq"], pieces["ckv"], _pad_cols(pieces["krope"], LANES),
                           _pad_cols(pieces["kidx"], LANES), _pad_cols(pieces["widx"], LANES)], axis=1)
    w_b = jnp.concatenate([pieces["rkv"], _pad_cols(pieces["wl"], LANES), _pad_cols(pieces["al"], LANES),
                           _pad_cols(pieces["gl"], 2 * LANES)], axis=1)
    return w_a.astype(BF16), w_b.astype(BF16)


def _pad_mu(mu):
    o = 3 * B_WIDTH
    wl = mu[o:o + B_DECAY_LORA]
    al = mu[o + B_DECAY_LORA:o + B_DECAY_LORA + B_A_LORA]
    gl = mu[o + B_DECAY_LORA + B_A_LORA:]
    z = lambda n: jnp.zeros((n,), F32)
    return jnp.concatenate([mu[:o], wl, z(LANES - B_DECAY_LORA), al, z(LANES - B_A_LORA),
                            gl, z(2 * LANES - B_GATE_LORA)])[None, :]


def _dsa_mixer(h_a3, start_frame, cq_norm, ckv_norm, kidx_norm, w_uq, w_uk, w_uv, w_qidx):
    wuq = w_uq.reshape(A_Q_RANK, A_HEADS * A_HEAD_DIM).astype(BF16)
    wuk_h = jnp.pad(jnp.transpose(w_uk, (1, 0, 2)), ((0, 0), (0, 0), (A_ROPE_DIM, 0))).astype(BF16)
    wqi = jnp.pad(w_qidx, ((0, 0), (0, 0), (0, LANES - IDX_DIM))).reshape(A_Q_RANK, IDX_HEADS * LANES).astype(BF16)
    wuv = jnp.transpose(w_uv, (1, 0, 2)).astype(BF16)
    qcat, kcat, kfull, qidx, kidx, widx = _dsa_proj(
        h_a3, start_frame, _rope_tables(), cq_norm[None, :], ckv_norm[None, :],
        _pad_cols(kidx_norm[None, :], LANES), wuq, wuk_h, wqi)
    return _dsa_attend(qidx, widx, kidx, qcat, kcat, kfull, wuv)


def _rwkv_mixer(h_b3, mu, w0, w2, a0, a2, g2, k_k, k_a, r_k, ln_w, ln_b):
    rkv, ld, a, g = _rwkv_prep(h_b3, _pad_mu(mu), w0[None, :], a0[None, :],
                               _pad_rows(w2, LANES).astype(BF16), _pad_rows(a2, LANES).astype(BF16),
                               _pad_rows(g2, 2 * LANES).astype(BF16))
    par = jnp.concatenate([k_k[None, :], k_a[None, :], r_k.reshape(1, B_WIDTH), ln_w[None, :], ln_b[None, :],
                           jnp.zeros((3, B_WIDTH), F32)], axis=0)
    return _rwkv_scan(rkv, ld, a, g, par)


def _even_mixer(x2, bsz, seq, start_frame, g_mix, w_in, w_out, cq_norm, ckv_norm, kidx_norm, w_uq, w_uk, w_uv,
                w_qidx, mu, w0, w2, a0, a2, g2, k_k, k_a, r_k, ln_w, ln_b):
    w_a, w_b = _even_in_weights(w_in)
    rms_pro = lambda x, g: _rms(x, g)
    h_a = _matmul([x2], [g_mix], w_a, [], prologue=rms_pro, epilogue=_epi_id, out_dtype=F32,
                  tm=512, tn=A_PAD_COLS, name="even_in_a")
    h_b = _matmul([x2], [g_mix], w_b, [], prologue=rms_pro, epilogue=_epi_id, out_dtype=F32,
                  tm=512, tn=B_PAD_COLS // 2, name="even_in_b")
    y_a = _dsa_mixer(h_a.reshape(bsz, seq, A_PAD_COLS), start_frame, cq_norm, ckv_norm, kidx_norm,
                     w_uq, w_uk, w_uv, w_qidx)
    y_b = _rwkv_mixer(h_b.reshape(bsz, seq, B_PAD_COLS), mu, w0, w2, a0, a2, g2, k_k, k_a, r_k, ln_w, ln_b)

    m = bsz * seq
    cat_pro = lambda ya, yb: jnp.concatenate([ya, yb], axis=-1)
    return _matmul([y_a.reshape(m, A_WIDTH), y_b.reshape(m, B_WIDTH)], [], w_out.astype(BF16), [(x2, "tile")],
                   prologue=cat_pro, epilogue=_epi_add, out_dtype=F32, tm=512, tn=D_MODEL, name="even_out")


def _gelu_tanh(y):
    return 0.5 * y * (1.0 + jnp.tanh(math.sqrt(2.0 / math.pi) * (y + 0.044715 * (y * y * y))))


def _odd_mixer(x2, bsz, seq, g_mix, w_in, w_out, lam_re, lam_im, log_dt, b_re, b_im, c_re, c_im, d_skip,
               w_glu, b_glu):
    u, u4 = _s5_in(x2, g_mix, w_in.astype(BF16), bsz, seq)
    gated = _s5_mixer(u, u4, lam_re, lam_im, log_dt, b_re, b_im, c_re, c_im, d_skip, w_glu, b_glu)
    return _matmul([gated], [], w_out.astype(BF16), [(x2, "tile")], prologue=_pro_id, epilogue=_epi_add,
                   out_dtype=F32, tm=512, tn=D_MODEL, name="odd_out")


def _s5_mixer(u, u4, lam_re, lam_im, log_dt, b_re, b_im, c_re, c_im, d_skip, w_glu, b_glu):
    n_chunk, bsz = u4.shape[0], u4.shape[1]
    m = u.shape[0]
    assert bsz % 8 == 0, "chunk rows are addressed as whole sublane tiles"

    ldt = jnp.broadcast_to(log_dt[:, None], lam_re.shape)
    lam_r = jnp.pad(jnp.stack([lam_re, lam_im, ldt], axis=1), ((0, 0), (0, 5), (0, 0)))
    lam_c = jnp.transpose(lam_r, (0, 2, 1))
    bt = jnp.stack([jnp.transpose(b_re, (0, 2, 1)), jnp.transpose(b_im, (0, 2, 1))], axis=1)
    c = jnp.stack([c_re, c_im], axis=1)
    ct = jnp.stack([jnp.transpose(c_re, (0, 2, 1)), jnp.transpose(c_im, (0, 2, 1))], axis=1)
    y4 = _s5_groups(u4, lam_c, lam_r, bt, c, ct)
    y = jnp.transpose(y4, (1, 0, 3, 2)).reshape(m, C_WIDTH)

    z_of = lambda yy, uu, dd: _gelu_tanh(yy + dd * uu)
    return _matmul([y, u], [d_skip[None, :]], w_glu.astype(BF16),
                   [(None, 0), (None, 1), (d_skip[None, :], "row"), (b_glu[None, :], "row")],
                   prologue=z_of,
                   epilogue=lambda acc, yy, uu, dd, bb: z_of(yy, uu, dd) * jax.nn.sigmoid(acc + bb),
                   out_dtype=BF16, tm=512, tn=C_WIDTH, name="s5_glu")


def kernel(x, mem, start_frame, norm_mix, norm_xattn, norm_mem, norm_ffn, final_norm, xattn_wq, xattn_wkv, xattn_wo, ffn_up, ffn_down, even_w_in, even_w_out, dsa_cq_norm, dsa_ckv_norm, dsa_kidx_norm, dsa_w_uq, dsa_w_uk, dsa_w_uv, dsa_w_qidx, rwkv_mu, rwkv_w0, rwkv_w2, rwkv_a0, rwkv_a2, rwkv_g2, rwkv_k_k, rwkv_k_a, rwkv_r_k, rwkv_ln_w, rwkv_ln_b, odd_w_in, odd_w_out, s5_lam_re, s5_lam_im, s5_log_dt, s5_b_re, s5_b_im, s5_c_re, s5_c_im, s5_d, s5_w_glu, s5_b_glu):
    bsz, seq, d = x.shape
    depth = norm_mix.shape[0]
    m = bsz * seq
    x2 = x.reshape(m, d)
    mem2 = mem.reshape(bsz * mem.shape[1], d)
    for layer in range(depth):
        i = layer // 2
        g_mix = norm_mix[layer][None, :]
        if layer % 2 == 0:
            x2 = _even_mixer(x2, bsz, seq, start_frame, g_mix, even_w_in[i], even_w_out[i], dsa_cq_norm[i],
                             dsa_ckv_norm[i], dsa_kidx_norm[i], dsa_w_uq[i], dsa_w_uk[i], dsa_w_uv[i],
                             dsa_w_qidx[i], rwkv_mu[i], rwkv_w0[i], rwkv_w2[i], rwkv_a0[i], rwkv_a2[i],
                             rwkv_g2[i], rwkv_k_k[i], rwkv_k_a[i], rwkv_r_k[i], rwkv_ln_w[i], rwkv_ln_b[i])
        else:
            x2 = _odd_mixer(x2, bsz, seq, g_mix, odd_w_in[i], odd_w_out[i], s5_lam_re[i], s5_lam_im[i],
                            s5_log_dt[i], s5_b_re[i], s5_b_im[i], s5_c_re[i], s5_c_im[i], s5_d[i],
                            s5_w_glu[i], s5_b_glu[i])
        kv = _matmul([mem2], [norm_mem[layer][None, :]], xattn_wkv[layer].astype(BF16), [],
                     prologue=lambda a, g: _rms(a, g), epilogue=_epi_id, out_dtype=BF16, tm=512, tn=1024,
                     name="xattn_kv")
        x2 = _xattn(x2.reshape(bsz, seq, d), norm_xattn[layer][None, :], xattn_wq[layer].astype(BF16),
                    kv.reshape(bsz, mem.shape[1], -1), xattn_wo[layer].astype(BF16)).reshape(m, d)
        x2 = _ffn(x2, norm_ffn[layer][None, :], ffn_up[layer].astype(BF16), ffn_down[layer].astype(BF16),
                  final_norm[None, :] if layer == depth - 1 else None)
    return x2.reshape(bsz, seq, d)
```

```python
import functools
import math

import jax
import jax.numpy as jnp
from jax import lax
from jax.experimental import pallas as pl
from jax.experimental.pallas import tpu as pltpu

F32 = jnp.float32
BF16 = jnp.bfloat16
I32 = jnp.int32

D_MODEL = 2048
CHUNK = 64
ROPE_THETA = 500000.0
NORM_EPS = 1e-5

A_HEADS = 8
A_HEAD_DIM = 128
A_ROPE_DIM = 32
A_V_DIM = 128
A_WIDTH = A_HEADS * A_V_DIM
A_Q_RANK = 512
A_KV_RANK = 256
IDX_HEADS = 16
IDX_DIM = 64
IDX_ROPE_DIM = 16
TOPK_MAX = 256

B_HEAD = 64
B_WIDTH = 1024
B_HEADS = 16
B_DECAY_LORA = 64
B_A_LORA = 64
B_GATE_LORA = 160
B_LN_EPS = 64e-5

C_WIDTH = 2048
C_GROUP = 16
C_GROUPS = 128
C_STATE = 64

X_HEADS = 4
X_HEAD_DIM = 128
FFN_DIM = 4 * D_MODEL

LANES = 128
VMEM_LIMIT = 56 * 1024 * 1024

A_PAD_COLS = A_Q_RANK + A_KV_RANK + 3 * LANES
B_LORA_COLS = 2 * LANES + 2 * LANES
B_PAD_COLS = 3 * B_WIDTH + B_LORA_COLS

INT_MIN = -2147483648
NEG_BIG = -1e30
M_FLOOR = -1e20
SOFTMAX_C = (A_HEAD_DIM ** -0.5) * math.log2(math.e)
COUNT_ROWS = 32


def _cparams(sem):
    return pltpu.CompilerParams(dimension_semantics=sem, vmem_limit_bytes=VMEM_LIMIT)


def _rms(x, g):
    return x * lax.rsqrt(jnp.mean(x * x, axis=-1, keepdims=True) + NORM_EPS) * g


def _dot(a, b):
    return jnp.dot(a, b, preferred_element_type=F32)


def _dot_nt(a, b):
    return lax.dot_general(a, b, (((1,), (1,)), ((), ())), preferred_element_type=F32)


def _split2(x):
    hi = x.astype(BF16)
    lo = (x - hi.astype(F32)).astype(BF16)
    return hi, lo


def _dot3(a, b):
    ah, al = _split2(a)
    bh, bl = _split2(b)
    return _dot(ah, bh) + (_dot(ah, bl) + _dot(al, bh))


def _mm_body(*refs, n_row, n_const, epi_kinds, single_n, prologue, epilogue):
    rows = refs[:n_row]
    consts = refs[n_row:n_row + n_const]
    w_ref = refs[n_row + n_const]
    epi_refs = iter(refs[n_row + n_const + 1:])
    epis = [rows[kind] if isinstance(kind, int) else next(epi_refs) for kind in epi_kinds]
    if single_n:
        o_ref = refs[-1]
        xs = prologue(*[r[...] for r in rows], *[c[...] for c in consts]).astype(BF16)
    else:
        o_ref, xs_ref = refs[-2], refs[-1]

        @pl.when(pl.program_id(1) == 0)
        def _():
            xs_ref[...] = prologue(*[r[...] for r in rows], *[c[...] for c in consts]).astype(BF16)

        xs = xs_ref[...]
    acc = _dot(xs, w_ref[...])
    o_ref[...] = epilogue(acc, *[e[...] for e in epis]).astype(o_ref.dtype)


def _matmul(rows, consts, w, epis, *, prologue, epilogue, out_dtype, tm, tn, name):
    m = rows[0].shape[0]
    k, n = w.shape
    tm = min(tm, m)
    tn = min(tn, n)
    assert m % tm == 0 and n % tn == 0, (m, tm, n, tn)
    single_n = tn == n
    in_specs = [pl.BlockSpec((tm, r.shape[1]), lambda i, j: (i, 0)) for r in rows]
    in_specs += [pl.BlockSpec(c.shape, lambda i, j: (0, 0)) for c in consts]
    in_specs += [pl.BlockSpec((k, tn), lambda i, j: (0, j))]
    for _, kind in epis:
        if kind == "tile":
            in_specs.append(pl.BlockSpec((tm, tn), lambda i, j: (i, j)))
        elif kind == "row":
            in_specs.append(pl.BlockSpec((1, tn), lambda i, j: (0, j)))
        else:
            assert single_n and rows[kind].shape[1] == n
    body = functools.partial(_mm_body, n_row=len(rows), n_const=len(consts),
                             epi_kinds=tuple(kind for _, kind in epis), single_n=single_n,
                             prologue=prologue, epilogue=epilogue)
    return pl.pallas_call(
        body,
        grid=(m // tm, n // tn),
        in_specs=in_specs,
        out_specs=pl.BlockSpec((tm, tn), lambda i, j: (i, j)),
        out_shape=jax.ShapeDtypeStruct((m, n), out_dtype),
        scratch_shapes=[] if single_n else [pltpu.VMEM((tm, k), BF16)],
        compiler_params=_cparams(("parallel", "arbitrary")),
        name=name,
    )(*rows, *consts, w, *[e for e, kind in epis if not isinstance(kind, int)])


def _epi_id(acc):
    return acc


def _epi_add(acc, res):
    return acc + res


def _pro_id(x):
    return x


def _ffn_body(x_ref, g_ref, up_ref, dn_ref, gout_ref, o_ref, xs_ref, *, norm_out):
    j = pl.program_id(1)

    @pl.when(j == 0)
    def _():
        x = x_ref[...]
        xs_ref[...] = _rms(x, g_ref[...]).astype(BF16)
        o_ref[...] = x

    h = _dot(xs_ref[...], up_ref[...])
    h = jnp.square(jnp.maximum(h, 0.0)).astype(BF16)
    o_ref[...] += _dot(h, dn_ref[...])

    if norm_out:
        @pl.when(j == pl.num_programs(1) - 1)
        def _():
            o_ref[...] = _rms(o_ref[...], gout_ref[...])


def _ffn(x, g, w_up, w_dn, g_out=None, *, tm=1024, tf=512):
    m, d = x.shape
    f = w_up.shape[1]
    tm = min(tm, m)
    body = functools.partial(_ffn_body, norm_out=g_out is not None)
    return pl.pallas_call(
        body,
        grid=(m // tm, f // tf),
        in_specs=[pl.BlockSpec((tm, d), lambda i, j: (i, 0)),
                  pl.BlockSpec((1, d), lambda i, j: (0, 0)),
                  pl.BlockSpec((d, tf), lambda i, j: (0, j)),
                  pl.BlockSpec((tf, d), lambda i, j: (j, 0)),
                  pl.BlockSpec((1, d), lambda i, j: (0, 0))],
        out_specs=pl.BlockSpec((tm, d), lambda i, j: (i, 0)),
        out_shape=jax.ShapeDtypeStruct((m, d), F32),
        scratch_shapes=[pltpu.VMEM((tm, d), BF16)],
        compiler_params=_cparams(("parallel", "arbitrary")),
        name="ffn",
    )(x, g, w_up, w_dn, g if g_out is None else g_out)


def _xattn_body(x_ref, g_ref, wq_ref, kv_ref, wo_ref, o_ref):
    x = x_ref[0]
    xn = _rms(x, g_ref[...]).astype(BF16)
    q = _dot(xn, wq_ref[...])
    width = X_HEADS * X_HEAD_DIM
    outs = []
    for h in range(X_HEADS):
        sl = slice(h * X_HEAD_DIM, (h + 1) * X_HEAD_DIM)
        qh = q[:, sl].astype(BF16)
        kh = kv_ref[0, :, sl]
        vh = kv_ref[0, :, width + h * X_HEAD_DIM:width + (h + 1) * X_HEAD_DIM]
        s = _dot_nt(qh, kh) * (X_HEAD_DIM ** -0.5)
        s = s - jnp.max(s, axis=-1, keepdims=True)
        p = jnp.exp(s)
        p = p / jnp.sum(p, axis=-1, keepdims=True)
        outs.append(_dot(p.astype(BF16), vh))
    o = jnp.concatenate(outs, axis=-1).astype(BF16)
    o_ref[0] = x + _dot(o, wo_ref[...])


def _xattn(x3, g, wq, kv, wo, *, tm=512):
    b, s, d = x3.shape
    tm = min(tm, s)
    mlen = kv.shape[1]
    width = X_HEADS * X_HEAD_DIM
    return pl.pallas_call(
        _xattn_body,
        grid=(b, s // tm),
        in_specs=[pl.BlockSpec((1, tm, d), lambda bi, i: (bi, i, 0)),
                  pl.BlockSpec((1, d), lambda bi, i: (0, 0)),
                  pl.BlockSpec((d, width), lambda bi, i: (0, 0)),
                  pl.BlockSpec((1, mlen, 2 * width), lambda bi, i: (bi, 0, 0)),
                  pl.BlockSpec((width, d), lambda bi, i: (0, 0))],
        out_specs=pl.BlockSpec((1, tm, d), lambda bi, i: (bi, i, 0)),
        out_shape=jax.ShapeDtypeStruct((b, s, d), F32),
        compiler_params=_cparams(("parallel", "parallel")),
        name="xattn",
    )(x3, g, wq, kv, wo)


def _rope_tables():
    inv_a = ROPE_THETA ** (-jnp.arange(0, A_ROPE_DIM, 2, dtype=F32) / A_ROPE_DIM)
    inv_i = ROPE_THETA ** (-jnp.arange(0, IDX_ROPE_DIM, 2, dtype=F32) / IDX_ROPE_DIM)
    ha, hi = A_ROPE_DIM // 2, IDX_ROPE_DIM // 2

    def row(*pieces):
        r = jnp.concatenate([jnp.asarray(p, F32) for p in pieces])
        return jnp.pad(r, (0, LANES - r.shape[0]))

    ones, zeros = (lambda n: jnp.ones((n,), F32)), (lambda n: jnp.zeros((n,), F32))
    return jnp.stack([row(inv_a, inv_a), row(ones(2 * ha)), row(-ones(ha)), row(zeros(ha), ones(ha)),
                      row(inv_i, inv_i), row(-ones(hi)), row(zeros(hi), ones(hi)), zeros(LANES)])


def _dsa_proj_body(sf_ref, h_ref, tab_ref, nq_ref, nkv_ref, nki_ref, wuq_ref, wuk_ref, wqi_ref,
                   qcat_ref, kcat_ref, kfull_ref, qidx_ref, kidx_ref, widx_ref):
    bi, i = pl.program_id(0), pl.program_id(1)
    tm = h_ref.shape[1]
    ha, hi = A_ROPE_DIM // 2, IDX_ROPE_DIM // 2
    pos = (sf_ref[bi] + i * tm + lax.broadcasted_iota(I32, (tm, LANES), 0)).astype(F32)
    tab = tab_ref[...]
    ang_a = pos * tab[0:1]
    cos_a = jnp.cos(ang_a) * tab[1:2]
    sin_a = jnp.sin(ang_a)
    sin_a1, sin_a2 = sin_a * tab[2:3], sin_a * tab[3:4]
    ang_i = pos * tab[4:5]
    cos_i = jnp.cos(ang_i)
    sin_i = jnp.sin(ang_i)
    sin_i1, sin_i2 = sin_i * tab[5:6], sin_i * tab[6:7]

    def rope_a(xb):
        return xb * cos_a + pltpu.roll(xb, LANES - ha, 1) * sin_a1 + pltpu.roll(xb, ha, 1) * sin_a2

    def rope_i(xb):
        return xb * cos_i + pltpu.roll(xb, LANES - hi, 1) * sin_i1 + pltpu.roll(xb, hi, 1) * sin_i2

    c_q = h_ref[0, :, 0:A_Q_RANK]
    c_kv = h_ref[0, :, A_Q_RANK:A_Q_RANK + A_KV_RANK]
    off = A_Q_RANK + A_KV_RANK
    k_rope = h_ref[0, :, off:off + LANES]
    k_idx = h_ref[0, :, off + LANES:off + 2 * LANES]
    w_idx = h_ref[0, :, off + 2 * LANES:off + 3 * LANES]

    cqn = _rms(c_q, nq_ref[...]).astype(BF16)
    ckv = _rms(c_kv, nkv_ref[...]).astype(BF16)
    kcat_ref[0] = ckv
    k_rot = rope_a(k_rope)
    for h in range(A_HEADS):
        kfull_ref[0, h] = (_dot(ckv, wuk_ref[h]) + k_rot).astype(BF16)

    ki = k_idx * lax.rsqrt(jnp.sum(k_idx * k_idx, axis=-1, keepdims=True) * (1.0 / IDX_DIM) + NORM_EPS)
    kidx_ref[0] = rope_i(ki * nki_ref[...]).astype(BF16)
    widx_ref[0] = w_idx * ((IDX_HEADS ** -0.5) * (IDX_DIM ** -0.5))

    q = _dot(cqn, wuq_ref[...])
    nope = 1.0 - tab[1:2]
    for h in range(A_HEADS):
        qh = q[:, h * A_HEAD_DIM:(h + 1) * A_HEAD_DIM]
        qcat_ref[0, h] = ((rope_a(qh) + qh * nope) * SOFTMAX_C).astype(BF16)
    qi = _dot(cqn, wqi_ref[...])
    for h in range(IDX_HEADS):
        qidx_ref[0, h] = rope_i(qi[:, h * LANES:(h + 1) * LANES]).astype(BF16)


def _dsa_proj(h_a, start_frame, tab, nq, nkv, nki, wuq, wuk_t, wqi, *, tm=256):
    b, s, _ = h_a.shape
    tm = min(tm, s)
    c3 = lambda bi, i, sf: (bi, i, 0)
    c4 = lambda bi, i, sf: (bi, 0, i, 0)
    z2 = lambda bi, i, sf: (0, 0)
    z3 = lambda bi, i, sf: (0, 0, 0)
    grid_spec = pltpu.PrefetchScalarGridSpec(
        num_scalar_prefetch=1,
        grid=(b, s // tm),
        in_specs=[pl.BlockSpec((1, tm, A_PAD_COLS), c3),
                  pl.BlockSpec(tab.shape, z2),
                  pl.BlockSpec(nq.shape, z2),
                  pl.BlockSpec(nkv.shape, z2),
                  pl.BlockSpec(nki.shape, z2),
                  pl.BlockSpec(wuq.shape, z2),
                  pl.BlockSpec(wuk_t.shape, z3),
                  pl.BlockSpec(wqi.shape, z2)],
        out_specs=[pl.BlockSpec((1, A_HEADS, tm, A_HEAD_DIM), c4),
                   pl.BlockSpec((1, tm, A_KV_RANK), c3),
                   pl.BlockSpec((1, A_HEADS, tm, A_HEAD_DIM), c4),
                   pl.BlockSpec((1, IDX_HEADS, tm, LANES), c4),
                   pl.BlockSpec((1, tm, LANES), c3),
                   pl.BlockSpec((1, tm, LANES), c3)],
    )
    return pl.pallas_call(
        _dsa_proj_body,
        grid_spec=grid_spec,
        out_shape=[jax.ShapeDtypeStruct((b, A_HEADS, s, A_HEAD_DIM), BF16),
                   jax.ShapeDtypeStruct((b, s, A_KV_RANK), BF16),
                   jax.ShapeDtypeStruct((b, A_HEADS, s, A_HEAD_DIM), BF16),
                   jax.ShapeDtypeStruct((b, IDX_HEADS, s, LANES), BF16),
                   jax.ShapeDtypeStruct((b, s, LANES), BF16),
                   jax.ShapeDtypeStruct((b, s, LANES), F32)],
        compiler_params=_cparams(("parallel", "parallel")),
        name="dsa_proj",
    )(start_frame, h_a, tab, nq, nkv, nki, wuq, wuk_t, wqi)


def _dsa_attend_body(qidx_ref, widx_ref, kidx_ref, qcat_ref, kcat_ref, kfull_ref, wuv_ref, o_ref,
                     key_ref, m_ref, l_ref, a_ref, acc_ref, t_ref, p_ref, *, tq, kt, topk):
    i = pl.program_id(1)
    t0 = i * tq
    nk = (t0 + tq + kt - 1) // kt
    krow = lax.broadcasted_iota(I32, (kt, tq), 0)
    qcol = lax.broadcasted_iota(I32, (kt, tq), 1)
    limit = ((t0 + qcol) // CHUNK + 1) * CHUNK

    wt = widx_ref[0]

    def score_tile(j, carry):
        off = pl.multiple_of(j * kt, kt)
        kid = kidx_ref[0, pl.ds(off, kt), :]
        sc = jnp.zeros((kt, tq), F32)
        for h in range(IDX_HEADS):
            sc = sc + wt[h:h + 1, :] * jnp.maximum(_dot_nt(kid, qidx_ref[0, h]), 0.0)
        bits = pltpu.bitcast(sc, I32)
        key = jnp.where(bits >= 0, bits, bits ^ 0x7FFFFFFF)
        key_ref[j] = jnp.where(off + krow < limit, key, INT_MIN)
        return carry

    lax.fori_loop(0, nk, score_tile, 0)

    def count_ge(cand):
        def body(j, acc):
            hit = jnp.where(key_ref[j] >= cand, 1.0, 0.0)
            return acc + jnp.sum(hit.reshape(kt // COUNT_ROWS, COUNT_ROWS, tq), axis=0)
        acc = lax.fori_loop(0, nk, body, jnp.zeros((COUNT_ROWS, tq), F32))
        return jnp.sum(acc, axis=0, keepdims=True)

    kf = float(topk)
    thr = jnp.where(count_ge(jnp.zeros((1, tq), I32)) >= kf, 0, INT_MIN).astype(I32)

    def bit_step(it, thr):
        cand = thr | jnp.left_shift(jnp.int32(1), 30 - it)
        return jnp.where(count_ge(cand) >= kf, cand, thr)

    thr = lax.fori_loop(0, 31, bit_step, thr)
    thr = jnp.maximum(thr, INT_MIN + 1)

    n_ge = count_ge(thr)

    @pl.when(jnp.max(n_ge) > kf)
    def _():
        need = kf - count_ge(thr + 1)
        tri = jnp.where(lax.broadcasted_iota(I32, (kt, kt), 0) >= lax.broadcasted_iota(I32, (kt, kt), 1),
                        1.0, 0.0).astype(BF16)

        def drop_late_ties(j, seen):
            key = key_ref[j]
            tied = key == thr
            rank = _dot(tri, jnp.where(tied, 1.0, 0.0).astype(BF16)) + seen
            key_ref[j] = jnp.where(tied, jnp.where(rank > need, INT_MIN, key), key)
            return rank[kt - 1:kt, :]

        lax.fori_loop(0, nk, drop_late_ties, jnp.zeros((1, tq), F32))

    m_ref[...] = jnp.full(m_ref.shape, M_FLOOR, F32)
    l_ref[...] = jnp.zeros(l_ref.shape, F32)
    acc_ref[...] = jnp.zeros(acc_ref.shape, F32)
    eye_q = jnp.where(lax.broadcasted_iota(I32, (tq, tq), 0) == lax.broadcasted_iota(I32, (tq, tq), 1),
                      1.0, 0.0).astype(BF16)
    lane_tiles = [slice(c * LANES, (c + 1) * LANES) for c in range(kt // LANES)]
    acc_tiles = [slice(c * LANES, (c + 1) * LANES) for c in range(A_KV_RANK // LANES)]

    def attend_tile(j, carry):
        off = pl.multiple_of(j * kt, kt)
        ckv = kcat_ref[0, pl.ds(off, kt), :]
        bias_t = jnp.where(key_ref[j] >= thr, 0.0, NEG_BIG).astype(BF16)
        bias = _dot_nt(eye_q, bias_t)
        for h in range(A_HEADS):
            t_ref[h] = _dot_nt(qcat_ref[0, h], kfull_ref[0, h, pl.ds(off, kt), :]) + bias
        for h in range(A_HEADS):
            ts = [t_ref[h, :, lt] for lt in lane_tiles]
            m_old = m_ref[h]
            tmax = functools.reduce(jnp.maximum, ts)
            m_new = jnp.maximum(m_old, jnp.max(tmax, axis=-1, keepdims=True))
            ps = [jnp.exp2(t - m_new) for t in ts]
            alpha = jnp.exp2(m_old - m_new)
            l_ref[h] = alpha * l_ref[h] + jnp.sum(functools.reduce(jnp.add, ps), axis=-1, keepdims=True)
            for lt, p in zip(lane_tiles, ps):
                p_ref[h, :, lt] = p.astype(BF16)
            a_ref[h] = alpha
            m_ref[h] = m_new
        for h in range(A_HEADS):
            pv = _dot(p_ref[h], ckv)
            for at in acc_tiles:
                acc_ref[h, :, at] = a_ref[h] * acc_ref[h, :, at] + pv[:, at]
        return carry

    lax.fori_loop(0, nk, attend_tile, 0)

    for h in range(A_HEADS):
        inv_l = 1.0 / l_ref[h]
        o_lat = jnp.concatenate([acc_ref[h, :, at] * inv_l for at in acc_tiles], axis=1)
        o_ref[0, :, h * A_V_DIM:(h + 1) * A_V_DIM] = _dot(o_lat.astype(BF16), wuv_ref[h]).astype(o_ref.dtype)


def _dsa_attend(qidx, widx, kidx, qcat, kcat, kfull, wuv, *, tq=256, kt=512):
    b, _, s, _ = qidx.shape
    kt = min(kt, s)
    tq = min(tq, s)
    topk = min(TOPK_MAX, s // 4)
    assert s % kt == 0 and s % tq == 0 and tq % CHUNK == 0
    body = functools.partial(_dsa_attend_body, tq=tq, kt=kt, topk=topk)
    widx_t = jnp.transpose(widx[:, :, 0:IDX_HEADS], (0, 2, 1))
    return pl.pallas_call(
        body,
        grid=(b, s // tq),
        in_specs=[pl.BlockSpec((1, IDX_HEADS, tq, LANES), lambda bi, i: (bi, 0, i, 0)),
                  pl.BlockSpec((1, IDX_HEADS, tq), lambda bi, i: (bi, 0, i)),
                  pl.BlockSpec((1, s, LANES), lambda bi, i: (bi, 0, 0)),
                  pl.BlockSpec((1, A_HEADS, tq, A_HEAD_DIM), lambda bi, i: (bi, 0, i, 0)),
                  pl.BlockSpec((1, s, A_KV_RANK), lambda bi, i: (bi, 0, 0)),
                  pl.BlockSpec((1, A_HEADS, s, A_HEAD_DIM), lambda bi, i: (bi, 0, 0, 0)),
                  pl.BlockSpec(wuv.shape, lambda bi, i: (0, 0, 0))],
        out_specs=pl.BlockSpec((1, tq, A_WIDTH), lambda bi, i: (bi, i, 0)),
        out_shape=jax.ShapeDtypeStruct((b, s, A_WIDTH), BF16),
        scratch_shapes=[pltpu.VMEM((s // kt, kt, tq), I32),
                        pltpu.VMEM((A_HEADS, tq, LANES), F32),
                        pltpu.VMEM((A_HEADS, tq, LANES), F32),
                        pltpu.VMEM((A_HEADS, tq, LANES), F32),
                        pltpu.VMEM((A_HEADS, tq, A_KV_RANK), F32),
                        pltpu.VMEM((A_HEADS, tq, kt), F32),
                        pltpu.VMEM((A_HEADS, tq, kt), BF16)],
        compiler_params=_cparams(("parallel", "arbitrary")),
        name="dsa_attend",
    )(qidx, widx_t, kidx, qcat, kcat, kfull, wuv)


def _rwkv_prep_body(h_ref, prev_ref, mu_ref, w0_ref, a0_ref, w2_ref, a2_ref, g2_ref,
                    rkv_ref, ld_ref, a_ref, g_ref):
    i = pl.program_id(1)
    h = h_ref[0]
    tm = h.shape[0]
    prev = jnp.where(i > 0, prev_ref[0, 7:8, :], 0.0)
    row = lax.broadcasted_iota(I32, h.shape, 0)
    shifted = jnp.where(row == 0, prev, pltpu.roll(h, 1, 0))
    hm = h + (shifted - h) * mu_ref[...]
    w3 = 3 * B_WIDTH
    rkv_ref[0] = hm[:, 0:w3]
    wl = hm[:, w3:w3 + LANES]
    al = hm[:, w3 + LANES:w3 + 2 * LANES]
    gl = hm[:, w3 + 2 * LANES:w3 + 4 * LANES]
    wz = w0_ref[...] + _dot(jnp.tanh(wl).astype(BF16), w2_ref[...])
    log_w = -jax.nn.softplus(-wz) - 0.5
    ld_ref[0] = -jnp.exp(log_w)
    a_ref[0] = jax.nn.sigmoid(a0_ref[...] + _dot(al.astype(BF16), a2_ref[...]))
    g_ref[0] = _dot(jax.nn.sigmoid(gl).astype(BF16), g2_ref[...])


def _rwkv_prep(h_b, mu, w0, a0, w2, a2, g2, *, tm=256):
    b, s, cols = h_b.shape
    tm = min(tm, s)
    c3 = lambda bi, i: (bi, i, 0)
    z2 = lambda bi, i: (0, 0)
    prev_map = lambda bi, i: (bi, jnp.maximum(i * (tm // 8) - 1, 0), 0)
    out = lambda w: jax.ShapeDtypeStruct((b, s, w), F32)
    return pl.pallas_call(
        _rwkv_prep_body,
        grid=(b, s // tm),
        in_specs=[pl.BlockSpec((1, tm, cols), c3),
                  pl.BlockSpec((1, 8, cols), prev_map),
                  pl.BlockSpec(mu.shape, z2), pl.BlockSpec(w0.shape, z2), pl.BlockSpec(a0.shape, z2),
                  pl.BlockSpec(w2.shape, z2), pl.BlockSpec(a2.shape, z2), pl.BlockSpec(g2.shape, z2)],
        out_specs=[pl.BlockSpec((1, tm, 3 * B_WIDTH), c3), pl.BlockSpec((1, tm, B_WIDTH), c3),
                   pl.BlockSpec((1, tm, B_WIDTH), c3), pl.BlockSpec((1, tm, B_WIDTH), c3)],
        out_shape=[out(3 * B_WIDTH), out(B_WIDTH), out(B_WIDTH), out(B_WIDTH)],
        compiler_params=_cparams(("parallel", "parallel")),
        name="rwkv_prep",
    )(h_b, h_b, mu, w0, a0, w2, a2, g2)


RWKV_CHUNK = 64
RWKV_SUB = 16


def _bmm(a, b, kind="nn"):
    spec = {"nn": "bij,bjk->bik", "nt": "bik,bjk->bij"}[kind]
    return jnp.einsum(spec, a.astype(BF16), b.astype(BF16), preferred_element_type=F32)


def _unit_lower_inverse(l_mat, eye, same_blk):
    mm = _bmm
    ld = jnp.where(same_blk, l_mat, 0.0)
    lo = l_mat - ld
    x = eye + ld
    p = mm(ld, ld)
    x = x + mm(x, p)
    p = mm(p, p)
    x = x + mm(x, p)
    p = mm(p, p)
    d = x + mm(x, p)
    n = mm(d, lo)
    y = eye + n
    y = y + mm(y, mm(n, n))
    return mm(y, d)


def _rwkv_scan_body(r_ref, k_ref, v_ref, ld_ref, a_ref, g_ref, par_ref, o_ref, st_ref):
    c = RWKV_CHUNK
    tblk = r_ref.shape[1]
    nc = tblk // c
    nh = r_ref.shape[2] // B_HEAD

    @pl.when(pl.program_id(2) == 0)
    def _():
        st_ref[...] = jnp.zeros(st_ref.shape, F32)

    def heads(x):
        return jnp.concatenate([x[:, hd * B_HEAD:(hd + 1) * B_HEAD].reshape(nc, c, B_HEAD) for hd in range(nh)],
                               axis=0)

    def head_row(i):
        p = par_ref[i:i + 1, :]
        return jnp.concatenate([jnp.broadcast_to(p[:, hd * B_HEAD:(hd + 1) * B_HEAD][None], (nc, 1, B_HEAD))
                                for hd in range(nh)], axis=0)

    ri = lax.broadcasted_iota(I32, (c, c), 0)
    ci = lax.broadcasted_iota(I32, (c, c), 1)
    eye = jnp.where(ri == ci, 1.0, 0.0).astype(F32)[None]
    lower = (ri >= ci)[None]
    strict = (ri > ci)[None]
    same_blk = ((ri // RWKV_SUB) == (ci // RWKV_SUB))[None]

    rb = lax.broadcasted_iota(I32, (tblk, tblk), 0)
    cb = lax.broadcasted_iota(I32, (tblk, tblk), 1)
    tri = jnp.where((rb >= cb) & (rb // c == cb // c), 1.0, 0.0).astype(BF16)
    ld2 = ld_ref[0]
    l1 = ld2.astype(BF16)
    rem = ld2 - l1.astype(F32)
    l2 = rem.astype(BF16)
    l3 = (rem - l2.astype(F32)).astype(BF16)
    cs = heads(_dot(tri, l1) + (_dot(tri, l2) + _dot(tri, l3)))

    r, k, v = heads(r_ref[0]), heads(k_ref[0]), heads(v_ref[0])
    ld, a, g = heads(ld2), heads(a_ref[0]), heads(g_ref[0])
    kk = k * head_row(0)
    kk = kk * lax.rsqrt(jnp.maximum(jnp.sum(kk * kk, axis=-1, keepdims=True), 1e-24))
    km = k * (1.0 + (a - 1.0) * head_row(1))
    bv = kk * a
    cs_end = cs[:, c - 1:c, :]
    e_neg = jnp.exp(-cs)
    e_end = jnp.exp(cs_end - cs)
    at = -kk * jnp.exp(cs - ld)
    rt = r * jnp.exp(cs)
    bt, kt = bv * e_neg, km * e_neg
    bd, kd = bv * e_end, km * e_end
    p_end_t = jnp.swapaxes(jnp.broadcast_to(jnp.exp(cs_end), cs.shape), 1, 2)

    lab = jnp.where(strict, _bmm(at, bt, "nt"), 0.0)
    lak = jnp.where(strict, _bmm(at, kt, "nt"), 0.0)
    lrb = jnp.where(lower, _bmm(rt, bt, "nt"), 0.0)
    lrk = jnp.where(lower, _bmm(rt, kt, "nt"), 0.0)
    tinv = _unit_lower_inverse(lab, eye, same_blk)
    tat = _bmm(tinv, at)
    c1 = _bmm(tinv, _bmm(lak, v))
    r2 = (rt + _bmm(lrb, tat)).astype(BF16)
    c2 = _bmm(lrk, v) + _bmm(lrb, c1)
    bd_t, kd_t = jnp.swapaxes(bd, 1, 2), jnp.swapaxes(kd, 1, 2)
    gmat_t = _bmm(bd_t, tat).astype(BF16)
    c3_t = _bmm(bd_t, c1) + _bmm(kd_t, v)

    outs = [None] * (nh * nc)
    sts = [st_ref[hd] for hd in range(nh)]
    for ch in range(nc):
        for hd in range(nh):
            i = hd * nc + ch
            sb = sts[hd].astype(BF16)
            outs[i] = _dot(r2[i], sb) + c2[i]
            sts[hd] = sts[hd] * p_end_t[i] + (_dot(gmat_t[i], sb) + c3_t[i])
    for hd in range(nh):
        st_ref[hd] = sts[hd]
    out = jnp.stack(outs, axis=0)

    mean = jnp.mean(out, axis=-1, keepdims=True)
    var = jnp.mean(jnp.square(out - mean), axis=-1, keepdims=True)
    y = (out - mean) * lax.rsqrt(var + B_LN_EPS) * head_row(3) + head_row(4)
    y = (y + jnp.sum(r * km * head_row(2), axis=-1, keepdims=True) * v) * g
    for hd in range(nh):
        o_ref[0, :, hd * B_HEAD:(hd + 1) * B_HEAD] = y[hd * nc:(hd + 1) * nc].reshape(tblk, B_HEAD).astype(o_ref.dtype)


def _rwkv_scan(rkv, ld, a, g, par, *, tblk=512, hw=2 * LANES):
    b, s, _ = ld.shape
    tblk = min(tblk, s)
    npair = B_WIDTH // hw
    blk = (1, tblk, hw)
    return pl.pallas_call(
        _rwkv_scan_body,
        grid=(b, npair, s // tblk),
        in_specs=[pl.BlockSpec(blk, lambda bi, hp, t: (bi, t, hp)),
                  pl.BlockSpec(blk, lambda bi, hp, t: (bi, t, npair + hp)),
                  pl.BlockSpec(blk, lambda bi, hp, t: (bi, t, 2 * npair + hp)),
                  pl.BlockSpec(blk, lambda bi, hp, t: (bi, t, hp)),
                  pl.BlockSpec(blk, lambda bi, hp, t: (bi, t, hp)),
                  pl.BlockSpec(blk, lambda bi, hp, t: (bi, t, hp)),
                  pl.BlockSpec((8, hw), lambda bi, hp, t: (0, hp))],
        out_specs=pl.BlockSpec(blk, lambda bi, hp, t: (bi, t, hp)),
        out_shape=jax.ShapeDtypeStruct((b, s, B_WIDTH), BF16),
        scratch_shapes=[pltpu.VMEM((hw // B_HEAD, B_HEAD, B_HEAD), F32)],
        compiler_params=_cparams(("parallel", "parallel", "arbitrary")),
        name="rwkv_scan",
    )(rkv, rkv, rkv, ld, a, g, par)


S5_CHUNK = LANES
S5_ROWS = S5_CHUNK * C_GROUP


def _s5_group_body(lam_c_ref, lam_r_ref, bt_ref, c_ref, ct_ref, u_ref, y_ref,
                   kv_ref, m_ref, w_ref, v_ref, acc_ref, us_ref):
    L = S5_CHUNK
    n_chunk, bsz = u_ref.shape[0], u_ref.shape[1]
    rows = n_chunk * bsz

    def zoh(lre, lim, dt):
        lre = jnp.minimum(lre, -1e-4)
        rho, th = lre * dt, lim * dt
        mag = jnp.exp(rho)
        lbr, lbi = mag * jnp.cos(th), mag * jnp.sin(th)
        den = 1.0 / (lre * lre + lim * lim)
        cr = ((lbr - 1.0) * lre + lbi * lim) * den
        cim = (lbi * lre - (lbr - 1.0) * lim) * den
        return rho, th, cr, cim

    lc = lam_c_ref[0]
    rho_c, th_c, _, _ = zoh(lc[:, 0:1], lc[:, 1:2], jnp.exp(lc[:, 2:3]))
    lr = lam_r_ref[0]
    rho_r, th_r, cr_r, ci_r = zoh(lr[0:1], lr[1:2], jnp.exp(lr[2:3]))

    def powers(rho, th, n):
        mag = jnp.exp(rho * n)
        return mag * jnp.cos(th * n), mag * jnp.sin(th * n)

    btr, bti = bt_ref[0, 0], bt_ref[0, 1]
    bbr = btr * cr_r - bti * ci_r
    bbi = btr * ci_r + bti * cr_r
    c_re, c_im = c_ref[0, 0], c_ref[0, 1]

    d_re = jnp.concatenate([c_re * bbr[j:j + 1] - c_im * bbi[j:j + 1] for j in range(C_GROUP)], axis=0)
    d_im = jnp.concatenate([c_re * bbi[j:j + 1] + c_im * bbr[j:j + 1] for j in range(C_GROUP)], axis=0)
    tau = lax.broadcasted_iota(I32, (C_STATE, L), 1).astype(F32)
    e_re, e_im = powers(rho_c, th_c, tau)
    kvec = _dot3(d_re, e_re) - _dot3(d_im, e_im)

    kv_ref[...] = kvec

    back = (L - 1 - lax.broadcasted_iota(I32, (L, C_STATE), 0)).astype(F32)
    f_re, f_im = powers(rho_r, th_r, back)
    for j in range(C_GROUP):
        br, bi = bbr[j:j + 1], bbi[j:j + 1]
        w_ref[j * L:(j + 1) * L, :] = jnp.concatenate(
            [f_re * br - f_im * bi, f_re * bi + f_im * br], axis=1).astype(w_ref.dtype)

    e1_re, e1_im = powers(rho_c, th_c, tau + 1.0)
    ctr, cti = ct_ref[0, 0], ct_ref[0, 1]
    for i in range(C_GROUP):
        cr_i, ci_i = ctr[:, i:i + 1], cti[:, i:i + 1]
        v_ref[0:C_STATE, i * L:(i + 1) * L] = (cr_i * e1_re - ci_i * e1_im).astype(v_ref.dtype)
        v_ref[C_STATE:2 * C_STATE, i * L:(i + 1) * L] = (-(cr_i * e1_im + ci_i * e1_re)).astype(v_ref.dtype)

    for j in range(C_GROUP):
        us_ref[j] = u_ref[:, :, j, :].reshape(rows, L)
    u = jnp.concatenate([us_ref[j] for j in range(C_GROUP)], axis=1)
    x = _dot(u, w_ref[...])

    def shift_chunks(a, d):
        return jnp.concatenate([jnp.zeros((d * bsz, 2 * C_STATE), F32), a[:rows - d * bsz]], axis=0)

    d = 1
    while d < n_chunk:
        pr, pi = powers(rho_r, th_r, float(L * d))
        prev = shift_chunks(x, d)
        x = x + (prev * jnp.concatenate([pr, pr], axis=1)
                 + pltpu.roll(prev, C_STATE, 1) * jnp.concatenate([-pi, pi], axis=1))
        d *= 2
    x_in = shift_chunks(x, 1)
    acc_ref[...] = _dot(x_in.astype(BF16), v_ref[...])

    upper = lax.broadcasted_iota(I32, (L, L), 1) >= lax.broadcasted_iota(I32, (L, L), 0)

    def block_rows(jj, carry):
        for half in range(2):
            j = 2 * jj + half
            for i in range(C_GROUP):
                kv = jnp.broadcast_to(kv_ref[pl.ds(j * C_GROUP + i, 1), :], (L, L))
                blk = jnp.where(upper, pltpu.roll(kv, 0, 1, stride=1, stride_axis=0), 0.0)
                m_ref[half * L:(half + 1) * L, i * L:(i + 1) * L] = blk.astype(m_ref.dtype)
        uj = jnp.concatenate([us_ref[2 * jj], us_ref[2 * jj + 1]], axis=1)
        acc_ref[...] += _dot(uj, m_ref[...])
        return carry

    lax.fori_loop(0, C_GROUP // 2, block_rows, 0)
    for i in range(C_GROUP):
        y_ref[:, :, i, :] = acc_ref[:, i * L:(i + 1) * L].reshape(n_chunk, bsz, L)


def _s5_groups(u4, lam_c, lam_r, bt, c, ct):
    n_chunk, bsz, width, _ = u4.shape
    g = width // C_GROUP
    rows = n_chunk * bsz
    i3 = lambda gi: (gi, 0, 0)
    i4 = lambda gi: (gi, 0, 0, 0)
    blk = pl.BlockSpec((n_chunk, bsz, C_GROUP, S5_CHUNK), lambda gi: (0, 0, gi, 0))
    return pl.pallas_call(
        _s5_group_body,
        grid=(g,),
        in_specs=[pl.BlockSpec((1, C_STATE, 8), i3),
                  pl.BlockSpec((1, 8, C_STATE), i3),
                  pl.BlockSpec((1, 2, C_GROUP, C_STATE), i4),
                  pl.BlockSpec((1, 2, C_GROUP, C_STATE), i4),
                  pl.BlockSpec((1, 2, C_STATE, C_GROUP), i4),
                  blk],
        out_specs=blk,
        out_shape=jax.ShapeDtypeStruct(u4.shape, F32),
        scratch_shapes=[pltpu.VMEM((C_GROUP * C_GROUP, S5_CHUNK), F32),
                        pltpu.VMEM((2 * S5_CHUNK, S5_ROWS), BF16),
                        pltpu.VMEM((S5_ROWS, 2 * C_STATE), BF16),
                        pltpu.VMEM((2 * C_STATE, S5_ROWS), BF16),
                        pltpu.VMEM((rows, S5_ROWS), F32),
                        pltpu.VMEM((C_GROUP, rows, S5_CHUNK), BF16)],
        compiler_params=_cparams(("parallel",)),
        name="s5_groups",
    )(lam_c, lam_r, bt, c, ct, u4)


def _s5_in_body(x_ref, g_ref, w_ref, u_ref, u4_ref):
    xs = _rms(x_ref[...], g_ref[...]).astype(BF16)
    u = _dot(xs, w_ref[...])
    u_ref[...] = u
    for k in range(u4_ref.shape[0]):
        u4_ref[k, 0] = u[k * S5_CHUNK:(k + 1) * S5_CHUNK, :].T.astype(BF16)


def _s5_in(x2, g, w, bsz, seq, *, tm=512):
    m, d = x2.shape
    n = w.shape[1]
    tm = min(tm, seq)
    tiles_per_seq = seq // tm
    ck = tm // S5_CHUNK
    return pl.pallas_call(
        _s5_in_body,
        grid=(m // tm,),
        in_specs=[pl.BlockSpec((tm, d), lambda i: (i, 0)),
                  pl.BlockSpec((1, d), lambda i: (0, 0)),
                  pl.BlockSpec((d, n), lambda i: (0, 0))],
        out_specs=[pl.BlockSpec((tm, n), lambda i: (i, 0)),
                   pl.BlockSpec((ck, 1, n, S5_CHUNK), lambda i: (i % tiles_per_seq, i // tiles_per_seq, 0, 0))],
        out_shape=[jax.ShapeDtypeStruct((m, n), F32),
                   jax.ShapeDtypeStruct((seq // S5_CHUNK, bsz, n, S5_CHUNK), BF16)],
        compiler_params=_cparams(("parallel",)),
        name="odd_in",
    )(x2, g, w)


def _pad_cols(w, width):
    return jnp.pad(w, ((0, 0), (0, width - w.shape[1])))


def _pad_rows(w, height):
    return jnp.pad(w, ((0, height - w.shape[0]), (0, 0)))


def _even_in_weights(w_in):
    o = 0
    pieces = {}
    for name, width in (("cq", A_Q_RANK), ("ckv", A_KV_RANK), ("krope", A_ROPE_DIM), ("kidx", IDX_DIM),
                        ("widx", IDX_HEADS), ("rkv", 3 * B_WIDTH), ("wl", B_DECAY_LORA),
                        ("al", B_A_LORA), ("gl", B_GATE_LORA)):
        pieces[name] = w_in[:, o:o + width]
        o += width
    w_a = jnp.concatenate([pieces["cq"], pieces["ckv"], _pad_cols(pieces["krope"], LANES),
                           _pad_cols(pieces["kidx"], LANES), _pad_cols(pieces["widx"], LANES)], axis=1)
    w_b = jnp.concatenate([pieces["rkv"], _pad_cols(pieces["wl"], LANES), _pad_cols(pieces["al"], LANES),
                           _pad_cols(pieces["gl"], 2 * LANES)], axis=1)
    return w_a.astype(BF16), w_b.astype(BF16)


def _pad_mu(mu):
    o = 3 * B_WIDTH
    wl = mu[o:o + B_DECAY_LORA]
    al = mu[o + B_DECAY_LORA:o + B_DECAY_LORA + B_A_LORA]
    gl = mu[o + B_DECAY_LORA + B_A_LORA:]
    z = lambda n: jnp.zeros((n,), F32)
    return jnp.concatenate([mu[:o], wl, z(LANES - B_DECAY_LORA), al, z(LANES - B_A_LORA),
                            gl, z(2 * LANES - B_GATE_LORA)])[None, :]


def _dsa_mixer(h_a3, start_frame, cq_norm, ckv_norm, kidx_norm, w_uq, w_uk, w_uv, w_qidx):
    wuq = w_uq.reshape(A_Q_RANK, A_HEADS * A_HEAD_DIM).astype(BF16)
    wuk_h = jnp.pad(jnp.transpose(w_uk, (1, 0, 2)), ((0, 0), (0, 0), (A_ROPE_DIM, 0))).astype(BF16)
    wqi = jnp.pad(w_qidx, ((0, 0), (0, 0), (0, LANES - IDX_DIM))).reshape(A_Q_RANK, IDX_HEADS * LANES).astype(BF16)
    wuv = jnp.transpose(w_uv, (1, 0, 2)).astype(BF16)
    qcat, kcat, kfull, qidx, kidx, widx = _dsa_proj(
        h_a3, start_frame, _rope_tables(), cq_norm[None, :], ckv_norm[None, :],
        _pad_cols(kidx_norm[None, :], LANES), wuq, wuk_h, wqi)
    return _dsa_attend(qidx, widx, kidx, qcat, kcat, kfull, wuv)


def _rwkv_mixer(h_b3, mu, w0, w2, a0, a2, g2, k_k, k_a, r_k, ln_w, ln_b):
    rkv, ld, a, g = _rwkv_prep(h_b3, _pad_mu(mu), w0[None, :], a0[None, :],
                               _pad_rows(w2, LANES).astype(BF16), _pad_rows(a2, LANES).astype(BF16),
                               _pad_rows(g2, 2 * LANES).astype(BF16))
    par = jnp.concatenate([k_k[None, :], k_a[None, :], r_k.reshape(1, B_WIDTH), ln_w[None, :], ln_b[None, :],
                           jnp.zeros((3, B_WIDTH), F32)], axis=0)
    return _rwkv_scan(rkv, ld, a, g, par)


def _even_mixer(x2, bsz, seq, start_frame, g_mix, w_in, w_out, cq_norm, ckv_norm, kidx_norm, w_uq, w_uk, w_uv,
                w_qidx, mu, w0, w2, a0, a2, g2, k_k, k_a, r_k, ln_w, ln_b):
    w_a, w_b = _even_in_weights(w_in)
    rms_pro = lambda x, g: _rms(x, g)
    h_a = _matmul([x2], [g_mix], w_a, [], prologue=rms_pro, epilogue=_epi_id, out_dtype=F32,
                  tm=512, tn=A_PAD_COLS, name="even_in_a")
    h_b = _matmul([x2], [g_mix], w_b, [], prologue=rms_pro, epilogue=_epi_id, out_dtype=F32,
                  tm=512, tn=B_PAD_COLS // 2, name="even_in_b")
    y_a = _dsa_mixer(h_a.reshape(bsz, seq, A_PAD_COLS), start_frame, cq_norm, ckv_norm, kidx_norm,
                     w_uq, w_uk, w_uv, w_qidx)
    y_b = _rwkv_mixer(h_b.reshape(bsz, seq, B_PAD_COLS), mu, w0, w2, a0, a2, g2, k_k, k_a, r_k, ln_w, ln_b)

    m = bsz * seq
    cat_pro = lambda ya, yb: jnp.concatenate([ya, yb], axis=-1)
    return _matmul([y_a.reshape(m, A_WIDTH), y_b.reshape(m, B_WIDTH)], [], w_out.astype(BF16), [(x2, "tile")],
                   prologue=cat_pro, epilogue=_epi_add, out_dtype=F32, tm=512, tn=D_MODEL, name="even_out")


def _gelu_tanh(y):
    return 0.5 * y * (1.0 + jnp.tanh(math.sqrt(2.0 / math.pi) * (y + 0.044715 * (y * y * y))))


def _odd_mixer(x2, bsz, seq, g_mix, w_in, w_out, lam_re, lam_im, log_dt, b_re, b_im, c_re, c_im, d_skip,
               w_glu, b_glu):
    u, u4 = _s5_in(x2, g_mix, w_in.astype(BF16), bsz, seq)
    gated = _s5_mixer(u, u4, lam_re, lam_im, log_dt, b_re, b_im, c_re, c_im, d_skip, w_glu, b_glu)
    return _matmul([gated], [], w_out.astype(BF16), [(x2, "tile")], prologue=_pro_id, epilogue=_epi_add,
                   out_dtype=F32, tm=512, tn=D_MODEL, name="odd_out")


def _s5_mixer(u, u4, lam_re, lam_im, log_dt, b_re, b_im, c_re, c_im, d_skip, w_glu, b_glu):
    n_chunk, bsz = u4.shape[0], u4.shape[1]
    m = u.shape[0]
    assert bsz % 8 == 0, "chunk rows are addressed as whole sublane tiles"

    ldt = jnp.broadcast_to(log_dt[:, None], lam_re.shape)
    lam_r = jnp.pad(jnp.stack([lam_re, lam_im, ldt], axis=1), ((0, 0), (0, 5), (0, 0)))
    lam_c = jnp.transpose(lam_r, (0, 2, 1))
    bt = jnp.stack([jnp.transpose(b_re, (0, 2, 1)), jnp.transpose(b_im, (0, 2, 1))], axis=1)
    c = jnp.stack([c_re, c_im], axis=1)
    ct = jnp.stack([jnp.transpose(c_re, (0, 2, 1)), jnp.transpose(c_im, (0, 2, 1))], axis=1)
    y4 = _s5_groups(u4, lam_c, lam_r, bt, c, ct)
    y = jnp.transpose(y4, (1, 0, 3, 2)).reshape(m, C_WIDTH)

    z_of = lambda yy, uu, dd: _gelu_tanh(yy + dd * uu)
    return _matmul([y, u], [d_skip[None, :]], w_glu.astype(BF16),
                   [(None, 0), (None, 1), (d_skip[None, :], "row"), (b_glu[None, :], "row")],
                   prologue=z_of,
                   epilogue=lambda acc, yy, uu, dd, bb: z_of(yy, uu, dd) * jax.nn.sigmoid(acc + bb),
                   out_dtype=BF16, tm=512, tn=C_WIDTH, name="s5_glu")


def kernel(x, mem, start_frame, norm_mix, norm_xattn, norm_mem, norm_ffn, final_norm, xattn_wq, xattn_wkv, xattn_wo, ffn_up, ffn_down, even_w_in, even_w_out, dsa_cq_norm, dsa_ckv_norm, dsa_kidx_norm, dsa_w_uq, dsa_w_uk, dsa_w_uv, dsa_w_qidx, rwkv_mu, rwkv_w0, rwkv_w2, rwkv_a0, rwkv_a2, rwkv_g2, rwkv_k_k, rwkv_k_a, rwkv_r_k, rwkv_ln_w, rwkv_ln_b, odd_w_in, odd_w_out, s5_lam_re, s5_lam_im, s5_log_dt, s5_b_re, s5_b_im, s5_c_re, s5_c_im, s5_d, s5_w_glu, s5_b_glu):
    bsz, seq, d = x.shape
    depth = norm_mix.shape[0]
    m = bsz * seq
    x2 = x.reshape(m, d)
    mem2 = mem.reshape(bsz * mem.shape[1], d)
    for layer in range(depth):
        i = layer // 2
        g_mix = norm_mix[layer][None, :]
        if layer % 2 == 0:
            x2 = _even_mixer(x2, bsz, seq, start_frame, g_mix, even_w_in[i], even_w_out[i], dsa_cq_norm[i],
                             dsa_ckv_norm[i], dsa_kidx_norm[i], dsa_w_uq[i], dsa_w_uk[i], dsa_w_uv[i],
                             dsa_w_qidx[i], rwkv_mu[i], rwkv_w0[i], rwkv_w2[i], rwkv_a0[i], rwkv_a2[i],
                             rwkv_g2[i], rwkv_k_k[i], rwkv_k_a[i], rwkv_r_k[i], rwkv_ln_w[i], rwkv_ln_b[i])
        else:
            x2 = _odd_mixer(x2, bsz, seq, g_mix, odd_w_in[i], odd_w_out[i], s5_lam_re[i], s5_lam_im[i],
                            s5_log_dt[i], s5_b_re[i], s5_b_im[i], s5_c_re[i], s5_c_im[i], s5_d[i],
                            s5_w_glu[i], s5_b_glu[i])
        kv = _matmul([mem2], [norm_mem[layer][None, :]], xattn_wkv[layer].astype(BF16), [],
                     prologue=lambda a, g: _rms(a, g), epilogue=_epi_id, out_dtype=BF16, tm=512, tn=1024,
                     name="xattn_kv")
        x2 = _xattn(x2.reshape(bsz, seq, d), norm_xattn[layer][None, :], xattn_wq[layer].astype(BF16),
                    kv.reshape(bsz, mem.shape[1], -1), xattn_wo[layer].astype(BF16)).reshape(m, d)
        x2 = _ffn(x2, norm_ffn[layer][None, :], ffn_up[layer].astype(BF16), ffn_down[layer].astype(BF16),
                  final_norm[None, :] if layer == depth - 1 else None)
    return x2.reshape(bsz, seq, d)
```
